```python
import math
import jax, jax.numpy as jnp
from jax import lax
import numpy as np

D_MODEL = 1024
BATCH = 2
SEQ = 8192
DEPTH = 2
DEC_BATCH = 128
DEC_SEQ = 4
PAST_LEN = 2048
PAGE_SIZE = 128

N_EVEN = (DEPTH + 1) // 2
N_ODD = DEPTH // 2
H_A = D_MODEL // 256
DK_A = 128
DV_A = 128
F_A = H_A * DK_A
W_A = H_A * DV_A
CHUNK_A = 64
H_B = D_MODEL // 128
HD_B = 64
W_B = H_B * HD_B
MOBA_BLOCK = 256
MOBA_TOPK = 3
MOBA_Q_BLOCK = 64
W_C = D_MODEL // 2
NB_C = 8
BW_C = W_C // NB_C
CONV_W = 4
RG_C = 8.0
H_D = D_MODEL // 128
HD_D = 64
W_D = H_D * HD_D
FOX_Q_BLOCK = 128
D_FF = 256 * ((8 * D_MODEL // 3 + 255) // 256)
N_EXPERTS = 8
TOP_K = 2
D_FF_E = 7 * D_MODEL // 2
ROPE_THETA = 10000.0
EPS = 1e-6
NEG_INF = -1e30
COLS_E = 2 * F_A + 2 * W_A + 3 * W_B
COLS_O = 2 * W_C + 3 * W_D + H_D

kernel_name = "hybrid_hgrn2_moba_rglru_fox_step"


def _offsets(sizes):
    out, acc = [], 0
    for s in sizes[:-1]:
        acc += s
        out.append(acc)
    return out


def rmsnorm(x, g):
    x32 = x.astype(jnp.float32)
    y = x32 * lax.rsqrt(jnp.mean(x32 * x32, axis=-1, keepdims=True) + EPS) * g.astype(jnp.float32)
    return y.astype(x.dtype)


def rope(x, pos):
    half = x.shape[-1] // 2
    inv = ROPE_THETA ** (-jnp.arange(half, dtype=jnp.float32) / half)
    ang = pos.astype(jnp.float32)[:, None] * inv[None, :]
    cos = jnp.cos(ang)[None, :, None, :]
    sin = jnp.sin(ang)[None, :, None, :]
    x32 = x.astype(jnp.float32)
    x1, x2 = x32[..., :half], x32[..., half:]
    return jnp.concatenate([x1 * cos - x2 * sin, x2 * cos + x1 * sin], axis=-1).astype(x.dtype)


def swiglu(h, w1, w3, w2):
    return (jax.nn.silu(h @ w1) * (h @ w3)) @ w2


def moe_swiglu(h, router, w1, w3, w2):
    N, T, D = h.shape
    x = h.reshape(N * T, D)
    logits = (x @ router).astype(jnp.float32)
    top_v, top_i = lax.top_k(logits, TOP_K)
    gates = jax.nn.softmax(top_v, axis=-1)
    comb = jnp.sum(jax.nn.one_hot(top_i, N_EXPERTS, dtype=jnp.float32) * gates[..., None], axis=1)
    y = jnp.zeros((N * T, D), jnp.float32)
    for e in range(N_EXPERTS):
        y = y + comb[:, e:e + 1] * swiglu(x, w1[e], w3[e], w2[e]).astype(jnp.float32)
    return y.astype(h.dtype).reshape(N, T, D)


def hgrn2_chunked(q, k, v, logf, s0):
    N, T, H, DK = q.shape
    DV = v.shape[-1]
    C = min(CHUNK_A, T)
    nc = -(-T // C)
    pad = nc * C - T

    def prep(a):
        a = jnp.pad(a, ((0, 0), (0, pad), (0, 0), (0, 0)))
        return a.reshape(N, nc, C, H, a.shape[-1]).swapaxes(0, 1)

    tri = jnp.tril(jnp.ones((C, C), bool))

    def step(S, inp):
        qc, kc, vc, gc = inp
        G = jnp.cumsum(gc, axis=1)
        o_inter = jnp.einsum('nchk,nhkv->nchv', qc * jnp.exp(G), S)
        diff = G[:, :, None] - G[:, None, :]
        decay = jnp.exp(jnp.where(tri[None, :, :, None, None], diff, -jnp.inf))
        scores = jnp.einsum('nthk,nshk,ntshk->nths', qc, kc, decay)
        o_intra = jnp.einsum('nths,nshv->nthv', scores, vc)
        G_last = G[:, -1]
        S_new = jnp.exp(G_last)[..., None] * S + jnp.einsum('nshk,nshv->nhkv', kc * jnp.exp(G_last[:, None] - G), vc)
        return S_new, o_inter + o_intra

    S_fin, o = lax.scan(step, s0, (prep(q), prep(k), prep(v), prep(logf)))
    o = o.swapaxes(0, 1).reshape(N, nc * C, H, DV)[:, :T]
    return o, S_fin


def moba_blocks(k, v):
    N, L, H, D = k.shape
    nb = -(-L // MOBA_BLOCK)
    padw = ((0, 0), (0, nb * MOBA_BLOCK - L), (0, 0), (0, 0))
    kb = jnp.pad(k, padw).reshape(N, nb, MOBA_BLOCK, H, D)
    vb = jnp.pad(v, padw).reshape(N, nb, MOBA_BLOCK, H, D)
    means = jnp.mean(kb.astype(jnp.float32), axis=2)
    return means, kb.transpose(0, 3, 1, 2, 4), vb.transpose(0, 3, 1, 2, 4)


def moba_core(q, means, kb, vb, pos):
    N, Tq, H, D = q.shape
    NB = means.shape[1]
    own = pos // MOBA_BLOCK
    gate = jnp.einsum('nthd,nbhd->nthb', q.astype(jnp.float32), means)
    cand = jnp.arange(NB)[None, :] < own[:, None]
    gate = jnp.where(cand[None, :, None, :], gate, -jnp.inf)
    n_sel = min(MOBA_TOPK, NB)
    _, top_idx = lax.top_k(gate, n_sel)
    own_idx = jnp.broadcast_to(own[None, :, None, None], (N, Tq, H, 1)).astype(top_idx.dtype)
    idx = jnp.concatenate([top_idx, own_idx], axis=-1).transpose(0, 2, 1, 3)
    slot_ok = jnp.concatenate([jnp.arange(n_sel)[None, :] < own[:, None], jnp.ones((Tq, 1), bool)], axis=-1)
    take = jax.vmap(jax.vmap(lambda blocks, ix: blocks[ix]))
    k_sel = take(kb, idx)
    v_sel = take(vb, idx)
    logits = jnp.einsum('nthd,nhtsjd->nhtsj', q, k_sel).astype(jnp.float32) * (D ** -0.5)
    key_pos = idx[..., None] * MOBA_BLOCK + jnp.arange(MOBA_BLOCK)
    ok = slot_ok[None, None, :, :, None] & (key_pos <= pos[None, None, :, None, None])
    logits = jnp.where(ok, logits, NEG_INF)
    S = idx.shape[-1]
    p = jax.nn.softmax(logits.reshape(N, H, Tq, S * MOBA_BLOCK), axis=-1).reshape(logits.shape)
    return jnp.einsum('nhtsj,nhtsjd->nthd', p.astype(v_sel.dtype), v_sel)


def moba_prompt(q, k, v):
    N, T, H, D = q.shape
    means, kb, vb = moba_blocks(k, v)
    qb = math.gcd(T, MOBA_Q_BLOCK)
    nq = T // qb
    qs = q.reshape(N, nq, qb, H, D).swapaxes(0, 1)
    ps = jnp.arange(T, dtype=jnp.int32).reshape(nq, qb)
    out = lax.map(lambda a: moba_core(a[0], means, kb, vb, a[1]), (qs, ps))
    return out.swapaxes(0, 1).reshape(N, T, H, D)


def moba_sample(q, k, v, pool_k, pool_v, page_table):
    T = q.shape[1]

    def one(a):
        qi, ki, vi, pt = a
        kp = pool_k[pt].reshape(-1, H_B, HD_B)
        vp = pool_v[pt].reshape(-1, H_B, HD_B)
        past = kp.shape[0]
        k_all = jnp.concatenate([kp, ki.astype(kp.dtype)], axis=0)[None]
        v_all = jnp.concatenate([vp, vi.astype(vp.dtype)], axis=0)[None]
        means, kb, vb = moba_blocks(k_all, v_all)
        return moba_core(qi[None], means, kb, vb, past + jnp.arange(T, dtype=jnp.int32))[0]

    return lax.map(one, (q, k, v, page_table))


def fox_core(q, k, v, cq, ck, qpos, kpos):
    D = q.shape[-1]
    logits = jnp.einsum('nthd,nshd->nhts', q, k).astype(jnp.float32) * (D ** -0.5)
    logits = logits + cq.transpose(0, 2, 1)[..., None] - ck.transpose(0, 2, 1)[:, :, None, :]
    causal = kpos[None, :] <= qpos[:, None]
    p = jax.nn.softmax(jnp.where(causal, logits, NEG_INF), axis=-1)
    return jnp.einsum('nhts,nshd->nthd', p.astype(v.dtype), v)


def fox_prompt(q, k, v, logf):
    N, T, H, D = q.shape
    c = jnp.cumsum(logf, axis=1)
    qb = math.gcd(T, FOX_Q_BLOCK)
    nq = T // qb
    qs = q.reshape(N, nq, qb, H, D).swapaxes(0, 1)
    cs = c.reshape(N, nq, qb, H).swapaxes(0, 1)
    ps = jnp.arange(T, dtype=jnp.int32).reshape(nq, qb)
    kpos = jnp.arange(T, dtype=jnp.int32)
    out = lax.map(lambda a: fox_core(a[0], k, v, a[1], c, a[2], kpos), (qs, cs, ps))
    return out.swapaxes(0, 1).reshape(N, T, H, D)


def fox_sample(q, k, v, logf, pool_k, pool_v, pool_lf, page_table):
    T = q.shape[1]

    def one(a):
        qi, ki, vi, lfi, pt = a
        kp = pool_k[pt].reshape(-1, H_D, HD_D)
        vp = pool_v[pt].reshape(-1, H_D, HD_D)
        lfp = pool_lf[pt].reshape(-1, H_D).astype(jnp.float32)
        past = kp.shape[0]
        k_all = jnp.concatenate([kp, ki.astype(kp.dtype)], axis=0)
        v_all = jnp.concatenate([vp, vi.astype(vp.dtype)], axis=0)
        c = jnp.cumsum(jnp.concatenate([lfp, lfi], axis=0), axis=0)
        qpos = past + jnp.arange(T, dtype=jnp.int32)
        kpos = jnp.arange(past + T, dtype=jnp.int32)
        return fox_core(qi[None], k_all[None], v_all[None], c[past:][None], c[None], qpos, kpos)[0]

    return lax.map(one, (q, k, v, logf, page_table))


def causal_dwconv(xp, w, b):
    T = xp.shape[1] - (CONV_W - 1)
    acc = xp[:, 0:T] * w[0]
    for j in range(1, CONV_W):
        acc = acc + xp[:, j:j + T] * w[j]
    return acc + b


def block_diag(x, w):
    N, T, _ = x.shape
    return jnp.einsum('ntgi,gij->ntgj', x.reshape(N, T, NB_C, BW_C), w.astype(jnp.float32)).reshape(N, T, W_C)


def rglru_scan(a, b, h0):
    b = b.at[:, 0].add(a[:, 0] * h0)

    def combine(c1, c2):
        a1, b1 = c1
        a2, b2 = c2
        return a1 * a2, a2 * b1 + b2

    _, h = lax.associative_scan(combine, (a, b), axis=1)
    return h, h[:, -1]


def even_mixer(h, pos, s0, lb, w_in, gnorm, w_out, past):
    N, T, _ = h.shape
    f32 = jnp.float32
    aq, af, ai, ag, bq, bk, bv = jnp.split(h @ w_in, _offsets([F_A, F_A, W_A, W_A, W_B, W_B, W_B]), axis=-1)
    lbh = lb.reshape(H_A, DK_A)
    zf = af.astype(f32).reshape(N, T, H_A, DK_A)
    q = jax.nn.silu(aq.astype(f32)).reshape(N, T, H_A, DK_A)
    logf = jnp.logaddexp(jnp.log(lbh), jnp.log1p(-lbh) + jax.nn.log_sigmoid(zf))
    k = (1.0 - lbh) * jax.nn.sigmoid(-zf)
    v = ai.astype(f32).reshape(N, T, H_A, DV_A)
    o_a, s_new = hgrn2_chunked(q, k, v, logf, s0.astype(f32))
    o_a = rmsnorm(o_a.reshape(N, T, W_A), gnorm) * jax.nn.sigmoid(ag.astype(f32))
    qb = rope(bq.reshape(N, T, H_B, HD_B), pos)
    kb = rope(bk.reshape(N, T, H_B, HD_B), pos)
    vb = bv.reshape(N, T, H_B, HD_B)
    o_b = moba_prompt(qb, kb, vb) if past is None else moba_sample(qb, kb, vb, *past)
    mix = jnp.concatenate([o_a.astype(h.dtype), o_b.reshape(N, T, W_B).astype(h.dtype)], axis=-1)
    return mix @ w_out, s_new, kb, vb


def odd_mixer(h, pos, conv_buf, h0, w_in, cw, cb, wa, ba, wx, bx, lam, bf, w_out, past):
    N, T, _ = h.shape
    f32 = jnp.float32
    cx, cg, dq, dk, dv, df = jnp.split(h @ w_in, _offsets([W_C, W_C, W_D, W_D, W_D, H_D]), axis=-1)
    xp = jnp.concatenate([conv_buf.astype(cx.dtype), cx], axis=1)
    new_buf = xp[:, T:]
    xc = causal_dwconv(xp, cw, cb).astype(f32)
    r = jax.nn.sigmoid(block_diag(xc, wa) + ba.astype(f32))
    i = jax.nn.sigmoid(block_diag(xc, wx) + bx.astype(f32))
    log_a = -RG_C * r * jax.nn.softplus(-lam.astype(f32))
    mult = jnp.where((pos == 0)[None, :, None], 1.0, jnp.sqrt(-jnp.expm1(2.0 * log_a)))
    hseq, h_last = rglru_scan(jnp.exp(log_a), xc * i * mult, h0.astype(f32))
    o_c = hseq * jax.nn.gelu(cg.astype(f32))
    q = dq.reshape(N, T, H_D, HD_D)
    k = dk.reshape(N, T, H_D, HD_D)
    v = dv.reshape(N, T, H_D, HD_D)
    logf = jax.nn.log_sigmoid(df.astype(f32) + bf.astype(f32))
    o_d = fox_prompt(q, k, v, logf) if past is None else fox_sample(q, k, v, logf, *past)
    mix = jnp.concatenate([o_c.astype(h.dtype), o_d.reshape(N, T, W_D).astype(h.dtype)], axis=-1)
    return mix @ w_out, new_buf, h_last, k, v, logf


def setup_inputs(seed: int = 0) -> dict:
    key = jax.random.key(seed)
    ks = list(jax.random.split(key, 40))
    f32 = jnp.float32

    def nrm(i, shape, scale):
        return scale * jax.random.normal(ks[i], shape, f32)

    n_pages = PAST_LEN // PAGE_SIZE
    n_used = DEC_BATCH * n_pages
    n_pool = n_used + max(1, n_used // 4)
    page_table = jax.random.permutation(ks[0], n_pool)[:n_used].reshape(DEC_BATCH, n_pages).astype(jnp.int32)
    u = jax.random.uniform(ks[1], (N_ODD, W_C), f32, 0.9, 0.999)
    return {
        "x_prompt": nrm(2, (BATCH, SEQ, D_MODEL), 1.0),
        "x_sample": nrm(3, (DEC_BATCH, DEC_SEQ, D_MODEL), 1.0),
        "cache_k_e": nrm(4, (N_EVEN, n_pool, PAGE_SIZE, H_B, HD_B), 1.0),
        "cache_v_e": nrm(5, (N_EVEN, n_pool, PAGE_SIZE, H_B, HD_B), 1.0),
        "state_s_e": nrm(6, (N_EVEN, DEC_BATCH, H_A, DK_A, DV_A), 0.3),
        "state_conv_o": nrm(7, (N_ODD, DEC_BATCH, CONV_W - 1, W_C), 1.0),
        "state_h_o": nrm(8, (N_ODD, DEC_BATCH, W_C), 0.5),
        "cache_k_o": nrm(9, (N_ODD, n_pool, PAGE_SIZE, H_D, HD_D), 1.0),
        "cache_v_o": nrm(10, (N_ODD, n_pool, PAGE_SIZE, H_D, HD_D), 1.0),
        "cache_logf_o": jax.nn.log_sigmoid(3.0 + nrm(11, (N_ODD, n_pool, PAGE_SIZE, H_D), 1.0)),
        "page_table": page_table,
        "w_in_e": nrm(12, (N_EVEN, D_MODEL, COLS_E), D_MODEL ** -0.5),
        "lb_logits": nrm(13, (N_EVEN + 1, F_A), 0.1).at[0].add(-2.0),
        "gnorm_a": 1.0 + nrm(14, (N_EVEN, W_A), 0.05),
        "w_out_e": nrm(15, (N_EVEN, W_A + W_B, D_MODEL), (W_A + W_B) ** -0.5),
        "ffn_w1": nrm(16, (N_EVEN, D_MODEL, D_FF), D_MODEL ** -0.5),
        "ffn_w3": nrm(17, (N_EVEN, D_MODEL, D_FF), D_MODEL ** -0.5),
        "ffn_w2": nrm(18, (N_EVEN, D_FF, D_MODEL), D_FF ** -0.5),
        "w_in_o": nrm(19, (N_ODD, D_MODEL, COLS_O), D_MODEL ** -0.5),
        "conv_w": nrm(20, (N_ODD, CONV_W, W_C), CONV_W ** -0.5),
        "conv_b": nrm(21, (N_ODD, W_C), 0.01),
        "rg_wa": nrm(22, (N_ODD, NB_C, BW_C, BW_C), BW_C ** -0.5),
        "rg_ba": nrm(23, (N_ODD, W_C), 0.01),
        "rg_wx": nrm(24, (N_ODD, NB_C, BW_C, BW_C), BW_C ** -0.5),
        "rg_bx": nrm(25, (N_ODD, W_C), 0.01),
        "rg_lambda": jnp.log(u) - jnp.log1p(-u),
        "fox_bf": 3.0 + nrm(26, (N_ODD, H_D), 0.1),
        "w_out_o": nrm(27, (N_ODD, W_C + W_D, D_MODEL), (W_C + W_D) ** -0.5),
        "moe_router": nrm(28, (N_ODD, D_MODEL, N_EXPERTS), D_MODEL ** -0.5),
        "moe_w1": nrm(29, (N_ODD, N_EXPERTS, D_MODEL, D_FF_E), D_MODEL ** -0.5),
        "moe_w3": nrm(30, (N_ODD, N_EXPERTS, D_MODEL, D_FF_E), D_MODEL ** -0.5),
        "moe_w2": nrm(31, (N_ODD, N_EXPERTS, D_FF_E, D_MODEL), D_FF_E ** -0.5),
        "norm_mix": 1.0 + nrm(32, (DEPTH, D_MODEL), 0.05),
        "norm_ffn": 1.0 + nrm(33, (DEPTH, D_MODEL), 0.05),
        "norm_final": 1.0 + nrm(34, (D_MODEL,), 0.05),
    }


def reference(x_prompt, x_sample, cache_k_e, cache_v_e, state_s_e, state_conv_o, state_h_o, cache_k_o, cache_v_o,
              cache_logf_o, page_table, w_in_e, lb_logits, gnorm_a, w_out_e, ffn_w1, ffn_w3, ffn_w2, w_in_o, conv_w,
              conv_b, rg_wa, rg_ba, rg_wx, rg_bx, rg_lambda, fox_bf, w_out_o, moe_router, moe_w1, moe_w3, moe_w2,
              norm_mix, norm_ffn, norm_final):
    f32 = jnp.float32
    B_p, T_p, _ = x_prompt.shape
    pos_p = jnp.arange(T_p, dtype=jnp.int32)
    pos_s = PAST_LEN + jnp.arange(x_sample.shape[1], dtype=jnp.int32)
    lb_all = jnp.cumsum(jax.nn.softmax(lb_logits.astype(f32), axis=0), axis=0)
    s_p, s_s, ke_p, ve_p, ke_s, ve_s = [], [], [], [], [], []
    cv_p, cv_s, h_p, h_s, ko_p, vo_p, lo_p, ko_s, vo_s, lo_s = [], [], [], [], [], [], [], [], [], []
    yp, ys = x_prompt, x_sample
    for layer in range(DEPTH):
        j = layer // 2
        hp = rmsnorm(yp, norm_mix[layer])
        hs = rmsnorm(ys, norm_mix[layer])
        if layer % 2 == 0:
            ew = (lb_all[j], w_in_e[j], gnorm_a[j], w_out_e[j])
            s0p = jnp.zeros((B_p, H_A, DK_A, DV_A), f32)
            mp, sp, kp_, vp_ = even_mixer(hp, pos_p, s0p, *ew, None)
            ms, ss, ks_, vs_ = even_mixer(hs, pos_s, state_s_e[j], *ew, (cache_k_e[j], cache_v_e[j], page_table))
            s_p.append(sp.astype(state_s_e.dtype))
            s_s.append(ss.astype(state_s_e.dtype))
            ke_p.append(kp_.astype(cache_k_e.dtype))
            ve_p.append(vp_.astype(cache_v_e.dtype))
            ke_s.append(ks_.astype(cache_k_e.dtype))
            ve_s.append(vs_.astype(cache_v_e.dtype))
        else:
            ow = (w_in_o[j], conv_w[j], conv_b[j], rg_wa[j], rg_ba[j], rg_wx[j], rg_bx[j], rg_lambda[j], fox_bf[j], w_out_o[j])
            buf0 = jnp.zeros((B_p, CONV_W - 1, W_C), x_prompt.dtype)
            h0 = jnp.zeros((B_p, W_C), f32)
            mp, bp, hlp, kp_, vp_, lp_ = odd_mixer(hp, pos_p, buf0, h0, *ow, None)
            ms, bs, hls, ks_, vs_, ls_ = odd_mixer(hs, pos_s, state_conv_o[j], state_h_o[j], *ow,
                                                  (cache_k_o[j], cache_v_o[j], cache_logf_o[j], page_table))
            cv_p.append(bp.astype(state_conv_o.dtype))
            cv_s.append(bs.astype(state_conv_o.dtype))
            h_p.append(hlp.astype(state_h_o.dtype))
            h_s.append(hls.astype(state_h_o.dtype))
            ko_p.append(kp_.astype(cache_k_o.dtype))
            vo_p.append(vp_.astype(cache_v_o.dtype))
            lo_p.append(lp_.astype(cache_logf_o.dtype))
            ko_s.append(ks_.astype(cache_k_o.dtype))
            vo_s.append(vs_.astype(cache_v_o.dtype))
            lo_s.append(ls_.astype(cache_logf_o.dtype))
        yp = yp + mp
        ys = ys + ms
        hp = rmsnorm(yp, norm_ffn[layer])
        hs = rmsnorm(ys, norm_ffn[layer])
        if layer % 2 == 0:
            yp = yp + swiglu(hp, ffn_w1[j], ffn_w3[j], ffn_w2[j])
            ys = ys + swiglu(hs, ffn_w1[j], ffn_w3[j], ffn_w2[j])
        else:
            yp = yp + moe_swiglu(hp, moe_router[j], moe_w1[j], moe_w3[j], moe_w2[j])
            ys = ys + moe_swiglu(hs, moe_router[j], moe_w1[j], moe_w3[j], moe_w2[j])
    y_prompt = rmsnorm(yp, norm_final)
    y_sample = rmsnorm(ys, norm_final)
    return (y_prompt, y_sample,
            jnp.stack(s_p), jnp.stack(s_s),
            jnp.stack(ke_p), jnp.stack(ve_p), jnp.stack(ke_s), jnp.stack(ve_s),
            jnp.stack(cv_p), jnp.stack(cv_s), jnp.stack(h_p), jnp.stack(h_s),
            jnp.stack(ko_p), jnp.stack(vo_p), jnp.stack(lo_p),
            jnp.stack(ko_s), jnp.stack(vo_s), jnp.stack(lo_s))
```

```python
import functools
import math

import numpy as np
import jax
import jax.numpy as jnp
from jax import lax
from jax.experimental import pallas as pl
from jax.experimental.pallas import tpu as pltpu

F32 = jnp.float32
BF16 = jnp.bfloat16

D_MODEL = 1024
PAGE_SIZE = 128
H_A, DK_A, DV_A = 4, 128, 128
F_A, W_A = H_A * DK_A, H_A * DV_A
H_B, HD_B = 8, 64
W_B = H_B * HD_B
MOBA_BLOCK, MOBA_TOPK = 256, 3
W_C, NB_C, CONV_W, RG_C = 512, 8, 4, 8.0
BW_C = W_C // NB_C
H_D, HD_D = 8, 64
W_D = H_D * HD_D
N_EXPERTS, TOP_K = 8, 2
ROPE_THETA = 10000.0
EPS = 1e-6
NEG_INF = -1e30

LANES = 128
SUBLANES = 8
VMEM_LIMIT = 56 * 1024 * 1024


def _cparams(*sem):
    return pltpu.CompilerParams(dimension_semantics=sem, vmem_limit_bytes=VMEM_LIMIT)


def _split3(x):
    hi = x.astype(BF16)
    r1 = x - hi.astype(F32)
    mid = r1.astype(BF16)
    lo = (r1 - mid.astype(F32)).astype(BF16)
    return hi, mid, lo


def _dot(a, b):
    return jnp.dot(a, b, preferred_element_type=F32)


def _dot_nt(a, b):
    return lax.dot_general(a, b, (((1,), (1,)), ((), ())), preferred_element_type=F32)


def _dot_tn(a, b):
    return lax.dot_general(a, b, (((0,), (0,)), ((), ())), preferred_element_type=F32)


def _sigmoid(x):
    return 1.0 / (1.0 + jnp.exp(-x))


def _rmsnorm_kernel(x_ref, g_ref, o_ref):
    x = x_ref[...]
    y = x * lax.rsqrt(jnp.mean(x * x, axis=-1, keepdims=True) + EPS) * g_ref[...]
    o_ref[...] = y.astype(o_ref.dtype)


def _rmsnorm(x, g, out_dtype, tm=512):
    m, d = x.shape
    return pl.pallas_call(
        _rmsnorm_kernel,
        grid=(m // tm,),
        in_specs=[pl.BlockSpec((tm, d), lambda i: (i, 0)), pl.BlockSpec((1, d), lambda i: (0, 0))],
        out_specs=pl.BlockSpec((tm, d), lambda i: (i, 0)),
        out_shape=jax.ShapeDtypeStruct((m, d), out_dtype),
        compiler_params=_cparams("parallel"),
        name="rmsnorm",
    )(x, g.reshape(1, d))


def _matmul_kernel(*refs, n_x, has_res):
    xs = refs[:n_x]
    w_ref = refs[n_x]
    res_ref = refs[n_x + 1] if has_res else None
    o_ref = refs[-1]
    acc = None
    k0 = 0
    for x_ref in xs:
        kk = x_ref.shape[1]
        part = _dot(x_ref[...].astype(BF16), w_ref[k0:k0 + kk, :])
        acc = part if acc is None else acc + part
        k0 += kk
    if has_res:
        acc = acc + res_ref[...]
    o_ref[...] = acc


def _matmul(xs, w, res=None, tm=512, tn=512):
    m = xs[0].shape[0]
    kt, n = w.shape
    tn = min(tn, n)
    in_specs = [pl.BlockSpec((tm, x.shape[1]), lambda i, j: (i, 0)) for x in xs]
    in_specs.append(pl.BlockSpec((kt, tn), lambda i, j: (0, j)))
    args = list(xs) + [w]
    if res is not None:
        in_specs.append(pl.BlockSpec((tm, tn), lambda i, j: (i, j)))
        args.append(res)
    return pl.pallas_call(
        functools.partial(_matmul_kernel, n_x=len(xs), has_res=res is not None),
        grid=(m // tm, n // tn),
        in_specs=in_specs,
        out_specs=pl.BlockSpec((tm, tn), lambda i, j: (i, j)),
        out_shape=jax.ShapeDtypeStruct((m, n), F32),
        compiler_params=_cparams("parallel", "arbitrary"),
        name="matmul",
    )(*args)


def _ffn_kernel(te_ref, x_ref, w1_ref, w3_ref, w2_ref, *rest, has_res):
    del te_ref
    if has_res:
        res_ref, o_ref, acc_ref = rest
    else:
        o_ref, acc_ref = rest
    f = pl.program_id(1)

    @pl.when(f == 0)
    def _():
        acc_ref[...] = jnp.zeros_like(acc_ref)

    x = x_ref[...]
    a = _dot(x, w1_ref[...].astype(BF16))
    b = _dot(x, w3_ref[...].astype(BF16))
    g = (a * _sigmoid(a) * b).astype(BF16)
    acc_ref[...] += _dot(g, w2_ref[...].astype(BF16))

    @pl.when(f == pl.num_programs(1) - 1)
    def _():
        out = acc_ref[...]
        if has_res:
            out = out + res_ref[...]
        o_ref[...] = out


def _ffn(x, w1, w3, w2, res=None, tile_expert=None, tm=512, tf=256):
    m, d = x.shape
    f = w1.shape[-1]
    if tile_expert is None:
        tile_expert = jnp.zeros((m // tm,), jnp.int32)
    in_specs = [
        pl.BlockSpec((tm, d), lambda i, j, te: (i, 0)),
        pl.BlockSpec((None, d, tf), lambda i, j, te: (te[i], 0, j)),
        pl.BlockSpec((None, d, tf), lambda i, j, te: (te[i], 0, j)),
        pl.BlockSpec((None, tf, d), lambda i, j, te: (te[i], j, 0)),
    ]
    args = [x, w1, w3, w2]
    if res is not None:
        in_specs.append(pl.BlockSpec((tm, d), lambda i, j, te: (i, 0)))
        args.append(res)
    return pl.pallas_call(
        functools.partial(_ffn_kernel, has_res=res is not None),
        grid_spec=pltpu.PrefetchScalarGridSpec(
            num_scalar_prefetch=1,
            grid=(m // tm, f // tf),
            in_specs=in_specs,
            out_specs=pl.BlockSpec((tm, d), lambda i, j, te: (i, 0)),
            scratch_shapes=[pltpu.VMEM((tm, d), F32)],
        ),
        out_shape=jax.ShapeDtypeStruct((m, d), F32),
        compiler_params=_cparams("parallel", "arbitrary"),
        name="ffn",
    )(tile_expert, *args)


def _hgrn2_head(q, k, v, logf, st, c_sub):
    c = q.shape[0]
    ns = c // c_sub
    row = lax.broadcasted_iota(jnp.int32, (c, c), 0)
    col = lax.broadcasted_iota(jnp.int32, (c, c), 1)
    tri = (row >= col).astype(BF16)
    hi, mid, lo = _split3(logf)
    g = _dot(tri, hi) + _dot(tri, mid) + _dot(tri, lo)
    vb = v.astype(BF16)
    o = _dot_nt((q * jnp.exp(g)).astype(BF16), st.astype(BF16))
    srow = lax.broadcasted_iota(jnp.int32, (c_sub, c_sub), 0)
    scol = lax.broadcasted_iota(jnp.int32, (c_sub, c_sub), 1)
    o_rows = []
    for i in range(ns):
        r0 = i * c_sub
        qi, ki, gi = q[r0:r0 + c_sub], k[r0:r0 + c_sub], g[r0:r0 + c_sub]
        blk = jnp.zeros((c_sub, c_sub), F32)
        for s in range(c_sub):
            w = jnp.exp(jnp.minimum(gi - gi[s:s + 1, :], 0.0))
            colv = jnp.sum(qi * w * ki[s:s + 1, :], axis=-1, keepdims=True)
            blk = jnp.where(scol == s, colv, blk)
        blk = jnp.where(srow >= scol, blk, 0.0)
        oi = _dot(blk.astype(BF16), vb[r0:r0 + c_sub])
        if i > 0:
            gb = g[r0 - 1:r0, :]
            qs = qi * jnp.exp(gi - gb)
            ks = k[0:r0] * jnp.exp(gb - g[0:r0])
            sc = _dot_nt(qs.astype(BF16), ks.astype(BF16))
            oi = oi + _dot(sc.astype(BF16), vb[0:r0])
        o_rows.append(oi)
    o = o + (o_rows[0] if ns == 1 else jnp.concatenate(o_rows, axis=0))
    gl = g[c - 1:c, :]
    ke = (k * jnp.exp(gl - g)).astype(BF16)
    st_new = jnp.exp(gl) * st + _dot_tn(vb, ke)
    return o, st_new


def _hgrn2_kernel(aq_ref, af_ref, ai_ref, ag_ref, lb_ref, gn_ref, s0_ref, o_ref, s_ref, st_scr, *, c_sub, ct):
    t = pl.program_id(1)

    @pl.when(t == 0)
    def _():
        for h in range(H_A):
            st_scr[h] = s0_ref[h].T

    def padded(ref):
        x = ref[...]
        if ct < SUBLANES:
            x = jnp.concatenate([x, jnp.zeros((SUBLANES - ct, x.shape[1]), F32)], axis=0)
        return x

    aq, zf, vi = padded(aq_ref), padded(af_ref), padded(ai_ref)
    c = aq.shape[0]
    lb = lb_ref[...]
    real = lax.broadcasted_iota(jnp.int32, (c, F_A), 0) < ct
    sig = _sigmoid(zf)
    q = aq * _sigmoid(aq)
    logf = jnp.where(real, jnp.log(lb + (1.0 - lb) * sig), 0.0)
    k = jnp.where(real, (1.0 - lb) * _sigmoid(-zf), 0.0)
    outs = []
    for h in range(H_A):
        sl = slice(h * DK_A, (h + 1) * DK_A)
        o_h, st_new = _hgrn2_head(q[:, sl], k[:, sl], vi[:, sl], logf[:, sl], st_scr[h], c_sub)
        st_scr[h] = st_new
        outs.append(o_h)
    o = jnp.concatenate(outs, axis=1)[:ct]
    o = o * lax.rsqrt(jnp.mean(o * o, axis=-1, keepdims=True) + EPS) * gn_ref[...]
    o_ref[...] = o * _sigmoid(ag_ref[...])

    @pl.when(t == pl.num_programs(1) - 1)
    def _():
        for h in range(H_A):
            s_ref[h] = st_scr[h].T


def _hgrn2(proj, row0, n_seq, n_chunks, ct, s0, lb, gnorm):
    m, cols = proj.shape
    x3 = proj.reshape(m // ct, ct, cols)
    blk0 = row0 // ct
    c_sub = min(16, max(ct, SUBLANES))

    def tok_spec(cb):
        return pl.BlockSpec((None, ct, F_A), lambda n, t: (blk0 + n * n_chunks + t, 0, cb))

    o, s = pl.pallas_call(
        functools.partial(_hgrn2_kernel, c_sub=c_sub, ct=ct),
        grid=(n_seq, n_chunks),
        in_specs=[tok_spec(0), tok_spec(1), tok_spec(2), tok_spec(3),
                  pl.BlockSpec((1, F_A), lambda n, t: (0, 0)),
                  pl.BlockSpec((1, W_A), lambda n, t: (0, 0)),
                  pl.BlockSpec((None, H_A, DK_A, DV_A), lambda n, t: (n, 0, 0, 0))],
        out_specs=[pl.BlockSpec((None, ct, W_A), lambda n, t: (n * n_chunks + t, 0, 0)),
                   pl.BlockSpec((None, H_A, DK_A, DV_A), lambda n, t: (n, 0, 0, 0))],
        out_shape=[jax.ShapeDtypeStruct((n_seq * n_chunks, ct, W_A), F32),
                   jax.ShapeDtypeStruct((n_seq, H_A, DK_A, DV_A), F32)],
        scratch_shapes=[pltpu.VMEM((H_A, DV_A, DK_A), F32)],
        compiler_params=_cparams("parallel", "arbitrary"),
        name="hgrn2",
    )(x3, x3, x3, x3, lb.reshape(1, F_A), gnorm.reshape(1, W_A), s0)
    return o.reshape(n_seq * n_chunks * ct, W_A), s


def _rope_tables(pos):
    half = HD_B // 2
    inv = ROPE_THETA ** (-jnp.arange(half, dtype=F32) / half)
    ang = pos.astype(F32)[:, None] * inv[None, :]
    cos, sin = jnp.cos(ang), jnp.sin(ang)
    cos2 = jnp.concatenate([cos, cos, cos, cos], axis=1)
    sin2 = jnp.concatenate([-sin, sin, -sin, sin], axis=1)
    return cos2, sin2


def _rope(x, cos2, sin2, lane):
    swapped = jnp.where((lane & (HD_B - 1)) >= HD_B // 2, pltpu.roll(x, HD_B // 2, 1), pltpu.roll(x, LANES - HD_B // 2, 1))
    return x * cos2 + swapped * sin2


def _dot_nt_f32(a, b):
    a0, a1, a2 = _split3(a)
    b0, b1, b2 = _split3(b)
    return (_dot_nt(a0, b0) + (_dot_nt(a0, b1) + _dot_nt(a1, b0))
            + (_dot_nt(a0, b2) + _dot_nt(a1, b1) + _dot_nt(a2, b0)))


def _top_select(g, lane_or_row, axis, n_pick):
    sel = jnp.zeros(g.shape, F32)
    big = jnp.int32(1 << 20)
    for _ in range(n_pick):
        m = jnp.max(g, axis=axis, keepdims=True)
        first = jnp.min(jnp.where(g == m, lane_or_row, big), axis=axis, keepdims=True)
        pick = jnp.where((lane_or_row == first) & (m > 0.5 * NEG_INF), 1.0, 0.0)
        sel = sel + pick
        g = jnp.where(pick > 0.5, NEG_INF, g)
    return sel


def _moba_prep_kernel(q_ref, k_ref, v_ref, cos_ref, sin_ref, ke_ref, qa_ref, ka_ref, va_ref, mrow_scr):
    t = pl.program_id(1)
    rows = q_ref.shape[0]

    @pl.when(t == 0)
    def _():
        mrow_scr[...] = jnp.zeros_like(mrow_scr)

    lane = lax.broadcasted_iota(jnp.int32, (rows, LANES), 1)
    lane1 = lax.broadcasted_iota(jnp.int32, (1, LANES), 1)
    low = lane < HD_B
    blk = lane & (HD_B - 1)
    cos2, sin2 = cos_ref[...], sin_ref[...]
    scale = HD_B ** -0.5
    for c in range(W_B // LANES):
        sl = slice(c * LANES, (c + 1) * LANES)
        qr = _rope(q_ref[:, sl], cos2, sin2, lane)
        kr = _rope(k_ref[:, sl], cos2, sin2, lane)
        vc = v_ref[:, sl]
        ke_ref[:, sl] = kr
        gate = _dot_nt_f32(qr, mrow_scr[c])
        g = jnp.where(blk < t, gate, NEG_INF)
        sel = (_top_select(jnp.where(low, NEG_INF, g), lane, 1, MOBA_TOPK)
               + _top_select(jnp.where(low, g, NEG_INF), lane, 1, MOBA_TOPK))
        msel = jnp.where((sel > 0.5) | (blk == t), 0.0, NEG_INF)
        own = jnp.where(blk == t, 1.0, 0.0)
        qs = qr * scale
        qa_ref[2 * c] = jnp.where(low, qs, msel).astype(BF16)
        qa_ref[2 * c + 1] = jnp.where(low, msel, qs).astype(BF16)
        ka_ref[2 * c] = jnp.where(low, kr, own).astype(BF16)
        ka_ref[2 * c + 1] = jnp.where(low, own, kr).astype(BF16)
        va_ref[2 * c] = jnp.where(low, vc, 0.0).astype(BF16)
        va_ref[2 * c + 1] = jnp.where(low, 0.0, vc).astype(BF16)
        mean = jnp.mean(kr, axis=0, keepdims=True)
        mrow_scr[c, pl.ds(HD_B + t, 1), :] = jnp.where(lane1 < HD_B, mean, 0.0)
        mrow_scr[c, pl.ds(t, 1), :] = jnp.where(lane1 < HD_B, 0.0, mean)


def _moba_prep(proj, batch, seq, cos2, sin2):
    nb = seq // MOBA_BLOCK
    assert seq % MOBA_BLOCK == 0 and nb <= HD_B
    rows = MOBA_BLOCK

    def tok(cb):
        return pl.BlockSpec((rows, W_B), lambda b, t: (b * nb + t, cb))

    pair = pl.BlockSpec((None, H_B, rows, LANES), lambda b, t: (b, 0, t, 0))
    pair_shape = jax.ShapeDtypeStruct((batch, H_B, seq, LANES), BF16)
    cb0 = (2 * F_A + 2 * W_A) // W_B
    return pl.pallas_call(
        _moba_prep_kernel,
        grid=(batch, nb),
        in_specs=[tok(cb0), tok(cb0 + 1), tok(cb0 + 2),
                  pl.BlockSpec((rows, LANES), lambda b, t: (t, 0)),
                  pl.BlockSpec((rows, LANES), lambda b, t: (t, 0))],
        out_specs=[pl.BlockSpec((rows, W_B), lambda b, t: (b * nb + t, 0)), pair, pair, pair],
        out_shape=[jax.ShapeDtypeStruct((batch * seq, W_B), F32), pair_shape, pair_shape, pair_shape],
        scratch_shapes=[pltpu.VMEM((W_B // LANES, LANES, LANES), F32)],
        compiler_params=_cparams("parallel", "arbitrary"),
        name="moba_prep",
    )(proj, proj, proj, cos2, sin2)


def _flash_kernel(qa_ref, ka_ref, va_ref, o_ref, *, tq):
    i = pl.program_id(2)
    row = lax.broadcasted_iota(jnp.int32, (tq, tq), 0)
    col = lax.broadcasted_iota(jnp.int32, (tq, tq), 1)
    out = jnp.zeros((tq, LANES), F32)
    for hh in range(2):
        q = qa_ref[hh]

        def step(j, carry, diag):
            m, l, acc = carry
            start = pl.multiple_of(j * tq, tq)
            k = ka_ref[hh, pl.ds(start, tq), :]
            v = va_ref[hh, pl.ds(start, tq), :]
            s = _dot_nt(q, k)
            if diag:
                s = jnp.where(row >= col, s, NEG_INF)
            m_new = jnp.maximum(m, jnp.max(s, axis=-1, keepdims=True))
            p = jnp.exp(s - m_new)
            alpha = jnp.exp(m - m_new)
            l = alpha * l + jnp.sum(p, axis=-1, keepdims=True)
            acc = alpha * acc + _dot(p.astype(BF16), v)
            return m_new, l, acc

        init = (jnp.full((tq, 1), -jnp.inf, F32), jnp.zeros((tq, 1), F32), jnp.zeros((tq, LANES), F32))
        carry = lax.fori_loop(0, i, functools.partial(step, diag=False), init)
        m, l, acc = step(i, carry, True)
        out = out + acc / l
    o_ref[...] = out


def _flash(qa, ka, va, tq=256):
    batch, heads, seq, _ = qa.shape
    nq = seq // tq
    return pl.pallas_call(
        functools.partial(_flash_kernel, tq=tq),
        grid=(batch, heads // 2, nq),
        in_specs=[pl.BlockSpec((None, 2, tq, LANES), lambda b, c, i: (b, c, i, 0)),
                  pl.BlockSpec((None, 2, seq, LANES), lambda b, c, i: (b, c, 0, 0)),
                  pl.BlockSpec((None, 2, seq, LANES), lambda b, c, i: (b, c, 0, 0))],
        out_specs=pl.BlockSpec((tq, LANES), lambda b, c, i: (b * nq + i, c)),
        out_shape=jax.ShapeDtypeStruct((batch * seq, heads * LANES // 2), F32),
        compiler_params=_cparams("parallel", "parallel", "arbitrary"),
        name="flash",
    )(qa, ka, va)


def _query_rows(q4, lane, rowh):
    t, w = q4.shape
    rep = jnp.broadcast_to(q4[:, None, :], (t, H_B, w)).reshape(t * H_B, w)
    return jnp.where((lane // HD_B) == rowh, rep, 0.0)


def _head_rows_to_tokens(o, lane, rowh, t):
    o = jnp.where((lane // HD_B) == rowh, o, 0.0)
    return jnp.sum(o.reshape(t, H_B, o.shape[1]), axis=1)


def _moba_sample_kernel(pt_ref, q_ref, k_ref, v_ref, cos_ref, sin_ref, *rest, n_pages):
    del pt_ref
    kp, vp = rest[:n_pages], rest[n_pages:2 * n_pages]
    ke_ref, o_ref, mean_scr = rest[2 * n_pages:]
    t = q_ref.shape[0]
    nc = t * H_B
    nbp = n_pages * PAGE_SIZE // MOBA_BLOCK
    ppb = MOBA_BLOCK // PAGE_SIZE
    lane1 = lax.broadcasted_iota(jnp.int32, (SUBLANES, LANES), 1)
    zpad = jnp.zeros((SUBLANES - t, W_B), F32)
    q8 = jnp.concatenate([q_ref[...], zpad], axis=0)
    k8 = jnp.concatenate([k_ref[...], zpad], axis=0)
    v8 = jnp.concatenate([v_ref[...], zpad], axis=0)
    cos2 = jnp.concatenate([cos_ref[...], jnp.zeros((SUBLANES - t, LANES), F32)], axis=0)
    sin2 = jnp.concatenate([sin_ref[...], jnp.zeros((SUBLANES - t, LANES), F32)], axis=0)
    qr = jnp.concatenate([_rope(q8[:, c * LANES:(c + 1) * LANES], cos2, sin2, lane1) for c in range(W_B // LANES)], axis=1)
    kr = jnp.concatenate([_rope(k8[:, c * LANES:(c + 1) * LANES], cos2, sin2, lane1) for c in range(W_B // LANES)], axis=1)
    ke_ref[...] = kr[:t]

    lane = lax.broadcasted_iota(jnp.int32, (nc, W_B), 1)
    rowh = lax.broadcasted_iota(jnp.int32, (nc, W_B), 0) % H_B
    qrows = _query_rows(qr[:t], lane, rowh)

    mean_scr[...] = jnp.zeros_like(mean_scr)
    for j in range(nbp):
        tot = kp[ppb * j][...].sum(axis=0, keepdims=True)
        for u in range(1, ppb):
            tot = tot + kp[ppb * j + u][...].sum(axis=0, keepdims=True)
        mean_scr[j:j + 1, :] = tot * (1.0 / MOBA_BLOCK)
    nrow = mean_scr.shape[0]
    gate = _dot_nt_f32(mean_scr[...], qrows)
    brow = lax.broadcasted_iota(jnp.int32, (nrow, nc), 0)
    sel = _top_select(jnp.where(brow < nbp, gate, NEG_INF), brow, 0, MOBA_TOPK)

    qb = (qrows * (HD_B ** -0.5)).astype(BF16)
    s_pages = []
    for p in range(n_pages):
        s = _dot_nt(kp[p][...].astype(BF16), qb)
        j = p // ppb
        s_pages.append(jnp.where(sel[j:j + 1, :] > 0.5, s, NEG_INF))
    s_new = _dot_nt(kr.astype(BF16), qb)
    krow = lax.broadcasted_iota(jnp.int32, (SUBLANES, nc), 0)
    qtok = lax.broadcasted_iota(jnp.int32, (SUBLANES, nc), 1) // H_B
    s_new = jnp.where((krow <= qtok) & (krow < t), s_new, NEG_INF)
    m = jnp.max(s_new, axis=0, keepdims=True)
    for s in s_pages:
        m = jnp.maximum(m, jnp.max(s, axis=0, keepdims=True))
    p_new = jnp.exp(s_new - m)
    l = jnp.sum(p_new, axis=0, keepdims=True)
    p_pages = []
    for s in s_pages:
        pp = jnp.exp(s - m)
        l = l + jnp.sum(pp, axis=0, keepdims=True)
        p_pages.append(pp)
    inv = 1.0 / l
    o = _dot_tn((p_new * inv).astype(BF16), v8.astype(BF16))
    for p in range(n_pages):
        o = o + _dot_tn((p_pages[p] * inv).astype(BF16), vp[p][...].astype(BF16))
    o_ref[...] = _head_rows_to_tokens(o, lane, rowh, t)


def _paged_specs(n_pages, width):
    return [pl.BlockSpec((None, PAGE_SIZE, width), functools.partial(lambda n, pt, p: (pt[n, p], 0, 0), p=p))
            for p in range(n_pages)]


def _moba_sample(proj, row0, n_seq, t, pool_k, pool_v, page_table, cos2, sin2):
    m, cols = proj.shape
    n_pages = page_table.shape[1]
    assert (n_pages * PAGE_SIZE) % MOBA_BLOCK == 0
    x3 = proj.reshape(m // t, t, cols)
    blk0 = row0 // t
    cb0 = (2 * F_A + 2 * W_A) // W_B
    n_pool = pool_k.shape[0]
    pk = pool_k.reshape(n_pool, PAGE_SIZE, W_B)
    pv = pool_v.reshape(n_pool, PAGE_SIZE, W_B)
    nbp = n_pages * PAGE_SIZE // MOBA_BLOCK

    def tok(cb):
        return pl.BlockSpec((None, t, W_B), lambda n, pt: (blk0 + n, 0, cb))

    full = pl.BlockSpec((t, LANES), lambda n, pt: (0, 0))
    out = pl.BlockSpec((None, t, W_B), lambda n, pt: (n, 0, 0))
    ke, o = pl.pallas_call(
        functools.partial(_moba_sample_kernel, n_pages=n_pages),
        grid_spec=pltpu.PrefetchScalarGridSpec(
            num_scalar_prefetch=1,
            grid=(n_seq,),
            in_specs=[tok(cb0), tok(cb0 + 1), tok(cb0 + 2), full, full]
            + _paged_specs(n_pages, W_B) + _paged_specs(n_pages, W_B),
            out_specs=[out, out],
            scratch_shapes=[pltpu.VMEM((-(-nbp // SUBLANES) * SUBLANES, W_B), F32)],
        ),
        out_shape=[jax.ShapeDtypeStruct((n_seq, t, W_B), F32)] * 2,
        compiler_params=_cparams("arbitrary"),
        name="moba_sample",
    )(page_table, x3, x3, x3, cos2, sin2, *([pk] * n_pages), *([pv] * n_pages))
    return ke.reshape(n_seq * t, W_B), o.reshape(n_seq * t, W_B)


def _shift_rows(x, d, fill):
    return jnp.concatenate([jnp.full((d, x.shape[1]), fill, x.dtype), x[:-d]], axis=0)


def _log_sigmoid(x):
    return jnp.minimum(x, 0.0) - jnp.log(1.0 + jnp.exp(-jnp.abs(x)))


def _gelu_tanh(x):
    return 0.5 * x * (1.0 + jnp.tanh(math.sqrt(2.0 / math.pi) * (x + 0.044715 * (x * x * x))))


def _rglru_gates(xc, wa_ref, ba_ref, wx_ref, bx_ref, sp_ref, first_row_pos0):
    xb = xc.astype(BF16)
    r = _sigmoid(_dot(xb, wa_ref[...]) + ba_ref[...])
    i = _sigmoid(_dot(xb, wx_ref[...]) + bx_ref[...])
    log_a = -RG_C * r * sp_ref[...]
    a = jnp.exp(log_a)
    mult = jnp.sqrt(1.0 - jnp.exp(2.0 * log_a))
    if first_row_pos0 is not None:
        mult = jnp.where(first_row_pos0, 1.0, mult)
    return a, xc * i * mult


def _rglru_kernel(cx_ref, cg_ref, cw_ref, cb_ref, wa_ref, ba_ref, wx_ref, bx_ref, sp_ref,
                  o_ref, buf_ref, h_ref, tail_scr, h_scr):
    t = pl.program_id(1)
    tt = cx_ref.shape[0]

    @pl.when(t == 0)
    def _():
        tail_scr[...] = jnp.zeros_like(tail_scr)
        h_scr[...] = jnp.zeros_like(h_scr)

    cx = cx_ref[...]
    ext = jnp.concatenate([tail_scr[...], cx], axis=0)
    xc = cb_ref[...] + cx * cw_ref[CONV_W - 1:CONV_W, :]
    for d in range(1, CONV_W):
        xc = xc + ext[SUBLANES - d:SUBLANES - d + tt] * cw_ref[CONV_W - 1 - d:CONV_W - d, :]
    row = lax.broadcasted_iota(jnp.int32, (tt, W_C), 0)
    a, b = _rglru_gates(xc, wa_ref, ba_ref, wx_ref, bx_ref, sp_ref, (row == 0) & (t == 0))
    d = 1
    while d < tt:
        b = b + a * _shift_rows(b, d, 0.0)
        a = a * _shift_rows(a, d, 1.0)
        d *= 2
    h = a * h_scr[0:1, :] + b
    o_ref[...] = h * _gelu_tanh(cg_ref[...])
    h_scr[0:1, :] = h[tt - 1:tt, :]
    tail_scr[...] = cx[tt - SUBLANES:tt, :]

    @pl.when(t == pl.num_programs(1) - 1)
    def _():
        buf_ref[...] = cx[tt - (CONV_W - 1):tt, :]
        h_ref[...] = h[tt - 1:tt, :]


def _rglru_prompt(proj, batch, seq, cw, cb, wa, ba, wx, bx, sp, tt=256):
    nt = seq // tt
    vec = pl.BlockSpec((1, W_C), lambda b, t: (0, 0))
    mat = pl.BlockSpec((W_C, W_C), lambda b, t: (0, 0))
    o, buf, h = pl.pallas_call(
        _rglru_kernel,
        grid=(batch, nt),
        in_specs=[pl.BlockSpec((tt, W_C), lambda b, t: (b * nt + t, 0)),
                  pl.BlockSpec((tt, W_C), lambda b, t: (b * nt + t, 1)),
                  pl.BlockSpec((CONV_W, W_C), lambda b, t: (0, 0)), vec, mat, vec, mat, vec, vec],
        out_specs=[pl.BlockSpec((tt, W_C), lambda b, t: (b * nt + t, 0)),
                   pl.BlockSpec((None, CONV_W - 1, W_C), lambda b, t: (b, 0, 0)),
                   pl.BlockSpec((None, 1, W_C), lambda b, t: (b, 0, 0))],
        out_shape=[jax.ShapeDtypeStruct((batch * seq, W_C), F32),
                   jax.ShapeDtypeStruct((batch, CONV_W - 1, W_C), F32),
                   jax.ShapeDtypeStruct((batch, 1, W_C), F32)],
        scratch_shapes=[pltpu.VMEM((SUBLANES, W_C), F32), pltpu.VMEM((SUBLANES, W_C), F32)],
        compiler_params=_cparams("parallel", "arbitrary"),
        name="rglru",
    )(proj, proj, cw, cb, wa, ba, wx, bx, sp)
    return o, buf, h.reshape(batch, W_C)


def _rglru_sample_kernel(cx_ref, cg_ref, buf_ref, h0_ref, cw_ref, cb_ref, wa_ref, ba_ref, wx_ref, bx_ref, sp_ref,
                         o_ref, h_ref, *, pos0_is_zero):
    t, n, _ = cx_ref.shape
    xp = [buf_ref[j] for j in range(CONV_W - 1)] + [cx_ref[j] for j in range(t)]
    xcs = []
    for s in range(t):
        xc = cb_ref[...] + xp[s] * cw_ref[0:1, :]
        for j in range(1, CONV_W):
            xc = xc + xp[s + j] * cw_ref[j:j + 1, :]
        xcs.append(xc)
    xc = jnp.concatenate(xcs, axis=0)
    first = (lax.broadcasted_iota(jnp.int32, xc.shape, 0) < n) if pos0_is_zero else None
    a, b = _rglru_gates(xc, wa_ref, ba_ref, wx_ref, bx_ref, sp_ref, first)
    h = h0_ref[...]
    for s in range(t):
        h = a[s * n:(s + 1) * n] * h + b[s * n:(s + 1) * n]
        o_ref[s] = h * _gelu_tanh(cg_ref[s])
    h_ref[...] = h


def _rglru_sample(cx, cg, buf, h0, pos0_is_zero, cw, cb, wa, ba, wx, bx, sp):
    t, n, _ = cx.shape
    return pl.pallas_call(
        functools.partial(_rglru_sample_kernel, pos0_is_zero=pos0_is_zero),
        out_shape=[jax.ShapeDtypeStruct((t, n, W_C), F32), jax.ShapeDtypeStruct((n, W_C), F32)],
        compiler_params=pltpu.CompilerParams(vmem_limit_bytes=VMEM_LIMIT),
        name="rglru_sample",
    )(cx, cg, buf, h0, cw, cb, wa, ba, wx, bx, sp)


COLS_E_V = 2 * F_A + 2 * W_A + 2 * W_B
FOX_ONE_LANE = 3 * H_D


def _fox_selectors():
    selq = np.zeros((H_D // 2, LANES, LANES), np.float32)
    selk = np.zeros((H_D // 2, LANES, LANES), np.float32)
    for cc in range(H_D // 2):
        for h, base in ((2 * cc, HD_D), (2 * cc + 1, 0)):
            for piece in range(3):
                selq[cc, piece * H_D + h, base + piece] = 1.0
                selq[cc, FOX_ONE_LANE, base + 3 + piece] = 1.0
                selk[cc, FOX_ONE_LANE, base + piece] = 1.0
                selk[cc, piece * H_D + h, base + 3 + piece] = -1.0
    return jnp.asarray(selq, BF16), jnp.asarray(selk, BF16)


def _split_features(c, lane):
    hi, mid, lo = _split3(jnp.where(lane < H_D, c, 0.0))
    feat = (hi.astype(F32) + pltpu.roll(mid.astype(F32), H_D, 1) + pltpu.roll(lo.astype(F32), 2 * H_D, 1)
            + jnp.where(lane == FOX_ONE_LANE, 1.0, 0.0))
    return feat.astype(BF16)


def _fox_prep_kernel(q_ref, k_ref, v_ref, df_ref, bf_ref, selq_ref, selk_ref, lf_ref, qa_ref, ka_ref, va_ref, c_scr):
    t = pl.program_id(1)
    rows = q_ref.shape[0]

    @pl.when(t == 0)
    def _():
        c_scr[...] = jnp.zeros_like(c_scr)

    lane = lax.broadcasted_iota(jnp.int32, (rows, LANES), 1)
    low = lane < HD_D
    logf = jnp.where(lane < H_D, _log_sigmoid(df_ref[...] + bf_ref[...]), 0.0)
    lf_ref[...] = logf[:, :H_D]
    tri = (lax.broadcasted_iota(jnp.int32, (rows, rows), 0) >= lax.broadcasted_iota(jnp.int32, (rows, rows), 1)).astype(BF16)
    hi, mid, lo = _split3(logf)
    c = c_scr[0:1, :] + (_dot(tri, hi) + _dot(tri, mid) + _dot(tri, lo))
    c_scr[0:1, :] = c[rows - 1:rows, :]
    feat = _split_features(c, lane)
    scale = HD_D ** -0.5
    for cc in range(W_D // LANES):
        sl = slice(cc * LANES, (cc + 1) * LANES)
        eq = _dot(feat, selq_ref[cc])
        ek = _dot(feat, selk_ref[cc])
        qs, kc, vc = q_ref[:, sl] * scale, k_ref[:, sl], v_ref[:, sl]
        qa_ref[2 * cc] = jnp.where(low, qs, eq).astype(BF16)
        qa_ref[2 * cc + 1] = jnp.where(low, eq, qs).astype(BF16)
        ka_ref[2 * cc] = jnp.where(low, kc, ek).astype(BF16)
        ka_ref[2 * cc + 1] = jnp.where(low, ek, kc).astype(BF16)
        va_ref[2 * cc] = jnp.where(low, vc, 0.0).astype(BF16)
        va_ref[2 * cc + 1] = jnp.where(low, 0.0, vc).astype(BF16)


FOX_DF_COL = (2 * W_C + 3 * W_D) // LANES


def _fox_prep(proj, batch, seq, bf_row, selq, selk, rows=256):
    nt = seq // rows

    def tok(cb):
        return pl.BlockSpec((rows, W_D), lambda b, t: (b * nt + t, cb))

    pair = pl.BlockSpec((None, H_D, rows, LANES), lambda b, t: (b, 0, t, 0))
    pair_shape = jax.ShapeDtypeStruct((batch, H_D, seq, LANES), BF16)
    sel = pl.BlockSpec((H_D // 2, LANES, LANES), lambda b, t: (0, 0, 0))
    cb0 = 2 * W_C // W_D
    return pl.pallas_call(
        _fox_prep_kernel,
        grid=(batch, nt),
        in_specs=[tok(cb0), tok(cb0 + 1), tok(cb0 + 2),
                  pl.BlockSpec((rows, LANES), lambda b, t: (b * nt + t, FOX_DF_COL)),
                  pl.BlockSpec((1, LANES), lambda b, t: (0, 0)), sel, sel],
        out_specs=[pl.BlockSpec((rows, H_D), lambda b, t: (b * nt + t, 0)), pair, pair, pair],
        out_shape=[jax.ShapeDtypeStruct((batch * seq, H_D), F32), pair_shape, pair_shape, pair_shape],
        scratch_shapes=[pltpu.VMEM((SUBLANES, LANES), F32)],
        compiler_params=_cparams("parallel", "arbitrary"),
        name="fox_prep",
    )(proj, proj, proj, proj, bf_row, selq, selk)


def _fox_sample_kernel(pt_ref, q_ref, k_ref, v_ref, df_ref, bf_ref, hsel_ref, *rest, n_pages):
    del pt_ref
    kp, vp, lp = rest[:n_pages], rest[n_pages:2 * n_pages], rest[2 * n_pages:3 * n_pages]
    lf_ref, o_ref = rest[3 * n_pages:]
    t = q_ref.shape[0]
    nc = t * H_D
    zpad = jnp.zeros((SUBLANES - t, W_D), F32)
    k8 = jnp.concatenate([k_ref[...], zpad], axis=0)
    v8 = jnp.concatenate([v_ref[...], zpad], axis=0)
    lane = lax.broadcasted_iota(jnp.int32, (nc, W_D), 1)
    rowh = lax.broadcasted_iota(jnp.int32, (nc, W_D), 0) % H_D
    qb = (_query_rows(q_ref[...], lane, rowh) * (HD_D ** -0.5)).astype(BF16)
    hsel = hsel_ref[...]

    lfn = _log_sigmoid(df_ref[...] + bf_ref[...])[:, :H_D]
    lf_ref[...] = lfn
    lfn8 = jnp.concatenate([lfn, jnp.zeros((SUBLANES - t, H_D), F32)], axis=0)
    r8 = lax.broadcasted_iota(jnp.int32, (SUBLANES, SUBLANES), 0)
    c8 = lax.broadcasted_iota(jnp.int32, (SUBLANES, SUBLANES), 1)
    tri8 = (r8 >= c8).astype(BF16)
    h0, h1, h2 = _split3(lfn8)
    cn = _dot(tri8, h0) + _dot(tri8, h1) + _dot(tri8, h2)
    h0, h1, h2 = _split3(cn)
    cn_cols = _dot_nt(h0, hsel) + _dot_nt(h1, hsel) + _dot_nt(h2, hsel)
    krow = lax.broadcasted_iota(jnp.int32, (SUBLANES, nc), 0)
    qtok = lax.broadcasted_iota(jnp.int32, (SUBLANES, nc), 1) // H_D
    qterm = jnp.sum(jnp.where(krow == qtok, cn_cols, 0.0), axis=0, keepdims=True)

    rr = lax.broadcasted_iota(jnp.int32, (PAGE_SIZE, PAGE_SIZE), 0)
    cc = lax.broadcasted_iota(jnp.int32, (PAGE_SIZE, PAGE_SIZE), 1)
    ustrict = (cc > rr).astype(BF16)
    after = jnp.zeros((1, H_D), F32)
    s_pages = [None] * n_pages
    for p in reversed(range(n_pages)):
        lf = lp[p][...]
        h0, h1, h2 = _split3(lf)
        suf = after + (_dot(ustrict, h0) + _dot(ustrict, h1) + _dot(ustrict, h2))
        after = after + jnp.sum(lf, axis=0, keepdims=True)
        h0, h1, h2 = _split3(suf)
        bias = _dot_nt(h0, hsel) + _dot_nt(h1, hsel) + _dot_nt(h2, hsel)
        s_pages[p] = _dot_nt(kp[p][...].astype(BF16), qb) + bias + qterm
    s_new = _dot_nt(k8.astype(BF16), qb) + (qterm - cn_cols)
    s_new = jnp.where((krow <= qtok) & (krow < t), s_new, NEG_INF)
    m = jnp.max(s_new, axis=0, keepdims=True)
    for s in s_pages:
        m = jnp.maximum(m, jnp.max(s, axis=0, keepdims=True))
    p_new = jnp.exp(s_new - m)
    l = jnp.sum(p_new, axis=0, keepdims=True)
    p_pages = []
    for s in s_pages:
        pp = jnp.exp(s - m)
        l = l + jnp.sum(pp, axis=0, keepdims=True)
        p_pages.append(pp)
    inv = 1.0 / l
    o = _dot_tn((p_new * inv).astype(BF16), v8.astype(BF16))
    for p in range(n_pages):
        o = o + _dot_tn((p_pages[p] * inv).astype(BF16), vp[p][...].astype(BF16))
    o_ref[...] = _head_rows_to_tokens(o, lane, rowh, t)


def _fox_sample(proj, row0, n_seq, t, pool_k, pool_v, pool_lf, page_table, bf_row):
    m, cols = proj.shape
    n_pages = page_table.shape[1]
    x3 = proj.reshape(m // t, t, cols)
    blk0 = row0 // t
    cb0 = 2 * W_C // W_D
    n_pool = pool_k.shape[0]
    pk = pool_k.reshape(n_pool, PAGE_SIZE, W_D)
    pv = pool_v.reshape(n_pool, PAGE_SIZE, W_D)
    hsel = jnp.asarray((np.arange(t * H_D)[:, None] % H_D) == np.arange(H_D)[None, :], BF16)

    def tok(cb):
        return pl.BlockSpec((None, t, W_D), lambda n, pt: (blk0 + n, 0, cb))

    lf, o = pl.pallas_call(
        functools.partial(_fox_sample_kernel, n_pages=n_pages),
        grid_spec=pltpu.PrefetchScalarGridSpec(
            num_scalar_prefetch=1,
            grid=(n_seq,),
            in_specs=[tok(cb0), tok(cb0 + 1), tok(cb0 + 2),
                      pl.BlockSpec((None, t, LANES), lambda n, pt: (blk0 + n, 0, FOX_DF_COL)),
                      pl.BlockSpec((1, LANES), lambda n, pt: (0, 0)),
                      pl.BlockSpec((t * H_D, H_D), lambda n, pt: (0, 0))]
            + _paged_specs(n_pages, W_D) + _paged_specs(n_pages, W_D) + _paged_specs(n_pages, H_D),
            out_specs=[pl.BlockSpec((None, t, H_D), lambda n, pt: (n, 0, 0)),
                       pl.BlockSpec((None, t, W_D), lambda n, pt: (n, 0, 0))],
        ),
        out_shape=[jax.ShapeDtypeStruct((n_seq, t, H_D), F32), jax.ShapeDtypeStruct((n_seq, t, W_D), F32)],
        compiler_params=_cparams("arbitrary"),
        name="fox_sample",
    )(page_table, x3, x3, x3, x3, bf_row, hsel, *([pk] * n_pages), *([pv] * n_pages), *([pool_lf] * n_pages))
    return lf, o.reshape(n_seq * t, W_D)


def _odd_weights(cw, cb, wa, ba, wx, bx, lam):
    def bd(w):
        return jax.scipy.linalg.block_diag(*[w[g] for g in range(NB_C)]).astype(BF16)
    r = lambda v: v.reshape(1, W_C).astype(F32)
    return cw.astype(F32), r(cb), bd(wa), r(ba), bd(wx), r(bx), r(jax.nn.softplus(-lam.astype(F32)))


def _router_kernel(x_ref, g_ref, r0_ref, r1_ref, r2_ref, o_ref, lg_ref):
    x = x_ref[...]
    y = x * lax.rsqrt(jnp.mean(x * x, axis=-1, keepdims=True) + EPS) * g_ref[...]
    o_ref[...] = y.astype(o_ref.dtype)
    y0, y1, y2 = _split3(y)
    r0, r1, r2 = r0_ref[...], r1_ref[...], r2_ref[...]
    lg_ref[...] = (_dot(y0, r0) + (_dot(y0, r1) + _dot(y1, r0)) + (_dot(y0, r2) + _dot(y1, r1) + _dot(y2, r0)))


def _rmsnorm_router(x, g, router, tm=512):
    m, d = x.shape
    rp = jnp.zeros((d, LANES), F32).at[:, :router.shape[1]].set(router.astype(F32))
    r0 = rp.astype(BF16)
    r1 = (rp - r0.astype(F32)).astype(BF16)
    r2 = (rp - r0.astype(F32) - r1.astype(F32)).astype(BF16)
    rspec = pl.BlockSpec((d, LANES), lambda i: (0, 0))
    return pl.pallas_call(
        _router_kernel,
        grid=(m // tm,),
        in_specs=[pl.BlockSpec((tm, d), lambda i: (i, 0)), pl.BlockSpec((1, d), lambda i: (0, 0)), rspec, rspec, rspec],
        out_specs=[pl.BlockSpec((tm, d), lambda i: (i, 0)), pl.BlockSpec((tm, LANES), lambda i: (i, 0))],
        out_shape=[jax.ShapeDtypeStruct((m, d), BF16), jax.ShapeDtypeStruct((m, LANES), F32)],
        compiler_params=_cparams("parallel"),
        name="rmsnorm_router",
    )(x, g.reshape(1, d), r0, r1, r2)


def _combine_norm_kernel(y_ref, ya_ref, yb_ref, gt_ref, g_ref, o_ref):
    gt = gt_ref[...]
    x = y_ref[...] + (gt[:, 0:1] * ya_ref[...] + gt[:, 1:2] * yb_ref[...])
    o_ref[...] = x * lax.rsqrt(jnp.mean(x * x, axis=-1, keepdims=True) + EPS) * g_ref[...]


def _combine_norm(y, ya, yb, gates, g, tm=512):
    m, d = y.shape
    row = pl.BlockSpec((tm, d), lambda i: (i, 0))
    return pl.pallas_call(
        _combine_norm_kernel,
        grid=(m // tm,),
        in_specs=[row, row, row, pl.BlockSpec((tm, TOP_K), lambda i: (i, 0)), pl.BlockSpec((1, d), lambda i: (0, 0))],
        out_specs=row,
        out_shape=jax.ShapeDtypeStruct((m, d), F32),
        compiler_params=_cparams("parallel"),
        name="combine_norm",
    )(y, ya, yb, gates, g.reshape(1, d))


def _moe_dispatch(logits, tm):
    m = logits.shape[0]
    top_v, top_i = lax.top_k(logits, TOP_K)
    gates = jax.nn.softmax(top_v, axis=-1)
    flat_e = top_i.reshape(-1).astype(jnp.int32)
    n_pairs = flat_e.shape[0]
    order = jnp.argsort(flat_e, stable=True).astype(jnp.int32)
    sorted_e = flat_e[order]
    counts = jnp.zeros((N_EXPERTS,), jnp.int32).at[flat_e].add(1)
    padded = -(-counts // tm) * tm
    start = jnp.cumsum(counts) - counts
    pstart = jnp.cumsum(padded) - padded
    dest = pstart[sorted_e] + (jnp.arange(n_pairs, dtype=jnp.int32) - start[sorted_e])
    n_tiles = n_pairs // tm + N_EXPERTS
    src_token = jnp.zeros((n_tiles * tm,), jnp.int32).at[dest].set(order // TOP_K)
    pair_pos = jnp.zeros((n_pairs,), jnp.int32).at[order].set(dest).reshape(m, TOP_K)
    tile_expert = jnp.searchsorted(jnp.cumsum(padded), jnp.arange(n_tiles, dtype=jnp.int32) * tm, side="right")
    tile_expert = jnp.minimum(tile_expert, N_EXPERTS - 1).astype(jnp.int32)
    return gates, src_token, pair_pos, tile_expert


def kernel(x_prompt, x_sample, cache_k_e, cache_v_e, state_s_e, state_conv_o, state_h_o, cache_k_o, cache_v_o,
           cache_logf_o, page_table, w_in_e, lb_logits, gnorm_a, w_out_e, ffn_w1, ffn_w3, ffn_w2, w_in_o, conv_w,
           conv_b, rg_wa, rg_ba, rg_wx, rg_bx, rg_lambda, fox_bf, w_out_o, moe_router, moe_w1, moe_w3, moe_w2,
           norm_mix, norm_ffn, norm_final):
    batch, seq, d = x_prompt.shape
    n_seq, ts, _ = x_sample.shape
    mp, ms = batch * seq, n_seq * ts
    m = mp + ms
    n_pages = page_table.shape[1]
    past = n_pages * PAGE_SIZE
    chunk_a = 64
    assert d == D_MODEL and ts <= SUBLANES and seq % chunk_a == 0 and m % chunk_a == 0 and mp % ts == 0

    x = jnp.concatenate([x_prompt.reshape(mp, d), x_sample.reshape(ms, d)], axis=0)
    pos_p = jnp.arange(seq, dtype=jnp.int32)
    pos_s = past + jnp.arange(ts, dtype=jnp.int32)

    hn = _rmsnorm(x, norm_mix[0], BF16)
    proj = _matmul([hn], w_in_e[0].astype(BF16))
    lb = jnp.cumsum(jax.nn.softmax(lb_logits.astype(F32), axis=0), axis=0)[0]
    oa_p, s_p = _hgrn2(proj, 0, batch, seq // chunk_a, chunk_a, jnp.zeros((batch, H_A, DK_A, DV_A), F32), lb, gnorm_a[0])
    oa_s, s_s = _hgrn2(proj, mp, n_seq, 1, ts, state_s_e[0], lb, gnorm_a[0])
    cos_p, sin_p = _rope_tables(pos_p)
    cos_s, sin_s = _rope_tables(pos_s)
    ke_p, qa, ka, va = _moba_prep(proj, batch, seq, cos_p, sin_p)
    ob_p = _flash(qa, ka, va)
    ke_s, ob_s = _moba_sample(proj, mp, n_seq, ts, cache_k_e[0], cache_v_e[0], page_table, cos_s, sin_s)
    ve = proj[:, COLS_E_V:COLS_E_V + W_B]
    y = _matmul([jnp.concatenate([oa_p, oa_s], axis=0), jnp.concatenate([ob_p, ob_s], axis=0)],
                w_out_e[0].astype(BF16), res=x)
    hn = _rmsnorm(y, norm_ffn[0], BF16)
    y = _ffn(hn, ffn_w1.astype(BF16), ffn_w3.astype(BF16), ffn_w2.astype(BF16), res=y)

    hn = _rmsnorm(y, norm_mix[1], BF16)
    cols_o = w_in_o.shape[2]
    cols_pad = -(-cols_o // (7 * LANES)) * (7 * LANES)
    w_in_o_p = jnp.zeros((d, cols_pad), BF16).at[:, :cols_o].set(w_in_o[0].astype(BF16))
    proj_o = _matmul([hn], w_in_o_p, tn=7 * LANES)
    ow = _odd_weights(conv_w[0], conv_b[0], rg_wa[0], rg_ba[0], rg_wx[0], rg_bx[0], rg_lambda[0])
    oc_p, buf_p, h_p = _rglru_prompt(proj_o, batch, seq, *ow)
    ps = proj_o[mp:].reshape(n_seq, ts, cols_pad)
    cx_s = ps[..., :W_C]
    oc_s, h_s = _rglru_sample(cx_s.swapaxes(0, 1), ps[..., W_C:2 * W_C].swapaxes(0, 1), state_conv_o[0].swapaxes(0, 1),
                              state_h_o[0], past == 0, *ow)
    oc_s = oc_s.swapaxes(0, 1).reshape(ms, W_C)
    buf_s = jnp.concatenate([state_conv_o[0].astype(F32), cx_s], axis=1)[:, ts:]
    bf_row = jnp.zeros((1, LANES), F32).at[0, :H_D].set(fox_bf[0].astype(F32))
    selq, selk = _fox_selectors()
    lf_p, qa, ka, va = _fox_prep(proj_o, batch, seq, bf_row, selq, selk)
    od_p = _flash(qa, ka, va)
    lf_s, od_s = _fox_sample(proj_o, mp, n_seq, ts, cache_k_o[0], cache_v_o[0], cache_logf_o[0], page_table, bf_row)
    ko = proj_o[:, 2 * W_C + W_D:2 * W_C + 2 * W_D]
    vo = proj_o[:, 2 * W_C + 2 * W_D:2 * W_C + 3 * W_D]
    y = _matmul([jnp.concatenate([oc_p, oc_s], axis=0), jnp.concatenate([od_p, od_s], axis=0)],
                w_out_o[0].astype(BF16), res=y)
    tm_e = 512
    hn, logits = _rmsnorm_router(y, norm_ffn[1], moe_router[0])
    gates, src_token, pair_pos, tile_expert = _moe_dispatch(logits[:, :N_EXPERTS], tm_e)
    y_e = _ffn(jnp.take(hn, src_token, axis=0), moe_w1[0], moe_w3[0], moe_w2[0], tile_expert=tile_expert, tm=tm_e, tf=512)
    out = _combine_norm(y, jnp.take(y_e, pair_pos[:, 0], axis=0), jnp.take(y_e, pair_pos[:, 1], axis=0), gates, norm_final)

    def heads(a, n, t, h):
        return a.reshape(1, n, t, h, a.shape[-1] // h)

    return (out[:mp].reshape(batch, seq, d), out[mp:].reshape(n_seq, ts, d),
            s_p[None], s_s[None],
            heads(ke_p, batch, seq, H_B), heads(ve[:mp], batch, seq, H_B),
            heads(ke_s, n_seq, ts, H_B), heads(ve[mp:], n_seq, ts, H_B),
            buf_p[None], buf_s[None], h_p[None], h_s[None],
            heads(ko[:mp], batch, seq, H_D), heads(vo[:mp], batch, seq, H_D), lf_p.reshape(1, batch, seq, H_D),
            heads(ko[mp:], n_seq, ts, H_D), heads(vo[mp:], n_seq, ts, H_D), lf_s[None])


def _mock_moba(compile_fn, S):
    def f(proj):
        cos2, sin2 = _rope_tables(jnp.arange(8192, dtype=jnp.int32))
        ke, qa, ka, va = _moba_prep(proj, 2, 8192, cos2, sin2)
        return ke, _flash(qa, ka, va)
    compile_fn(f, S((16896, 3584)))


def _mock_moba_sample(compile_fn, S):
    def f(proj, pk, pv, pt):
        cos2, sin2 = _rope_tables(2048 + jnp.arange(4, dtype=jnp.int32))
        return _moba_sample(proj, 16384, 128, 4, pk, pv, pt, cos2, sin2)
    compile_fn(f, S((16896, 3584)), S((2560, 128, 8, 64)), S((2560, 128, 8, 64)), S((128, 16), jnp.int32))


def _mock_rglru(compile_fn, S):
    v = S((1, 512)); mt = S((512, 512), BF16)
    compile_fn(lambda p, cw, cb, wa, ba, wx, bx, sp: _rglru_prompt(p, 2, 8192, cw, cb, wa, ba, wx, bx, sp),
               S((16896, 2688)), S((4, 512)), v, mt, v, mt, v, v)
    compile_fn(lambda cx, cg, buf, h0, cw, cb, wa, ba, wx, bx, sp: _rglru_sample(cx, cg, buf, h0, False, cw, cb, wa, ba, wx, bx, sp),
               S((4, 128, 512)), S((4, 128, 512)), S((3, 128, 512)), S((128, 512)), S((4, 512)), v, mt, v, mt, v, v)


def _mock_fox(compile_fn, S):
    def f(proj, bf, pk, pv, plf, pt):
        selq, selk = _fox_selectors()
        lf, qa, ka, va = _fox_prep(proj, 2, 8192, bf, selq, selk)
        lfs, os_ = _fox_sample(proj, 16384, 128, 4, pk, pv, plf, pt, bf)
        return lf, _flash(qa, ka, va), lfs, os_
    compile_fn(f, S((16896, 2688)), S((1, 128)), S((2560, 128, 8, 64)), S((2560, 128, 8, 64)), S((2560, 128, 8)), S((128, 16), jnp.int32))
```

```python
import functools
import math

import numpy as np
import jax
import jax.numpy as jnp
from jax import lax
from jax.experimental import pallas as pl
from jax.experimental.pallas import tpu as pltpu

F32 = jnp.float32
BF16 = jnp.bfloat16

D_MODEL = 1024
PAGE_SIZE = 128
H_A, DK_A, DV_A = 4, 128, 128
F_A, W_A = H_A * DK_A, H_A * DV_A
H_B, HD_B = 8, 64
W_B = H_B * HD_B
MOBA_BLOCK, MOBA_TOPK = 256, 3
W_C, NB_C, CONV_W, RG_C = 512, 8, 4, 8.0
BW_C = W_C // NB_C
H_D, HD_D = 8, 64
W_D = H_D * HD_D
N_EXPERTS, TOP_K = 8, 2
ROPE_THETA = 10000.0
EPS = 1e-6
NEG_INF = -1e30

COLS_E_Q = 2 * F_A + 2 * W_A
COLS_O_Q = 2 * W_C
COLS_O_F = 2 * W_C + 3 * W_D

LANES = 128
SUBLANES = 8
VMEM_LIMIT = 56 * 1024 * 1024
ATT_TILE = 256


def _cparams(*sem):
    return pltpu.CompilerParams(dimension_semantics=sem, vmem_limit_bytes=VMEM_LIMIT)


def _split3(x):
    hi = x.astype(BF16)
    r1 = x - hi.astype(F32)
    mid = r1.astype(BF16)
    lo = (r1 - mid.astype(F32)).astype(BF16)
    return hi, mid, lo


def _dot(a, b):
    return jnp.dot(a, b, preferred_element_type=F32)


def _dot_nt(a, b):
    return lax.dot_general(a, b, (((1,), (1,)), ((), ())), preferred_element_type=F32)


def _dot_tn(a, b):
    return lax.dot_general(a, b, (((0,), (0,)), ((), ())), preferred_element_type=F32)


def _dot3(a, b):
    b0, b1, b2 = _split3(b)
    return _dot(a, b0) + _dot(a, b1) + _dot(a, b2)


def _sigmoid(x):
    return 1.0 / (1.0 + jnp.exp(-x))


def _log_sigmoid(x):
    return jnp.minimum(x, 0.0) - jnp.log(1.0 + jnp.exp(-jnp.abs(x)))


def _gelu_tanh(x):
    return 0.5 * x * (1.0 + jnp.tanh(math.sqrt(2.0 / math.pi) * (x + 0.044715 * (x * x * x))))


def _rmsnorm_kernel(x_ref, g_ref, o_ref):
    x = x_ref[...]
    y = x * lax.rsqrt(jnp.mean(x * x, axis=-1, keepdims=True) + EPS) * g_ref[...]
    o_ref[...] = y.astype(o_ref.dtype)


def _rmsnorm(x, g, out_dtype, tm=512):
    m, d = x.shape
    return pl.pallas_call(
        _rmsnorm_kernel,
        grid=(m // tm,),
        in_specs=[pl.BlockSpec((tm, d), lambda i: (i, 0)), pl.BlockSpec((1, d), lambda i: (0, 0))],
        out_specs=pl.BlockSpec((tm, d), lambda i: (i, 0)),
        out_shape=jax.ShapeDtypeStruct((m, d), out_dtype),
        compiler_params=_cparams("parallel"),
        name="rmsnorm",
    )(x, g.reshape(1, d))


def _matmul_kernel(*refs, n_x, has_res):
    xs = refs[:n_x]
    w_ref = refs[n_x]
    res_ref = refs[n_x + 1] if has_res else None
    o_ref = refs[-1]
    acc = None
    k0 = 0
    for x_ref in xs:
        kk = x_ref.shape[1]
        part = _dot(x_ref[...].astype(BF16), w_ref[k0:k0 + kk, :])
        acc = part if acc is None else acc + part
        k0 += kk
    if has_res:
        acc = acc + res_ref[...]
    o_ref[...] = acc


def _matmul(xs, w, res=None, tm=512):
    m = xs[0].shape[0]
    kt, n = w.shape
    in_specs = [pl.BlockSpec((tm, x.shape[1]), lambda i: (i, 0)) for x in xs]
    in_specs.append(pl.BlockSpec((kt, n), lambda i: (0, 0)))
    args = list(xs) + [w]
    if res is not None:
        in_specs.append(pl.BlockSpec((tm, n), lambda i: (i, 0)))
        args.append(res)
    return pl.pallas_call(
        functools.partial(_matmul_kernel, n_x=len(xs), has_res=res is not None),
        grid=(m // tm,),
        in_specs=in_specs,
        out_specs=pl.BlockSpec((tm, n), lambda i: (i, 0)),
        out_shape=jax.ShapeDtypeStruct((m, n), F32),
        compiler_params=_cparams("parallel"),
        name="matmul",
    )(*args)


def _ffn_kernel(te_ref, x_ref, w1_ref, w3_ref, w2_ref, *rest, has_res):
    del te_ref
    if has_res:
        res_ref, o_ref, acc_ref = rest
    else:
        o_ref, acc_ref = rest
    f = pl.program_id(1)

    @pl.when(f == 0)
    def _():
        acc_ref[...] = jnp.zeros_like(acc_ref)

    x = x_ref[...]
    a = _dot(x, w1_ref[...].astype(BF16))
    b = _dot(x, w3_ref[...].astype(BF16))
    g = (a * _sigmoid(a) * b).astype(BF16)
    acc_ref[...] += _dot(g, w2_ref[...].astype(BF16))

    @pl.when(f == pl.num_programs(1) - 1)
    def _():
        out = acc_ref[...]
        if has_res:
            out = out + res_ref[...]
        o_ref[...] = out


def _ffn(x, w1, w3, w2, res=None, tile_expert=None, tm=512, tf=256):
    m, d = x.shape
    f = w1.shape[-1]
    if tile_expert is None:
        tile_expert = jnp.zeros((m // tm,), jnp.int32)
    in_specs = [
        pl.BlockSpec((tm, d), lambda i, j, te: (i, 0)),
        pl.BlockSpec((None, d, tf), lambda i, j, te: (te[i], 0, j)),
        pl.BlockSpec((None, d, tf), lambda i, j, te: (te[i], 0, j)),
        pl.BlockSpec((None, tf, d), lambda i, j, te: (te[i], j, 0)),
    ]
    args = [x, w1, w3, w2]
    if res is not None:
        in_specs.append(pl.BlockSpec((tm, d), lambda i, j, te: (i, 0)))
        args.append(res)
    return pl.pallas_call(
        functools.partial(_ffn_kernel, has_res=res is not None),
        grid_spec=pltpu.PrefetchScalarGridSpec(
            num_scalar_prefetch=1,
            grid=(m // tm, f // tf),
            in_specs=in_specs,
            out_specs=pl.BlockSpec((tm, d), lambda i, j, te: (i, 0)),
            scratch_shapes=[pltpu.VMEM((tm, d), F32)],
        ),
        out_shape=jax.ShapeDtypeStruct((m, d), F32),
        compiler_params=_cparams("parallel", "arbitrary"),
        name="ffn",
    )(tile_expert, *args)


def _hgrn2_head(q, k, v, logf, st, c_sub):
    c = q.shape[0]
    ns = c // c_sub
    row = lax.broadcasted_iota(jnp.int32, (c, c), 0)
    col = lax.broadcasted_iota(jnp.int32, (c, c), 1)
    tri = (row >= col).astype(BF16)
    g = _dot3(tri, logf)
    vb = v.astype(BF16)
    o = _dot_nt((q * jnp.exp(g)).astype(BF16), st.astype(BF16))
    srow = lax.broadcasted_iota(jnp.int32, (c_sub, c_sub), 0)
    scol = lax.broadcasted_iota(jnp.int32, (c_sub, c_sub), 1)
    o_rows = []
    for i in range(ns):
        r0 = i * c_sub
        qi, ki, gi = q[r0:r0 + c_sub], k[r0:r0 + c_sub], g[r0:r0 + c_sub]
        blk = jnp.zeros((c_sub, c_sub), F32)
        for s in range(c_sub):
            w = jnp.exp(jnp.minimum(gi - gi[s:s + 1, :], 0.0))
            colv = jnp.sum(qi * w * ki[s:s + 1, :], axis=-1, keepdims=True)
            blk = jnp.where(scol == s, colv, blk)
        blk = jnp.where(srow >= scol, blk, 0.0)
        oi = _dot(blk.astype(BF16), vb[r0:r0 + c_sub])
        if i > 0:
            gb = g[r0 - 1:r0, :]
            qs = qi * jnp.exp(gi - gb)
            ks = k[0:r0] * jnp.exp(gb - g[0:r0])
            sc = _dot_nt(qs.astype(BF16), ks.astype(BF16))
            oi = oi + _dot(sc.astype(BF16), vb[0:r0])
        o_rows.append(oi)
    o = o + (o_rows[0] if ns == 1 else jnp.concatenate(o_rows, axis=0))
    gl = g[c - 1:c, :]
    ke = (k * jnp.exp(gl - g)).astype(BF16)
    st_new = jnp.exp(gl) * st + _dot_tn(vb, ke)
    return o, st_new


def _hgrn2_kernel(aq_ref, af_ref, ai_ref, ag_ref, lb_ref, gn_ref, s0_ref, o_ref, s_ref, st_scr, *, c_sub, ct):
    t = pl.program_id(1)

    @pl.when(t == 0)
    def _():
        for h in range(H_A):
            st_scr[h] = s0_ref[h].T

    def padded(ref):
        x = ref[...]
        if ct < SUBLANES:
            x = jnp.concatenate([x, jnp.zeros((SUBLANES - ct, x.shape[1]), F32)], axis=0)
        return x

    aq, zf, vi = padded(aq_ref), padded(af_ref), padded(ai_ref)
    c = aq.shape[0]
    lb = lb_ref[...]
    real = lax.broadcasted_iota(jnp.int32, (c, F_A), 0) < ct
    sig = _sigmoid(zf)
    q = aq * _sigmoid(aq)
    logf = jnp.where(real, jnp.log(lb + (1.0 - lb) * sig), 0.0)
    k = jnp.where(real, (1.0 - lb) * _sigmoid(-zf), 0.0)
    outs = []
    for h in range(H_A):
        sl = slice(h * DK_A, (h + 1) * DK_A)
        o_h, st_new = _hgrn2_head(q[:, sl], k[:, sl], vi[:, sl], logf[:, sl], st_scr[h], c_sub)
        st_scr[h] = st_new
        outs.append(o_h)
    o = jnp.concatenate(outs, axis=1)[:ct]
    o = o * lax.rsqrt(jnp.mean(o * o, axis=-1, keepdims=True) + EPS) * gn_ref[...]
    o_ref[...] = o * _sigmoid(ag_ref[...])

    @pl.when(t == pl.num_programs(1) - 1)
    def _():
        for h in range(H_A):
            s_ref[h] = st_scr[h].T


def _hgrn2(x3, n_seq, n_chunks, s0, lb, gnorm):
    ct = x3.shape[1]
    c_sub = min(16, max(ct, SUBLANES))

    def tok_spec(cb):
        return pl.BlockSpec((None, ct, F_A), lambda n, t: (n * n_chunks + t, 0, cb))

    o, s = pl.pallas_call(
        functools.partial(_hgrn2_kernel, c_sub=c_sub, ct=ct),
        grid=(n_seq, n_chunks),
        in_specs=[tok_spec(0), tok_spec(1), tok_spec(2), tok_spec(3),
                  pl.BlockSpec((1, F_A), lambda n, t: (0, 0)),
                  pl.BlockSpec((1, W_A), lambda n, t: (0, 0)),
                  pl.BlockSpec((None, H_A, DK_A, DV_A), lambda n, t: (n, 0, 0, 0))],
        out_specs=[pl.BlockSpec((None, ct, W_A), lambda n, t: (n * n_chunks + t, 0, 0)),
                   pl.BlockSpec((None, H_A, DK_A, DV_A), lambda n, t: (n, 0, 0, 0))],
        out_shape=[jax.ShapeDtypeStruct((n_seq * n_chunks, ct, W_A), F32),
                   jax.ShapeDtypeStruct((n_seq, H_A, DK_A, DV_A), F32)],
        scratch_shapes=[pltpu.VMEM((H_A, DV_A, DK_A), F32)],
        compiler_params=_cparams("parallel", "arbitrary"),
        name="hgrn2",
    )(x3, x3, x3, x3, lb.reshape(1, F_A), gnorm.reshape(1, W_A), s0)
    return o.reshape(n_seq * n_chunks * ct, W_A), s


def _rope_tables(pos):
    half = HD_B // 2
    inv = ROPE_THETA ** (-jnp.arange(half, dtype=F32) / half)
    ang = pos.astype(F32)[:, None] * inv[None, :]
    cos, sin = jnp.cos(ang), jnp.sin(ang)
    return jnp.concatenate([cos, cos], axis=1), jnp.concatenate([-sin, sin], axis=1)


def _rope(x, cos2, sin2, lane):
    swapped = jnp.where((lane & (HD_B - 1)) >= HD_B // 2, pltpu.roll(x, HD_B // 2, 1), pltpu.roll(x, LANES - HD_B // 2, 1))
    return x * cos2 + swapped * sin2


def _dot_nt_f32(a, b):
    a0, a1, a2 = _split3(a)
    b0, b1, b2 = _split3(b)
    return (_dot_nt(a0, b0) + (_dot_nt(a0, b1) + _dot_nt(a1, b0))
            + (_dot_nt(a0, b2) + _dot_nt(a1, b1) + _dot_nt(a2, b0)))


def _top_select(g, index, axis, n_pick):
    sel = jnp.zeros(g.shape, F32)
    big = jnp.int32(1 << 20)
    for _ in range(n_pick):
        m = jnp.max(g, axis=axis, keepdims=True)
        first = jnp.min(jnp.where(g == m, index, big), axis=axis, keepdims=True)
        pick = jnp.where((index == first) & (m > 0.5 * NEG_INF), 1.0, 0.0)
        sel = sel + pick
        g = jnp.where(pick > 0.5, NEG_INF, g)
    return sel


def _store_heads_t(refs, c, x):
    xt = x.T
    for ref in refs:
        ref[2 * c] = xt[:HD_B].astype(ref.dtype)
        ref[2 * c + 1] = xt[HD_B:].astype(ref.dtype)


def _moba_prep_kernel(q_ref, k_ref, v_ref, cos_ref, sin_ref, kt_ref, vl_ref, qa_ref, ka_ref, vt_ref, mrow_scr):
    t = pl.program_id(1)
    rows = q_ref.shape[0]

    @pl.when(t == 0)
    def _():
        mrow_scr[...] = jnp.zeros_like(mrow_scr)

    lane = lax.broadcasted_iota(jnp.int32, (rows, LANES), 1)
    lane1 = lax.broadcasted_iota(jnp.int32, (1, LANES), 1)
    low = lane < HD_B
    blk = lane & (HD_B - 1)
    cos2, sin2 = cos_ref[...], sin_ref[...]
    scale = HD_B ** -0.5
    for c in range(W_B // LANES):
        sl = slice(c * LANES, (c + 1) * LANES)
        qr = _rope(q_ref[:, sl], cos2, sin2, lane)
        kr = _rope(k_ref[:, sl], cos2, sin2, lane)
        _store_heads_t([kt_ref], c, kr)
        gate = _dot_nt_f32(qr, mrow_scr[c])
        g = jnp.where(blk < t, gate, NEG_INF)
        sel = (_top_select(jnp.where(low, NEG_INF, g), lane, 1, MOBA_TOPK)
               + _top_select(jnp.where(low, g, NEG_INF), lane, 1, MOBA_TOPK))
        msel = jnp.where((sel > 0.5) | (blk == t), 0.0, NEG_INF)
        own = jnp.where(blk == t, 1.0, 0.0)
        qs = qr * scale
        qa_ref[2 * c] = jnp.where(low, qs, msel).astype(BF16)
        qa_ref[2 * c + 1] = jnp.where(low, msel, qs).astype(BF16)
        ka_ref[2 * c] = jnp.where(low, kr, own).astype(BF16)
        ka_ref[2 * c + 1] = jnp.where(low, own, kr).astype(BF16)
        _store_heads_t([vl_ref, vt_ref], c, v_ref[:, sl])
        mean = jnp.mean(kr, axis=0, keepdims=True)
        mrow_scr[c, pl.ds(HD_B + t, 1), :] = jnp.where(lane1 < HD_B, mean, 0.0)
        mrow_scr[c, pl.ds(t, 1), :] = jnp.where(lane1 < HD_B, 0.0, mean)


def _pair_specs(batch, heads, seq):
    nt = seq // ATT_TILE
    pair = pl.BlockSpec((None, heads, ATT_TILE, LANES), lambda b, t: (b, 0, t, 0))
    pair_shape = jax.ShapeDtypeStruct((batch, heads, seq, LANES), BF16)
    vt = pl.BlockSpec((None, heads, None, LANES // 2, ATT_TILE), lambda b, t: (b, 0, t, 0, 0))
    vt_shape = jax.ShapeDtypeStruct((batch, heads, nt, LANES // 2, ATT_TILE), BF16)
    leaf = pl.BlockSpec((None, heads, LANES // 2, ATT_TILE), lambda b, t: (b, 0, 0, t))
    leaf_shape = jax.ShapeDtypeStruct((batch, heads, LANES // 2, seq), F32)
    return [leaf, leaf, pair, pair, vt], [leaf_shape, leaf_shape, pair_shape, pair_shape, vt_shape]


def _moba_prep(proj, batch, seq, cos2, sin2):
    nb = seq // MOBA_BLOCK
    assert seq % MOBA_BLOCK == 0 and nb <= HD_B and MOBA_BLOCK == ATT_TILE
    rows = MOBA_BLOCK

    def tok(cb):
        return pl.BlockSpec((rows, W_B), lambda b, t: (b * nb + t, cb))

    specs, shapes = _pair_specs(batch, H_B, seq)
    cb0 = COLS_E_Q // W_B
    return pl.pallas_call(
        _moba_prep_kernel,
        grid=(batch, nb),
        in_specs=[tok(cb0), tok(cb0 + 1), tok(cb0 + 2),
                  pl.BlockSpec((rows, LANES), lambda b, t: (t, 0)),
                  pl.BlockSpec((rows, LANES), lambda b, t: (t, 0))],
        out_specs=specs,
        out_shape=shapes,
        scratch_shapes=[pltpu.VMEM((W_B // LANES, LANES, LANES), F32)],
        compiler_params=_cparams("parallel", "arbitrary"),
        name="moba_prep",
    )(proj, proj, proj, jnp.concatenate([cos2, cos2], axis=1), jnp.concatenate([sin2, sin2], axis=1))


def _flash_kernel(qa_ref, ka_ref, vt_ref, o_ref):
    i = pl.program_id(2)
    tq = qa_ref.shape[1]
    hd, tv = vt_ref.shape[2], vt_ref.shape[3]
    sub = tq // tv
    krow = lax.broadcasted_iota(jnp.int32, (tq, tq), 0)
    qcol = lax.broadcasted_iota(jnp.int32, (tq, tq), 1)
    qs = [qa_ref[0], qa_ref[1]]

    def logits(hh, j, diag):
        k = ka_ref[hh, pl.ds(pl.multiple_of(j * tq, tq), tq), :]
        s = _dot_nt(k, qs[hh])
        return jnp.where(krow <= qcol, s, NEG_INF) if diag else s

    def fold(x, op):
        return op(x.reshape(tq // SUBLANES, SUBLANES, tq), axis=0)

    def max_step(j, carry, diag):
        return tuple(jnp.maximum(carry[hh], fold(logits(hh, j, diag), jnp.max)) for hh in range(2))

    macc = tuple(jnp.full((SUBLANES, tq), -jnp.inf, F32) for _ in range(2))
    macc = lax.fori_loop(0, i, functools.partial(max_step, diag=False), macc)
    macc = max_step(i, macc, True)
    ms = [jnp.max(a, axis=0, keepdims=True) for a in macc]

    def sum_step(j, carry, diag):
        new = []
        for hh in range(2):
            lacc, acc = carry[hh]
            p = jnp.exp(logits(hh, j, diag) - ms[hh])
            lacc = lacc + fold(p, jnp.sum)
            pb = p.astype(BF16)
            for u in range(sub):
                acc = acc + _dot(vt_ref[hh, j * sub + u], pb[u * tv:(u + 1) * tv])
            new.append((lacc, acc))
        return tuple(new)

    init = tuple((jnp.zeros((SUBLANES, tq), F32), jnp.zeros((hd, tq), F32)) for _ in range(2))
    carry = lax.fori_loop(0, i, functools.partial(sum_step, diag=False), init)
    carry = sum_step(i, carry, True)
    ot = jnp.concatenate([acc / jnp.sum(lacc, axis=0, keepdims=True) for lacc, acc in carry], axis=0)
    o_ref[...] = ot.T


def _flash(qa, ka, vt, tq=512):
    batch, heads, seq, _ = qa.shape
    tq = min(tq, seq)
    nq = seq // tq
    nv, hd, tv = vt.shape[2:]
    return pl.pallas_call(
        _flash_kernel,
        grid=(batch, heads // 2, nq),
        in_specs=[pl.BlockSpec((None, 2, tq, LANES), lambda b, c, i: (b, c, i, 0)),
                  pl.BlockSpec((None, 2, seq, LANES), lambda b, c, i: (b, c, 0, 0)),
                  pl.BlockSpec((None, 2, nv, hd, tv), lambda b, c, i: (b, c, 0, 0, 0))],
        out_specs=pl.BlockSpec((tq, LANES), lambda b, c, i: (b * nq + i, c)),
        out_shape=jax.ShapeDtypeStruct((batch * seq, heads * LANES // 2), F32),
        compiler_params=_cparams("parallel", "parallel", "arbitrary"),
        name="flash",
    )(qa, ka, vt)


def _paged_specs(n_pages, tail):
    zeros = (0,) * len(tail)
    return [pl.BlockSpec((None,) + tail, functools.partial(lambda n, pt, p: (pt[n, p],) + zeros, p=p))
            for p in range(n_pages)]


def _pool_pages(pool):
    n_pool, page, h, dd = pool.shape
    return jnp.transpose(pool, (0, 2, 3, 1)).reshape(n_pool, h * dd, page)


def _pad_rows(x, rows):
    return jnp.concatenate([x, jnp.zeros((rows - x.shape[0], x.shape[1]), x.dtype)], axis=0)


def _query_rows(q, lane, rowh):
    t, w = q.shape
    rep = jnp.broadcast_to(q[:, None, :], (t, H_B, w)).reshape(t * H_B, w)
    return jnp.where((lane // HD_B) == rowh, rep, 0.0)


def _head_rows_to_tokens(o, lane, rowh, t):
    o = jnp.where((lane // HD_B) == rowh, o, 0.0)
    return jnp.sum(o.reshape(t, H_B, o.shape[1]), axis=1)


def _sample_softmax_pv(s_new, s_pages, v_new, vp, o_ref, lane, rowh, t):
    m = jnp.max(s_new, axis=1, keepdims=True)
    mm = s_pages[0]
    for s in s_pages[1:]:
        mm = jnp.maximum(mm, s)
    m = jnp.maximum(m, jnp.max(mm, axis=1, keepdims=True))
    p_new = jnp.exp(s_new - m)
    o = _dot(p_new.astype(BF16), v_new.astype(BF16))
    lsum = None
    for p, s in enumerate(s_pages):
        pp = jnp.exp(s - m)
        lsum = pp if lsum is None else lsum + pp
        o = o + _dot_nt(pp.astype(BF16), vp[p][...].astype(BF16))
    l = jnp.sum(p_new, axis=1, keepdims=True) + jnp.sum(lsum, axis=1, keepdims=True)
    o_ref[...] = _head_rows_to_tokens(o / l, lane, rowh, t)


def _moba_sample_kernel(pt_ref, q_ref, k_ref, v_ref, cos_ref, sin_ref, *rest, n_pages):
    del pt_ref
    kp, vp = rest[:n_pages], rest[n_pages:2 * n_pages]
    ke_ref, o_ref = rest[2 * n_pages:]
    t = q_ref.shape[0]
    nc = t * H_B
    nbp = n_pages * PAGE_SIZE // MOBA_BLOCK
    ppb = MOBA_BLOCK // PAGE_SIZE
    lane8 = lax.broadcasted_iota(jnp.int32, (SUBLANES, LANES), 1)
    cos2, sin2 = _pad_rows(cos_ref[...], SUBLANES), _pad_rows(sin_ref[...], SUBLANES)
    q8, k8, v8 = _pad_rows(q_ref[...], SUBLANES), _pad_rows(k_ref[...], SUBLANES), _pad_rows(v_ref[...], SUBLANES)
    qr = jnp.concatenate([_rope(q8[:, c * LANES:(c + 1) * LANES], cos2, sin2, lane8) for c in range(W_B // LANES)], axis=1)
    kr = jnp.concatenate([_rope(k8[:, c * LANES:(c + 1) * LANES], cos2, sin2, lane8) for c in range(W_B // LANES)], axis=1)
    ke_ref[...] = kr[:t]

    lane = lax.broadcasted_iota(jnp.int32, (nc, W_B), 1)
    rowh = lax.broadcasted_iota(jnp.int32, (nc, W_B), 0) % H_B
    qrows = _query_rows(qr[:t], lane, rowh)

    lanem = lax.broadcasted_iota(jnp.int32, (W_B, LANES), 1)
    means = jnp.zeros((W_B, LANES), F32)
    for j in range(nbp):
        tot = jnp.sum(kp[ppb * j][...], axis=1, keepdims=True)
        for u in range(1, ppb):
            tot = tot + jnp.sum(kp[ppb * j + u][...], axis=1, keepdims=True)
        means = jnp.where(lanem == j, tot * (1.0 / MOBA_BLOCK), means)
    q0, q1, q2 = _split3(qrows)
    m0, m1, m2 = _split3(means)
    gate = _dot(q0, m0) + (_dot(q0, m1) + _dot(q1, m0)) + (_dot(q0, m2) + _dot(q1, m1) + _dot(q2, m0))
    lg = lax.broadcasted_iota(jnp.int32, (nc, LANES), 1)
    sel = _top_select(jnp.where(lg < nbp, gate, NEG_INF), lg, 1, MOBA_TOPK)

    qb = (qrows * (HD_B ** -0.5)).astype(BF16)
    s_pages = []
    for p in range(n_pages):
        s = _dot(qb, kp[p][...].astype(BF16))
        j = p // ppb
        s_pages.append(jnp.where(sel[:, j:j + 1] > 0.5, s, NEG_INF))
    s_new = _dot_nt(qb, kr.astype(BF16))
    ktok = lax.broadcasted_iota(jnp.int32, (nc, SUBLANES), 1)
    qtok = lax.broadcasted_iota(jnp.int32, (nc, SUBLANES), 0) // H_B
    s_new = jnp.where((ktok <= qtok) & (ktok < t), s_new, NEG_INF)
    _sample_softmax_pv(s_new, s_pages, v8, vp, o_ref, lane, rowh, t)


def _moba_sample(q, k, v, pool_k, pool_v, page_table, cos2, sin2):
    n_seq, t, w = q.shape
    n_pages = page_table.shape[1]
    assert (n_pages * PAGE_SIZE) % MOBA_BLOCK == 0
    row = pl.BlockSpec((None, t, w), lambda n, pt: (n, 0, 0))
    tab = pl.BlockSpec((t, LANES), lambda n, pt: (0, 0))
    page = (w, PAGE_SIZE)
    return pl.pallas_call(
        functools.partial(_moba_sample_kernel, n_pages=n_pages),
        grid_spec=pltpu.PrefetchScalarGridSpec(
            num_scalar_prefetch=1,
            grid=(n_seq,),
            in_specs=[row, row, row, tab, tab] + _paged_specs(n_pages, page) + _paged_specs(n_pages, page),
            out_specs=[row, row],
        ),
        out_shape=[jax.ShapeDtypeStruct((n_seq, t, w), F32)] * 2,
        compiler_params=_cparams("arbitrary"),
        name="moba_sample",
    )(page_table, q, k, v, jnp.concatenate([cos2, cos2], axis=1), jnp.concatenate([sin2, sin2], axis=1),
      *([_pool_pages(pool_k)] * n_pages), *([_pool_pages(pool_v)] * n_pages))


def _fox_sample_kernel(pt_ref, q_ref, k_ref, v_ref, df_ref, bf_ref, *rest, n_pages):
    del pt_ref
    kp, vp, lp = rest[:n_pages], rest[n_pages:2 * n_pages], rest[2 * n_pages:3 * n_pages]
    lf_ref, o_ref = rest[3 * n_pages:]
    t = q_ref.shape[0]
    nc = t * H_D
    k8, v8 = _pad_rows(k_ref[...], SUBLANES), _pad_rows(v_ref[...], SUBLANES)
    lane = lax.broadcasted_iota(jnp.int32, (nc, W_D), 1)
    rowh = lax.broadcasted_iota(jnp.int32, (nc, W_D), 0) % H_D
    qb = (_query_rows(q_ref[...], lane, rowh) * (HD_D ** -0.5)).astype(BF16)

    lf_row = _log_sigmoid(df_ref[...] + bf_ref[...])
    lf_ref[...] = lf_row
    rr = lax.broadcasted_iota(jnp.int32, (nc, nc), 0)
    cc = lax.broadcasted_iota(jnp.int32, (nc, nc), 1)
    same = (rr % H_D) == (cc % H_D)
    cn_col = jnp.sum(jnp.where(same & (cc // H_D <= rr // H_D), lf_row, 0.0), axis=1, keepdims=True)
    ktok = lax.broadcasted_iota(jnp.int32, (nc, SUBLANES), 1)
    qtok = lax.broadcasted_iota(jnp.int32, (nc, SUBLANES), 0) // H_D
    cn_keys = jnp.zeros((nc, SUBLANES), F32)
    for tp in range(t):
        col = jnp.sum(jnp.where(same & (cc // H_D <= tp), lf_row, 0.0), axis=1, keepdims=True)
        cn_keys = jnp.where(ktok == tp, col, cn_keys)

    lfa = jnp.concatenate([lp[p][...] for p in range(n_pages)], axis=0)
    nr = lfa.shape[0]
    lane_r = lax.broadcasted_iota(jnp.int32, (nr, LANES), 1)
    incl = lfa
    sh = 1
    while sh < LANES:
        incl = incl + jnp.where(lane_r < LANES - sh, pltpu.roll(incl, LANES - sh, 1), 0.0)
        sh *= 2
    r2 = lax.broadcasted_iota(jnp.int32, (nr, nr), 0)
    c2 = lax.broadcasted_iota(jnp.int32, (nr, nr), 1)
    later = ((c2 > r2) & ((c2 % H_D) == (r2 % H_D))).astype(BF16)
    suf = (incl - lfa) + _dot3(later, jnp.broadcast_to(incl[:, 0:1], (nr, LANES)))

    s_pages = []
    for p in range(n_pages):
        bias = jnp.concatenate([suf[H_D * p:H_D * (p + 1)]] * t, axis=0)
        s_pages.append(_dot(qb, kp[p][...].astype(BF16)) + bias + cn_col)
    s_new = _dot_nt(qb, k8.astype(BF16)) + (cn_col - cn_keys)
    s_new = jnp.where((ktok <= qtok) & (ktok < t), s_new, NEG_INF)
    _sample_softmax_pv(s_new, s_pages, v8, vp, o_ref, lane, rowh, t)


def _fox_sample(q, k, v, df, bf, pool_k, pool_v, pool_lf, page_table):
    n_seq, t, w = q.shape
    nc = t * H_D
    n_pages = page_table.shape[1]
    bft = jnp.tile(bf.astype(F32), t)
    row = pl.BlockSpec((None, t, w), lambda n, pt: (n, 0, 0))
    frow = pl.BlockSpec((None, 1, nc), lambda n, pt: (n, 0, 0))
    page = (w, PAGE_SIZE)
    lf, o = pl.pallas_call(
        functools.partial(_fox_sample_kernel, n_pages=n_pages),
        grid_spec=pltpu.PrefetchScalarGridSpec(
            num_scalar_prefetch=1,
            grid=(n_seq,),
            in_specs=[row, row, row, frow, pl.BlockSpec((1, nc), lambda n, pt: (0, 0))]
            + _paged_specs(n_pages, page) + _paged_specs(n_pages, page) + _paged_specs(n_pages, (H_D, PAGE_SIZE)),
            out_specs=[frow, row],
        ),
        out_shape=[jax.ShapeDtypeStruct((n_seq, 1, nc), F32), jax.ShapeDtypeStruct((n_seq, t, w), F32)],
        compiler_params=_cparams("arbitrary"),
        name="fox_sample",
    )(page_table, q, k, v, df.reshape(n_seq, 1, nc), bft.reshape(1, nc),
      *([_pool_pages(pool_k)] * n_pages), *([_pool_pages(pool_v)] * n_pages),
      *([jnp.transpose(pool_lf, (0, 2, 1))] * n_pages))
    return lf, o


def _shift_rows(x, d, fill):
    return jnp.concatenate([jnp.full((d, x.shape[1]), fill, x.dtype), x[:-d]], axis=0)


def _rglru_gates(xc, wa_ref, ba_ref, wx_ref, bx_ref, sp_ref, first_row_pos0):
    xb = xc.astype(BF16)
    r = _sigmoid(_dot(xb, wa_ref[...]) + ba_ref[...])
    i = _sigmoid(_dot(xb, wx_ref[...]) + bx_ref[...])
    log_a = -RG_C * r * sp_ref[...]
    a = jnp.exp(log_a)
    mult = jnp.sqrt(1.0 - jnp.exp(2.0 * log_a))
    if first_row_pos0 is not None:
        mult = jnp.where(first_row_pos0, 1.0, mult)
    return a, xc * i * mult


def _rglru_kernel(cx_ref, cg_ref, cw_ref, cb_ref, wa_ref, ba_ref, wx_ref, bx_ref, sp_ref,
                  o_ref, buf_ref, h_ref, tail_scr, h_scr):
    t = pl.program_id(1)
    tt = cx_ref.shape[0]

    @pl.when(t == 0)
    def _():
        tail_scr[...] = jnp.zeros_like(tail_scr)
        h_scr[...] = jnp.zeros_like(h_scr)

    cx = cx_ref[...]
    ext = jnp.concatenate([tail_scr[...], cx], axis=0)
    xc = cb_ref[...] + cx * cw_ref[CONV_W - 1:CONV_W, :]
    for d in range(1, CONV_W):
        xc = xc + ext[SUBLANES - d:SUBLANES - d + tt] * cw_ref[CONV_W - 1 - d:CONV_W - d, :]
    row = lax.broadcasted_iota(jnp.int32, (tt, W_C), 0)
    a, b = _rglru_gates(xc, wa_ref, ba_ref, wx_ref, bx_ref, sp_ref, (row == 0) & (t == 0))
    d = 1
    while d < tt:
        b = b + a * _shift_rows(b, d, 0.0)
        a = a * _shift_rows(a, d, 1.0)
        d *= 2
    h = a * h_scr[0:1, :] + b
    o_ref[...] = h * _gelu_tanh(cg_ref[...])
    h_scr[0:1, :] = h[tt - 1:tt, :]
    tail_scr[...] = cx[tt - SUBLANES:tt, :]

    @pl.when(t == pl.num_programs(1) - 1)
    def _():
        buf_ref[...] = cx[tt - (CONV_W - 1):tt, :]
        h_ref[...] = h[tt - 1:tt, :]


def _rglru_prompt(proj, batch, seq, cw, cb, wa, ba, wx, bx, sp, tt=256):
    nt = seq // tt
    vec = pl.BlockSpec((1, W_C), lambda b, t: (0, 0))
    mat = pl.BlockSpec((W_C, W_C), lambda b, t: (0, 0))
    o, buf, h = pl.pallas_call(
        _rglru_kernel,
        grid=(batch, nt),
        in_specs=[pl.BlockSpec((tt, W_C), lambda b, t: (b * nt + t, 0)),
                  pl.BlockSpec((tt, W_C), lambda b, t: (b * nt + t, 1)),
                  pl.BlockSpec((CONV_W, W_C), lambda b, t: (0, 0)), vec, mat, vec, mat, vec, vec],
        out_specs=[pl.BlockSpec((tt, W_C), lambda b, t: (b * nt + t, 0)),
                   pl.BlockSpec((None, CONV_W - 1, W_C), lambda b, t: (b, 0, 0)),
                   pl.BlockSpec((None, 1, W_C), lambda b, t: (b, 0, 0))],
        out_shape=[jax.ShapeDtypeStruct((batch * seq, W_C), F32),
                   jax.ShapeDtypeStruct((batch, CONV_W - 1, W_C), F32),
                   jax.ShapeDtypeStruct((batch, 1, W_C), F32)],
        scratch_shapes=[pltpu.VMEM((SUBLANES, W_C), F32), pltpu.VMEM((SUBLANES, W_C), F32)],
        compiler_params=_cparams("parallel", "arbitrary"),
        name="rglru",
    )(proj, proj, cw, cb, wa, ba, wx, bx, sp)
    return o, buf, h.reshape(batch, W_C)


def _rglru_sample_kernel(cx_ref, cg_ref, buf_ref, h0_ref, cw_ref, cb_ref, wa_ref, ba_ref, wx_ref, bx_ref, sp_ref,
                         o_ref, h_ref, *, pos0_is_zero):
    t, n, _ = cx_ref.shape
    xp = [buf_ref[j] for j in range(CONV_W - 1)] + [cx_ref[j] for j in range(t)]
    xcs = []
    for s in range(t):
        xc = cb_ref[...] + xp[s] * cw_ref[0:1, :]
        for j in range(1, CONV_W):
            xc = xc + xp[s + j] * cw_ref[j:j + 1, :]
        xcs.append(xc)
    xc = jnp.concatenate(xcs, axis=0)
    first = (lax.broadcasted_iota(jnp.int32, xc.shape, 0) < n) if pos0_is_zero else None
    a, b = _rglru_gates(xc, wa_ref, ba_ref, wx_ref, bx_ref, sp_ref, first)
    h = h0_ref[...]
    for s in range(t):
        h = a[s * n:(s + 1) * n] * h + b[s * n:(s + 1) * n]
        o_ref[s] = h * _gelu_tanh(cg_ref[s])
    h_ref[...] = h


def _rglru_sample(cx, cg, buf, h0, pos0_is_zero, cw, cb, wa, ba, wx, bx, sp):
    t, n, _ = cx.shape
    return pl.pallas_call(
        functools.partial(_rglru_sample_kernel, pos0_is_zero=pos0_is_zero),
        out_shape=[jax.ShapeDtypeStruct((t, n, W_C), F32), jax.ShapeDtypeStruct((n, W_C), F32)],
        compiler_params=pltpu.CompilerParams(vmem_limit_bytes=VMEM_LIMIT),
        name="rglru_sample",
    )(cx, cg, buf, h0, cw, cb, wa, ba, wx, bx, sp)


def _odd_weights(cw, cb, wa, ba, wx, bx, lam):
    def bd(w):
        return jax.scipy.linalg.block_diag(*[w[g] for g in range(NB_C)]).astype(BF16)

    def r(v):
        return v.reshape(1, W_C).astype(F32)

    return cw.astype(F32), r(cb), bd(wa), r(ba), bd(wx), r(bx), r(jax.nn.softplus(-lam.astype(F32)))


FOX_ONE_LANE = 3 * H_D


def _fox_selectors():
    selq = np.zeros((H_D // 2, LANES, LANES), np.float32)
    selk = np.zeros((H_D // 2, LANES, LANES), np.float32)
    for cc in range(H_D // 2):
        for h, base in ((2 * cc, HD_D), (2 * cc + 1, 0)):
            for piece in range(3):
                selq[cc, piece * H_D + h, base + piece] = 1.0
                selq[cc, FOX_ONE_LANE, base + 3 + piece] = 1.0
                selk[cc, FOX_ONE_LANE, base + piece] = 1.0
                selk[cc, piece * H_D + h, base + 3 + piece] = -1.0
    return jnp.asarray(selq, BF16), jnp.asarray(selk, BF16)


def _split_features(c, lane):
    hi, mid, lo = _split3(jnp.where(lane < H_D, c, 0.0))
    feat = (hi.astype(F32) + pltpu.roll(mid.astype(F32), H_D, 1) + pltpu.roll(lo.astype(F32), 2 * H_D, 1)
            + jnp.where(lane == FOX_ONE_LANE, 1.0, 0.0))
    return feat.astype(BF16)


def _fox_prep_kernel(q_ref, k_ref, v_ref, df_ref, bf_ref, selq_ref, selk_ref,
                     lf_ref, kt_ref, vl_ref, qa_ref, ka_ref, vt_ref, c_scr):
    t = pl.program_id(1)
    rows = q_ref.shape[0]

    @pl.when(t == 0)
    def _():
        c_scr[...] = jnp.zeros_like(c_scr)

    lane = lax.broadcasted_iota(jnp.int32, (rows, LANES), 1)
    low = lane < HD_D
    logf = jnp.where(lane < H_D, _log_sigmoid(df_ref[...] + bf_ref[...]), 0.0)
    lf_ref[...] = logf.T[:H_D]
    tri =(lax.broadcasted_iota(jnp.int32, (rows, rows), 0) >= lax.broadcasted_iota(jnp.int32, (rows, rows), 1)).astype(BF16)
    c = c_scr[0:1, :] + _dot3(tri, logf)
    c_scr[0:1, :] = c[rows - 1:rows, :]
    feat = _split_features(c, lane)
    scale = HD_D ** -0.5
    for cc in range(W_D // LANES):
        sl = slice(cc * LANES, (cc + 1) * LANES)
        eq = _dot(feat, selq_ref[cc])
        ek = _dot(feat, selk_ref[cc])
        qs, kc = q_ref[:, sl] * scale, k_ref[:, sl]
        qa_ref[2 * cc] = jnp.where(low, qs, eq).astype(BF16)
        qa_ref[2 * cc + 1] = jnp.where(low, eq, qs).astype(BF16)
        ka_ref[2 * cc] = jnp.where(low, kc, ek).astype(BF16)
        ka_ref[2 * cc + 1] = jnp.where(low, ek, kc).astype(BF16)
        _store_heads_t([kt_ref], cc, kc)
        _store_heads_t([vl_ref, vt_ref], cc, v_ref[:, sl])


def _fox_prep(proj, batch, seq, bf_row, selq, selk):
    rows = ATT_TILE
    nt = seq // rows

    def tok(cb):
        return pl.BlockSpec((rows, W_D), lambda b, t: (b * nt + t, cb))

    specs, shapes = _pair_specs(batch, H_D, seq)
    sel = pl.BlockSpec((H_D // 2, LANES, LANES), lambda b, t: (0, 0, 0))
    cb0 = COLS_O_Q // W_D
    return pl.pallas_call(
        _fox_prep_kernel,
        grid=(batch, nt),
        in_specs=[tok(cb0), tok(cb0 + 1), tok(cb0 + 2),
                  pl.BlockSpec((rows, LANES), lambda b, t: (b * nt + t, COLS_O_F // LANES)),
                  pl.BlockSpec((1, LANES), lambda b, t: (0, 0)), sel, sel],
        out_specs=[pl.BlockSpec((None, H_D, rows), lambda b, t: (b, 0, t))] + specs,
        out_shape=[jax.ShapeDtypeStruct((batch, H_D, seq), F32)] + shapes,
        scratch_shapes=[pltpu.VMEM((SUBLANES, LANES), F32)],
        compiler_params=_cparams("parallel", "arbitrary"),
        name="fox_prep",
    )(proj, proj, proj, proj, bf_row, selq, selk)


def _router_kernel(x_ref, g_ref, r0_ref, r1_ref, r2_ref, o_ref, lg_ref):
    x = x_ref[...]
    y = x * lax.rsqrt(jnp.mean(x * x, axis=-1, keepdims=True) + EPS) * g_ref[...]
    o_ref[...] = y.astype(o_ref.dtype)
    y0, y1, y2 = _split3(y)
    r0, r1, r2 = r0_ref[...], r1_ref[...], r2_ref[...]
    lg_ref[...] = (_dot(y0, r0) + (_dot(y0, r1) + _dot(y1, r0)) + (_dot(y0, r2) + _dot(y1, r1) + _dot(y2, r0)))


def _rmsnorm_router(x, g, router, tm=512):
    m, d = x.shape
    rp = jnp.zeros((d, LANES), F32).at[:, :router.shape[1]].set(router.astype(F32))
    r0 = rp.astype(BF16)
    r1 = (rp - r0.astype(F32)).astype(BF16)
    r2 = (rp - r0.astype(F32) - r1.astype(F32)).astype(BF16)
    rspec = pl.BlockSpec((d, LANES), lambda i: (0, 0))
    return pl.pallas_call(
        _router_kernel,
        grid=(m // tm,),
        in_specs=[pl.BlockSpec((tm, d), lambda i: (i, 0)), pl.BlockSpec((1, d), lambda i: (0, 0)), rspec, rspec, rspec],
        out_specs=[pl.BlockSpec((tm, d), lambda i: (i, 0)), pl.BlockSpec((tm, LANES), lambda i: (i, 0))],
        out_shape=[jax.ShapeDtypeStruct((m, d), BF16), jax.ShapeDtypeStruct((m, LANES), F32)],
        compiler_params=_cparams("parallel"),
        name="rmsnorm_router",
    )(x, g.reshape(1, d), r0, r1, r2)


def _combine_norm_kernel(y_ref, ya_ref, yb_ref, gt_ref, g_ref, o_ref):
    gt = gt_ref[...]
    x = y_ref[...] + (gt[:, 0:1] * ya_ref[...] + gt[:, 1:2] * yb_ref[...])
    o_ref[...] = x * lax.rsqrt(jnp.mean(x * x, axis=-1, keepdims=True) + EPS) * g_ref[...]


def _combine_norm(y, ya, yb, gates, g, tm=512):
    m, d = y.shape
    row = pl.BlockSpec((tm, d), lambda i: (i, 0))
    return pl.pallas_call(
        _combine_norm_kernel,
        grid=(m // tm,),
        in_specs=[row, row, row, pl.BlockSpec((tm, TOP_K), lambda i: (i, 0)), pl.BlockSpec((1, d), lambda i: (0, 0))],
        out_specs=row,
        out_shape=jax.ShapeDtypeStruct((m, d), F32),
        compiler_params=_cparams("parallel"),
        name="combine_norm",
    )(y, ya, yb, gates, g.reshape(1, d))


def _moe_dispatch(logits, tm):
    m, e = logits.shape
    idx = lax.broadcasted_iota(jnp.int32, (m, e), 1)
    m1 = jnp.max(logits, axis=1, keepdims=True)
    i1 = jnp.min(jnp.where(logits == m1, idx, e), axis=1, keepdims=True)
    rest = jnp.where(idx == i1, -jnp.inf, logits)
    m2 = jnp.max(rest, axis=1, keepdims=True)
    i2 = jnp.min(jnp.where(rest == m2, idx, e), axis=1, keepdims=True)
    ex = jnp.exp(m2 - m1)
    gates = jnp.concatenate([1.0 / (1.0 + ex), ex / (1.0 + ex)], axis=1)
    flat_e = jnp.concatenate([i1, i2], axis=1).reshape(m * TOP_K)
    onehot = (flat_e[:, None] == jnp.arange(e, dtype=jnp.int32)[None, :]).astype(jnp.int32)
    csum = jnp.cumsum(onehot, axis=0)
    counts = csum[-1]
    padded = -(-counts // tm) * tm
    ends = jnp.cumsum(padded)
    dest = jnp.sum(onehot * ((ends - padded)[None, :] + csum - 1), axis=1)
    n_tiles = (m * TOP_K) // tm + e
    src_token = jnp.zeros((n_tiles * tm,), jnp.int32).at[dest].set(jnp.arange(m * TOP_K, dtype=jnp.int32) // TOP_K)
    tile_start = jnp.arange(n_tiles, dtype=jnp.int32) * tm
    tile_expert = jnp.minimum(jnp.sum((ends[None, :] <= tile_start[:, None]).astype(jnp.int32), axis=1), e - 1)
    return gates, src_token, dest.reshape(m, TOP_K), tile_expert


def kernel(x_prompt, x_sample, cache_k_e, cache_v_e, state_s_e, state_conv_o, state_h_o, cache_k_o, cache_v_o,
           cache_logf_o, page_table, w_in_e, lb_logits, gnorm_a, w_out_e, ffn_w1, ffn_w3, ffn_w2, w_in_o, conv_w,
           conv_b, rg_wa, rg_ba, rg_wx, rg_bx, rg_lambda, fox_bf, w_out_o, moe_router, moe_w1, moe_w3, moe_w2,
           norm_mix, norm_ffn, norm_final):
    batch, seq, d = x_prompt.shape
    n_seq, ts, _ = x_sample.shape
    mp, ms = batch * seq, n_seq * ts
    m = mp + ms
    n_pages = page_table.shape[1]
    past = n_pages * PAGE_SIZE
    chunk_a = 64
    assert d == D_MODEL and ts <= SUBLANES and seq % chunk_a == 0 and m % chunk_a == 0

    x = jnp.concatenate([x_prompt.reshape(mp, d), x_sample.reshape(ms, d)], axis=0)
    pos_p = jnp.arange(seq, dtype=jnp.int32)
    pos_s = past + jnp.arange(ts, dtype=jnp.int32)

    def seq_rows(a):
        return a.reshape(n_seq, ts, a.shape[1])

    hn = _rmsnorm(x, norm_mix[0], BF16)
    proj = _matmul([hn], w_in_e[0].astype(BF16))
    proj_s = proj[mp:]
    lb = jnp.cumsum(jax.nn.softmax(lb_logits.astype(F32), axis=0), axis=0)[0]
    oa_p, s_p = _hgrn2(proj.reshape(m // chunk_a, chunk_a, proj.shape[1]), batch, seq // chunk_a,
                       jnp.zeros((batch, H_A, DK_A, DV_A), F32), lb, gnorm_a[0])
    oa_s, s_s = _hgrn2(proj_s.reshape(n_seq, ts, proj.shape[1]), n_seq, 1, state_s_e[0], lb, gnorm_a[0])
    cos_p, sin_p = _rope_tables(pos_p)
    cos_s, sin_s = _rope_tables(pos_s)
    ke_p, ve_p, qa, ka, vt = _moba_prep(proj, batch, seq, cos_p, sin_p)
    ob_p = _flash(qa, ka, vt)
    ve_s = proj_s[:, COLS_E_Q + 2 * W_B:COLS_E_Q + 3 * W_B]
    ke_s, ob_s = _moba_sample(seq_rows(proj_s[:, COLS_E_Q:COLS_E_Q + W_B]),
                              seq_rows(proj_s[:, COLS_E_Q + W_B:COLS_E_Q + 2 * W_B]), seq_rows(ve_s),
                              cache_k_e[0], cache_v_e[0], page_table, cos_s, sin_s)
    y = _matmul([jnp.concatenate([oa_p, oa_s], axis=0), jnp.concatenate([ob_p, ob_s.reshape(ms, W_B)], axis=0)],
                w_out_e[0].astype(BF16), res=x)
    hn = _rmsnorm(y, norm_ffn[0], BF16)
    y = _ffn(hn, ffn_w1.astype(BF16), ffn_w3.astype(BF16), ffn_w2.astype(BF16), res=y, tf=ffn_w1.shape[2] // 2)

    hn = _rmsnorm(y, norm_mix[1], BF16)
    cols_o = w_in_o.shape[2]
    cols_pad = -(-cols_o // LANES) * LANES
    w_in_o_p = jnp.zeros((d, cols_pad), BF16).at[:, :cols_o].set(w_in_o[0].astype(BF16))
    proj_o = _matmul([hn], w_in_o_p)
    proj_os = proj_o[mp:]
    ow = _odd_weights(conv_w[0], conv_b[0], rg_wa[0], rg_ba[0], rg_wx[0], rg_bx[0], rg_lambda[0])
    oc_p, buf_p, h_p = _rglru_prompt(proj_o, batch, seq, *ow)
    ps = proj_os.reshape(n_seq, ts, cols_pad)
    cx_s = ps[..., :W_C]
    oc_s, h_s = _rglru_sample(cx_s.swapaxes(0, 1), ps[..., W_C:2 * W_C].swapaxes(0, 1), state_conv_o[0].swapaxes(0, 1),
                              state_h_o[0], past == 0, *ow)
    oc_s = oc_s.swapaxes(0, 1).reshape(ms, W_C)
    buf_s = jnp.concatenate([state_conv_o[0].astype(F32), cx_s], axis=1)[:, ts:]
    bf_row = jnp.zeros((1, LANES), F32).at[0, :H_D].set(fox_bf[0].astype(F32))
    selq, selk = _fox_selectors()
    lf_p, ko_p, vo_p, qa, ka, vt = _fox_prep(proj_o, batch, seq, bf_row, selq, selk)
    od_p = _flash(qa, ka, vt)
    ko_s = proj_os[:, COLS_O_Q + W_D:COLS_O_Q + 2 * W_D]
    vo_s = proj_os[:, COLS_O_Q + 2 * W_D:COLS_O_Q + 3 * W_D]
    lf_s, od_s = _fox_sample(seq_rows(proj_os[:, COLS_O_Q:COLS_O_Q + W_D]), seq_rows(ko_s), seq_rows(vo_s),
                             proj_os[:, COLS_O_F:COLS_O_F + H_D].reshape(n_seq, ts * H_D), fox_bf[0],
                             cache_k_o[0], cache_v_o[0], cache_logf_o[0], page_table)
    y = _matmul([jnp.concatenate([oc_p, oc_s], axis=0), jnp.concatenate([od_p, od_s.reshape(ms, W_D)], axis=0)],
                w_out_o[0].astype(BF16), res=y)
    tm_e = 512
    hn, logits = _rmsnorm_router(y, norm_ffn[1], moe_router[0])
    gates, src_token, pair_pos, tile_expert = _moe_dispatch(logits[:, :N_EXPERTS], tm_e)
    y_e = _ffn(jnp.take(hn, src_token, axis=0), moe_w1[0], moe_w3[0], moe_w2[0], tile_expert=tile_expert, tm=tm_e, tf=512)
    out = _combine_norm(y, jnp.take(y_e, pair_pos[:, 0], axis=0), jnp.take(y_e, pair_pos[:, 1], axis=0), gates, norm_final)

    def heads(a, h):
        return a.reshape(1, n_seq, ts, h, a.shape[-1] // h)

    def heads_t(a):
        return jnp.transpose(a, (0, 3, 1, 2))[None]

    return (out[:mp].reshape(batch, seq, d), out[mp:].reshape(n_seq, ts, d),
            s_p[None], s_s[None],
            heads_t(ke_p), heads_t(ve_p), heads(ke_s, H_B), heads(ve_s, H_B),
            buf_p[None], buf_s[None], h_p[None], h_s[None],
            heads_t(ko_p), heads_t(vo_p), jnp.transpose(lf_p, (0, 2, 1))[None],
            heads(ko_s, H_D), heads(vo_s, H_D), lf_s.reshape(1, n_seq, ts, H_D))
```

```python
import functools
import math

import numpy as np
import jax
import jax.numpy as jnp
from jax import lax
from jax.experimental import pallas as pl
from jax.experimental.pallas import tpu as pltpu

F32 = jnp.float32
BF16 = jnp.bfloat16

D_MODEL = 1024
PAGE_SIZE = 128
H_A, DK_A, DV_A = 4, 128, 128
F_A, W_A = H_A * DK_A, H_A * DV_A
H_B, HD_B = 8, 64
W_B = H_B * HD_B
MOBA_BLOCK, MOBA_TOPK = 256, 3
W_C, NB_C, CONV_W, RG_C = 512, 8, 4, 8.0
BW_C = W_C // NB_C
H_D, HD_D = 8, 64
W_D = H_D * HD_D
N_EXPERTS, TOP_K = 8, 2
ROPE_THETA = 10000.0
EPS = 1e-6
NEG_INF = -1e30
LOG2E = math.log2(math.e)

COLS_E_Q = 2 * F_A + 2 * W_A
COLS_O_Q = 2 * W_C
COLS_O_F = 2 * W_C + 3 * W_D

LANES = 128
SUBLANES = 8
VMEM_LIMIT = 56 * 1024 * 1024
ATT_TILE = 256


def _cparams(*sem):
    return pltpu.CompilerParams(dimension_semantics=sem, vmem_limit_bytes=VMEM_LIMIT)


def _split3(x):
    hi = x.astype(BF16)
    r1 = x - hi.astype(F32)
    mid = r1.astype(BF16)
    lo = (r1 - mid.astype(F32)).astype(BF16)
    return hi, mid, lo


def _dot(a, b):
    return jnp.dot(a, b, preferred_element_type=F32)


def _dot_nt(a, b):
    return lax.dot_general(a, b, (((1,), (1,)), ((), ())), preferred_element_type=F32)


def _dot_tn(a, b):
    return lax.dot_general(a, b, (((0,), (0,)), ((), ())), preferred_element_type=F32)


def _dot3(a, b):
    b0, b1, b2 = _split3(b)
    return _dot(a, b0) + _dot(a, b1) + _dot(a, b2)


def _sigmoid(x):
    return 1.0 / (1.0 + jnp.exp(-x))


def _log_sigmoid(x):
    return jnp.minimum(x, 0.0) - jnp.log(1.0 + jnp.exp(-jnp.abs(x)))


def _gelu_tanh(x):
    return 0.5 * x * (1.0 + jnp.tanh(math.sqrt(2.0 / math.pi) * (x + 0.044715 * (x * x * x))))


def _rms(x, g):
    return x * lax.rsqrt(jnp.mean(x * x, axis=-1, keepdims=True) + EPS) * g


def _matmul_kernel(*refs, parts, n_prompt, has_norm, has_res):
    i = pl.program_id(0)
    refs = list(refs)
    g_ref = refs.pop(0) if has_norm else None
    xs = []
    for split in parts:
        if split:
            p_ref, s_ref = refs.pop(0), refs.pop(0)
            xs.append(jnp.where(i < n_prompt, p_ref[...], s_ref[...]))
        else:
            xs.append(refs.pop(0)[...])
    w_ref = refs.pop(0)
    res_ref = refs.pop(0) if has_res else None
    o_ref = refs.pop(0)
    acc = None
    k0 = 0
    for x in xs:
        if has_norm:
            x = _rms(x, g_ref[...])
        kk = x.shape[1]
        part = _dot(x.astype(BF16), w_ref[k0:k0 + kk, :])
        acc = part if acc is None else acc + part
        k0 += kk
    if has_res:
        acc = acc + res_ref[...]
    o_ref[...] = acc


def _matmul(xs, w, res=None, norm_g=None, tm=512):
    kt, n = w.shape
    parts = tuple(isinstance(x, tuple) for x in xs)
    m = sum(a.shape[0] for a in xs[0]) if parts[0] else xs[0].shape[0]
    n_prompt = 0
    in_specs, args = [], []
    if norm_g is not None:
        assert len(xs) == 1
        in_specs.append(pl.BlockSpec((1, kt), lambda i: (0, 0)))
        args.append(norm_g.reshape(1, kt).astype(F32))
    for x, split in zip(xs, parts):
        if split:
            xp, xsm = x
            assert xp.shape[0] % tm == 0 and xsm.shape[0] % tm == 0
            n_prompt = xp.shape[0] // tm
            in_specs.append(pl.BlockSpec((tm, xp.shape[1]), lambda i, n_p=n_prompt: (jnp.minimum(i, n_p - 1), 0)))
            in_specs.append(pl.BlockSpec((tm, xp.shape[1]), lambda i, n_p=n_prompt: (jnp.maximum(i - n_p, 0), 0)))
            args += [xp, xsm]
        else:
            in_specs.append(pl.BlockSpec((tm, x.shape[1]), lambda i: (i, 0)))
            args.append(x)
    in_specs.append(pl.BlockSpec((kt, n), lambda i: (0, 0)))
    args.append(w)
    if res is not None:
        in_specs.append(pl.BlockSpec((tm, n), lambda i: (i, 0)))
        args.append(res)
    return pl.pallas_call(
        functools.partial(_matmul_kernel, parts=parts, n_prompt=n_prompt, has_norm=norm_g is not None,
                          has_res=res is not None),
        grid=(m // tm,),
        in_specs=in_specs,
        out_specs=pl.BlockSpec((tm, n), lambda i: (i, 0)),
        out_shape=jax.ShapeDtypeStruct((m, n), F32),
        compiler_params=_cparams("parallel"),
        name="matmul",
    )(*args)


def _ffn_kernel(te_ref, nu_ref, x_ref, w1_ref, w3_ref, w2_ref, *rest, has_norm, has_res):
    del te_ref
    rest = list(rest)
    g_ref = rest.pop(0) if has_norm else None
    res_ref = x_ref if has_res else None
    o_ref, acc_ref, xb_ref = rest
    i, f = pl.program_id(0), pl.program_id(1)

    @pl.when(i < nu_ref[0])
    def _():
        @pl.when(f == 0)
        def _():
            x = x_ref[...]
            if has_norm:
                x = _rms(x, g_ref[...])
            xb_ref[...] = x.astype(BF16)
            acc_ref[...] = jnp.zeros_like(acc_ref)

        x = xb_ref[...]
        a = _dot(x, w1_ref[...].astype(BF16))
        b = _dot(x, w3_ref[...].astype(BF16))
        g = (a * _sigmoid(a) * b).astype(BF16)
        acc_ref[...] += _dot(g, w2_ref[...].astype(BF16))

        @pl.when(f == pl.num_programs(1) - 1)
        def _():
            out = acc_ref[...]
            if has_res:
                out = out + res_ref[...]
            o_ref[...] = out


def _ffn(x, w1, w3, w2, residual=False, norm_g=None, tile_expert=None, n_used=None, tm=512, tf=256):
    m, d = x.shape
    f = w1.shape[-1]
    nf = f // tf
    if tile_expert is None:
        tile_expert = jnp.zeros((m // tm,), jnp.int32)
    if n_used is None:
        n_used = jnp.full((1,), m // tm, jnp.int32)

    def fblk(i, j, nu):
        return jnp.where(i < nu[0], j, nf - 1)

    row = pl.BlockSpec((tm, d), lambda i, j, te, nu: (i, 0))
    in_specs = [
        row,
        pl.BlockSpec((None, d, tf), lambda i, j, te, nu: (te[i], 0, fblk(i, j, nu))),
        pl.BlockSpec((None, d, tf), lambda i, j, te, nu: (te[i], 0, fblk(i, j, nu))),
        pl.BlockSpec((None, tf, d), lambda i, j, te, nu: (te[i], fblk(i, j, nu), 0)),
    ]
    args = [x, w1, w3, w2]
    if norm_g is not None:
        in_specs.append(pl.BlockSpec((1, d), lambda i, j, te, nu: (0, 0)))
        args.append(norm_g.reshape(1, d).astype(F32))
    return pl.pallas_call(
        functools.partial(_ffn_kernel, has_norm=norm_g is not None, has_res=residual),
        grid_spec=pltpu.PrefetchScalarGridSpec(
            num_scalar_prefetch=2,
            grid=(m // tm, nf),
            in_specs=in_specs,
            out_specs=row,
            scratch_shapes=[pltpu.VMEM((tm, d), F32), pltpu.VMEM((tm, d), BF16)],
        ),
        out_shape=jax.ShapeDtypeStruct((m, d), F32),
        compiler_params=_cparams("arbitrary", "arbitrary"),
        name="ffn",
    )(tile_expert, n_used, *args)


def _hgrn2_head(q, k, v, logf, st, c_sub):
    c = q.shape[0]
    ns = c // c_sub
    row = lax.broadcasted_iota(jnp.int32, (c, c), 0)
    col = lax.broadcasted_iota(jnp.int32, (c, c), 1)
    tri = (row >= col).astype(BF16)
    g = _dot3(tri, logf)
    vb = v.astype(BF16)
    o = _dot_nt((q * jnp.exp(g)).astype(BF16), st.astype(BF16))

    lane_c = lax.broadcasted_iota(jnp.int32, (c_sub, c), 1)
    slabs = []
    for i in range(ns):
        r0 = i * c_sub
        qi, ki, gi = q[r0:r0 + c_sub], k[r0:r0 + c_sub], g[r0:r0 + c_sub]
        slab = jnp.zeros((c_sub, c), F32)
        for s in range(c_sub):
            w = jnp.exp(jnp.minimum(gi - gi[s:s + 1, :], 0.0))
            colv = jnp.sum(qi * w * ki[s:s + 1, :], axis=-1, keepdims=True)
            slab = jnp.where(lane_c == r0 + s, colv, slab)
        slabs.append(slab)
    scores = slabs[0] if ns == 1 else jnp.concatenate(slabs, axis=0)
    scores = jnp.where((row >= col) & (row // c_sub == col // c_sub), scores, 0.0)

    w = c // 2
    while w >= c_sub:
        q_parts, k_parts = [], []
        for b in range(c // w):
            r0 = b * w
            if b % 2 == 1:
                q_parts.append(q[r0:r0 + w] * jnp.exp(g[r0:r0 + w] - g[r0 - 1:r0, :]))
                k_parts.append(jnp.zeros((w, q.shape[1]), F32))
            else:
                q_parts.append(jnp.zeros((w, q.shape[1]), F32))
                k_parts.append(k[r0:r0 + w] * jnp.exp(g[r0 + w - 1:r0 + w, :] - g[r0:r0 + w]))
        lvl = _dot_nt(jnp.concatenate(q_parts, axis=0).astype(BF16), jnp.concatenate(k_parts, axis=0).astype(BF16))
        scores = scores + jnp.where(((row // w) % 2 == 1) & (col // w == row // w - 1), lvl, 0.0)
        w //= 2
    o = o + _dot(scores.astype(BF16), vb)
    gl = g[c - 1:c, :]
    ke = (k * jnp.exp(gl - g)).astype(BF16)
    st_new = jnp.exp(gl) * st + _dot_tn(vb, ke)
    return o, st_new


def _hgrn2_kernel(aq_ref, af_ref, ai_ref, ag_ref, lb_ref, gn_ref, s0_ref, o_ref, s_ref, st_scr, *, c_sub, ct):
    t = pl.program_id(1)

    @pl.when(t == 0)
    def _():
        for h in range(H_A):
            st_scr[h] = s0_ref[h].T

    def padded(ref):
        x = ref[...]
        if ct < SUBLANES:
            x = jnp.concatenate([x, jnp.zeros((SUBLANES - ct, x.shape[1]), F32)], axis=0)
        return x

    aq, zf, vi = padded(aq_ref), padded(af_ref), padded(ai_ref)
    c = aq.shape[0]
    lb = lb_ref[...]
    real = lax.broadcasted_iota(jnp.int32, (c, F_A), 0) < ct
    sig = _sigmoid(zf)
    q = aq * _sigmoid(aq)
    logf = jnp.where(real, jnp.log(lb + (1.0 - lb) * sig), 0.0)
    k = jnp.where(real, (1.0 - lb) * _sigmoid(-zf), 0.0)
    outs = []
    for h in range(H_A):
        sl = slice(h * DK_A, (h + 1) * DK_A)
        o_h, st_new = _hgrn2_head(q[:, sl], k[:, sl], vi[:, sl], logf[:, sl], st_scr[h], c_sub)
        st_scr[h] = st_new
        outs.append(o_h)
    o = jnp.concatenate(outs, axis=1)[:ct]
    o = o * lax.rsqrt(jnp.mean(o * o, axis=-1, keepdims=True) + EPS) * gn_ref[...]
    o_ref[...] = o * _sigmoid(ag_ref[...])

    @pl.when(t == pl.num_programs(1) - 1)
    def _():
        for h in range(H_A):
            s_ref[h] = st_scr[h].T


def _hgrn2(x3, n_seq, n_chunks, s0, lb, gnorm):
    ct = x3.shape[1]
    c_sub = SUBLANES

    def tok_spec(cb):
        return pl.BlockSpec((None, ct, F_A), lambda n, t: (n * n_chunks + t, 0, cb))

    o, s = pl.pallas_call(
        functools.partial(_hgrn2_kernel, c_sub=c_sub, ct=ct),
        grid=(n_seq, n_chunks),
        in_specs=[tok_spec(0), tok_spec(1), tok_spec(2), tok_spec(3),
                  pl.BlockSpec((1, F_A), lambda n, t: (0, 0)),
                  pl.BlockSpec((1, W_A), lambda n, t: (0, 0)),
                  pl.BlockSpec((None, H_A, DK_A, DV_A), lambda n, t: (n, 0, 0, 0))],
        out_specs=[pl.BlockSpec((None, ct, W_A), lambda n, t: (n * n_chunks + t, 0, 0)),
                   pl.BlockSpec((None, H_A, DK_A, DV_A), lambda n, t: (n, 0, 0, 0))],
        out_shape=[jax.ShapeDtypeStruct((n_seq * n_chunks, ct, W_A), F32),
                   jax.ShapeDtypeStruct((n_seq, H_A, DK_A, DV_A), F32)],
        scratch_shapes=[pltpu.VMEM((H_A, DV_A, DK_A), F32)],
        compiler_params=_cparams("parallel", "arbitrary"),
        name="hgrn2",
    )(x3, x3, x3, x3, lb.reshape(1, F_A), gnorm.reshape(1, W_A), s0)
    return o.reshape(n_seq * n_chunks * ct, W_A), s


def _rope_tables(pos):
    half = HD_B // 2
    inv = ROPE_THETA ** (-jnp.arange(half, dtype=F32) / half)
    ang = pos.astype(F32)[:, None] * inv[None, :]
    cos, sin = jnp.cos(ang), jnp.sin(ang)
    return jnp.concatenate([cos, cos], axis=1), jnp.concatenate([-sin, sin], axis=1)


def _rope(x, cos2, sin2, lane):
    swapped = jnp.where((lane & (HD_B - 1)) >= HD_B // 2, pltpu.roll(x, HD_B // 2, 1), pltpu.roll(x, LANES - HD_B // 2, 1))
    return x * cos2 + swapped * sin2


def _dot_nt_f32(a, b):
    a0, a1, a2 = _split3(a)
    b0, b1, b2 = _split3(b)
    return (_dot_nt(a0, b0) + (_dot_nt(a0, b1) + _dot_nt(a1, b0))
            + (_dot_nt(a0, b2) + _dot_nt(a1, b1) + _dot_nt(a2, b0)))


def _top_select(g, index, axis, n_pick):
    sel = jnp.zeros(g.shape, F32)
    big = jnp.int32(1 << 20)
    for _ in range(n_pick):
        m = jnp.max(g, axis=axis, keepdims=True)
        first = jnp.min(jnp.where(g == m, index, big), axis=axis, keepdims=True)
        pick = jnp.where((index == first) & (m > 0.5 * NEG_INF), 1.0, 0.0)
        sel = sel + pick
        g = jnp.where(pick > 0.5, NEG_INF, g)
    return sel


def _store_heads_t(refs, c, x):
    xt = x.T
    for ref in refs:
        ref[2 * c] = xt[:HD_B].astype(ref.dtype)
        ref[2 * c + 1] = xt[HD_B:].astype(ref.dtype)


def _moba_prep_kernel(q_ref, k_ref, v_ref, cos_ref, sin_ref, kt_ref, vl_ref, qa_ref, ka_ref, vt_ref, mrow_scr):
    t = pl.program_id(1)
    rows = q_ref.shape[0]

    @pl.when(t == 0)
    def _():
        mrow_scr[...] = jnp.zeros_like(mrow_scr)

    lane = lax.broadcasted_iota(jnp.int32, (rows, LANES), 1)
    lane1 = lax.broadcasted_iota(jnp.int32, (1, LANES), 1)
    low = lane < HD_B
    blk = lane & (HD_B - 1)
    nbm = HD_B // 2
    brow = lax.broadcasted_iota(jnp.int32, (nbm, rows), 0)
    cos2, sin2 = cos_ref[...], sin_ref[...]
    scale = HD_B ** -0.5 * LOG2E
    for c in range(W_B // LANES):
        sl = slice(c * LANES, (c + 1) * LANES)
        qr = _rope(q_ref[:, sl], cos2, sin2, lane)
        kr = _rope(k_ref[:, sl], cos2, sin2, lane)
        _store_heads_t([kt_ref], c, kr)
        gate_t = _dot_nt_f32(mrow_scr[c], qr)
        halves = []
        for r0 in (0, HD_B):
            g = jnp.where(brow < t, gate_t[r0:r0 + nbm], NEG_INF)
            keep = (_top_select(g, brow, 0, MOBA_TOPK) > 0.5) | (brow == t)
            halves += [jnp.where(keep, 0.0, NEG_INF), jnp.full((HD_B - nbm, rows), NEG_INF, F32)]
        msel = jnp.concatenate(halves, axis=0).T
        own = jnp.where(blk == t, 1.0, 0.0)
        qs = qr * scale
        qa_ref[2 * c] = jnp.where(low, qs, msel).astype(BF16)
        qa_ref[2 * c + 1] = jnp.where(low, msel, qs).astype(BF16)
        ka_ref[2 * c] = jnp.where(low, kr, own).astype(BF16)
        ka_ref[2 * c + 1] = jnp.where(low, own, kr).astype(BF16)
        _store_heads_t([vl_ref, vt_ref], c, v_ref[:, sl])
        mean = jnp.mean(kr, axis=0, keepdims=True)
        mrow_scr[c, pl.ds(HD_B + t, 1), :] = jnp.where(lane1 < HD_B, mean, 0.0)
        mrow_scr[c, pl.ds(t, 1), :] = jnp.where(lane1 < HD_B, 0.0, mean)


def _pair_specs(batch, heads, seq):
    nt = seq // ATT_TILE
    pair = pl.BlockSpec((None, heads, ATT_TILE, LANES), lambda b, t: (b, 0, t, 0))
    pair_shape = jax.ShapeDtypeStruct((batch, heads, seq, LANES), BF16)
    vt = pl.BlockSpec((None, heads, None, LANES // 2, ATT_TILE), lambda b, t: (b, 0, t, 0, 0))
    vt_shape = jax.ShapeDtypeStruct((batch, heads, nt, LANES // 2, ATT_TILE), BF16)
    leaf = pl.BlockSpec((None, heads, LANES // 2, ATT_TILE), lambda b, t: (b, 0, 0, t))
    leaf_shape = jax.ShapeDtypeStruct((batch, heads, LANES // 2, seq), F32)
    return [leaf, leaf, pair, pair, vt], [leaf_shape, leaf_shape, pair_shape, pair_shape, vt_shape]


def _moba_prep(proj, batch, seq, cos2, sin2):
    nb = seq // MOBA_BLOCK
    assert seq % MOBA_BLOCK == 0 and nb <= HD_B // 2 and MOBA_BLOCK == ATT_TILE
    rows = MOBA_BLOCK

    def tok(cb):
        return pl.BlockSpec((rows, W_B), lambda b, t: (b * nb + t, cb))

    specs, shapes = _pair_specs(batch, H_B, seq)
    cb0 = COLS_E_Q // W_B
    return pl.pallas_call(
        _moba_prep_kernel,
        grid=(batch, nb),
        in_specs=[tok(cb0), tok(cb0 + 1), tok(cb0 + 2),
                  pl.BlockSpec((rows, LANES), lambda b, t: (t, 0)),
                  pl.BlockSpec((rows, LANES), lambda b, t: (t, 0))],
        out_specs=specs,
        out_shape=shapes,
        scratch_shapes=[pltpu.VMEM((W_B // LANES, LANES, LANES), F32)],
        compiler_params=_cparams("parallel", "arbitrary"),
        name="moba_prep",
    )(proj, proj, proj, jnp.concatenate([cos2, cos2], axis=1), jnp.concatenate([sin2, sin2], axis=1))


def _flash_kernel(qa_ref, ka_ref, vt_ref, o_ref, s0_scr, s1_scr, acc_scr):
    i = pl.program_id(2)
    tq = qa_ref.shape[1]
    tv = vt_ref.shape[3]
    sub = tq // tv
    qs = [qa_ref[0], qa_ref[1]]

    def produce(j, s_ref):
        for hh in range(2):
            k = ka_ref[hh, pl.ds(pl.multiple_of(j * tq, tq), tq), :]
            s_ref[hh] = _dot_nt(k, qs[hh])

    def fold(x, op):
        return op(x.reshape(tq // SUBLANES, SUBLANES, tq), axis=0)

    def consume(j, s_ref, carry, diag=False):
        new = []
        for hh in range(2):
            m, lacc = carry[hh]
            if diag:
                krow = lax.broadcasted_iota(jnp.int32, (tq, tq), 0)
                qcol = lax.broadcasted_iota(jnp.int32, (tq, tq), 1)
                s_ref[hh] = jnp.where(krow <= qcol, s_ref[hh], NEG_INF)
            m_new = jnp.maximum(m, jnp.max(fold(s_ref[hh], jnp.max), axis=0, keepdims=True))
            alpha = jnp.exp2(m - m_new)
            p = jnp.exp2(s_ref[hh] - m_new)
            lacc = alpha * lacc + fold(p, jnp.sum)
            pb = p.astype(BF16)
            pv = _dot(vt_ref[hh, j * sub], pb[:tv])
            for u in range(1, sub):
                pv = pv + _dot(vt_ref[hh, j * sub + u], pb[u * tv:(u + 1) * tv])
            acc_scr[hh] = alpha * acc_scr[hh] + pv
            new.append((m_new, lacc))
        return tuple(new)

    def pair(u, carry):
        j = 2 * u
        produce(j + 1, s1_scr)
        carry = consume(j, s0_scr, carry)
        produce(j + 2, s0_scr)
        return consume(j + 1, s1_scr, carry)

    def odd_tail(carry):
        produce(i, s1_scr)
        carry = consume(i - 1, s0_scr, carry)
        return consume(i, s1_scr, carry, diag=True)

    def even_tail(carry):
        return consume(i, s0_scr, carry, diag=True)

    acc_scr[...] = jnp.zeros_like(acc_scr)
    produce(0, s0_scr)
    init = tuple((jnp.full((1, tq), -jnp.inf, F32), jnp.zeros((SUBLANES, tq), F32)) for _ in range(2))
    carry = lax.fori_loop(0, i // 2, pair, init)
    carry = lax.cond(i % 2 == 1, odd_tail, even_tail, carry)
    ot = jnp.concatenate([acc_scr[hh] / jnp.sum(carry[hh][1], axis=0, keepdims=True) for hh in range(2)], axis=0)
    o_ref[...] = ot.T


def _flash(qa, ka, vt, tq=512):
    batch, heads, seq, _ = qa.shape
    tq = min(tq, seq)
    nq = seq // tq
    nv, hd, tv = vt.shape[2:]
    return pl.pallas_call(
        _flash_kernel,
        grid=(batch, heads // 2, nq),
        in_specs=[pl.BlockSpec((None, 2, tq, LANES), lambda b, c, i: (b, c, i, 0)),
                  pl.BlockSpec((None, 2, seq, LANES), lambda b, c, i: (b, c, 0, 0)),
                  pl.BlockSpec((None, 2, nv, hd, tv), lambda b, c, i: (b, c, 0, 0, 0))],
        out_specs=pl.BlockSpec((tq, LANES), lambda b, c, i: (b * nq + i, c)),
        out_shape=jax.ShapeDtypeStruct((batch * seq, heads * LANES // 2), F32),
        scratch_shapes=[pltpu.VMEM((2, tq, tq), F32), pltpu.VMEM((2, tq, tq), F32), pltpu.VMEM((2, hd, tq), F32)],
        compiler_params=_cparams("parallel", "parallel", "arbitrary"),
        name="flash",
    )(qa, ka, vt)


def _paged_specs(n_pages, tail):
    zeros = (0,) * len(tail)
    return [pl.BlockSpec((None,) + tail, functools.partial(lambda n, pt, p: (pt[n, p],) + zeros, p=p))
            for p in range(n_pages)]


def _pool_pages(pool):
    n_pool, page, h, dd = pool.shape
    return jnp.transpose(pool, (0, 2, 3, 1)).reshape(n_pool, h * dd, page)


def _pad_rows(x, rows):
    return jnp.concatenate([x, jnp.zeros((rows - x.shape[0], x.shape[1]), x.dtype)], axis=0)


def _query_rows(q, lane, rowh):
    t, w = q.shape
    rep = jnp.broadcast_to(q[:, None, :], (t, H_B, w)).reshape(t * H_B, w)
    return jnp.where((lane // HD_B) == rowh, rep, 0.0)


def _head_rows_to_tokens(o, lane, rowh, t):
    o = jnp.where((lane // HD_B) == rowh, o, 0.0)
    return jnp.sum(o.reshape(t, H_B, o.shape[1]), axis=1)


def _sample_softmax_pv(s_new, s_pages, v_new, vp, o_ref, lane, rowh, t):
    m = jnp.max(s_new, axis=1, keepdims=True)
    mm = s_pages[0]
    for s in s_pages[1:]:
        mm = jnp.maximum(mm, s)
    m = jnp.maximum(m, jnp.max(mm, axis=1, keepdims=True))
    p_new = jnp.exp(s_new - m)
    o = _dot(p_new.astype(BF16), v_new.astype(BF16))
    lsum = None
    for p, s in enumerate(s_pages):
        pp = jnp.exp(s - m)
        lsum = pp if lsum is None else lsum + pp
        o = o + _dot_nt(pp.astype(BF16), vp[p][...].astype(BF16))
    l = jnp.sum(p_new, axis=1, keepdims=True) + jnp.sum(lsum, axis=1, keepdims=True)
    o_ref[...] = _head_rows_to_tokens(o / l, lane, rowh, t)


def _moba_sample_kernel(pt_ref, q_ref, k_ref, v_ref, cos_ref, sin_ref, *rest, n_pages):
    del pt_ref
    kp, vp = rest[:n_pages], rest[n_pages:2 * n_pages]
    ke_ref, o_ref = rest[2 * n_pages:]
    t = q_ref.shape[0]
    nc = t * H_B
    nbp = n_pages * PAGE_SIZE // MOBA_BLOCK
    ppb = MOBA_BLOCK // PAGE_SIZE
    lane8 = lax.broadcasted_iota(jnp.int32, (SUBLANES, LANES), 1)
    cos2, sin2 = _pad_rows(cos_ref[...], SUBLANES), _pad_rows(sin_ref[...], SUBLANES)
    q8, k8, v8 = _pad_rows(q_ref[...], SUBLANES), _pad_rows(k_ref[...], SUBLANES), _pad_rows(v_ref[...], SUBLANES)
    qr = jnp.concatenate([_rope(q8[:, c * LANES:(c + 1) * LANES], cos2, sin2, lane8) for c in range(W_B // LANES)], axis=1)
    kr = jnp.concatenate([_rope(k8[:, c * LANES:(c + 1) * LANES], cos2, sin2, lane8) for c in range(W_B // LANES)], axis=1)
    ke_ref[...] = kr[:t]

    lane = lax.broadcasted_iota(jnp.int32, (nc, W_B), 1)
    rowh = lax.broadcasted_iota(jnp.int32, (nc, W_B), 0) % H_B
    qrows = _query_rows(qr[:t], lane, rowh)

    lanem = lax.broadcasted_iota(jnp.int32, (W_B, LANES), 1)
    means = jnp.zeros((W_B, LANES), F32)
    for j in range(nbp):
        blk_sum = kp[ppb * j][...]
        for u in range(1, ppb):
            blk_sum = blk_sum + kp[ppb * j + u][...]
        means = jnp.where(lanem == j, jnp.sum(blk_sum, axis=1, keepdims=True) * (1.0 / MOBA_BLOCK), means)
    q0, q1, q2 = _split3(qrows)
    m0, m1, m2 = _split3(means)
    gate = _dot(q0, m0) + (_dot(q0, m1) + _dot(q1, m0)) + (_dot(q0, m2) + _dot(q1, m1) + _dot(q2, m0))
    lg = lax.broadcasted_iota(jnp.int32, (nc, LANES), 1)
    sel = _top_select(jnp.where(lg < nbp, gate, NEG_INF), lg, 1, MOBA_TOPK)

    qb = (qrows * (HD_B ** -0.5)).astype(BF16)
    s_pages = []
    for p in range(n_pages):
        s = _dot(qb, kp[p][...].astype(BF16))
        j = p // ppb
        s_pages.append(jnp.where(sel[:, j:j + 1] > 0.5, s, NEG_INF))
    s_new = _dot_nt(qb, kr.astype(BF16))
    ktok = lax.broadcasted_iota(jnp.int32, (nc, SUBLANES), 1)
    qtok = lax.broadcasted_iota(jnp.int32, (nc, SUBLANES), 0) // H_B
    s_new = jnp.where((ktok <= qtok) & (ktok < t), s_new, NEG_INF)
    _sample_softmax_pv(s_new, s_pages, v8, vp, o_ref, lane, rowh, t)


def _moba_sample(q, k, v, pool_k, pool_v, page_table, cos2, sin2):
    n_seq, t, w = q.shape
    n_pages = page_table.shape[1]
    assert (n_pages * PAGE_SIZE) % MOBA_BLOCK == 0
    row = pl.BlockSpec((None, t, w), lambda n, pt: (n, 0, 0))
    tab = pl.BlockSpec((t, LANES), lambda n, pt: (0, 0))
    page = (w, PAGE_SIZE)
    return pl.pallas_call(
        functools.partial(_moba_sample_kernel, n_pages=n_pages),
        grid_spec=pltpu.PrefetchScalarGridSpec(
            num_scalar_prefetch=1,
            grid=(n_seq,),
            in_specs=[row, row, row, tab, tab] + _paged_specs(n_pages, page) + _paged_specs(n_pages, page),
            out_specs=[row, row],
        ),
        out_shape=[jax.ShapeDtypeStruct((n_seq, t, w), F32)] * 2,
        compiler_params=_cparams("arbitrary"),
        name="moba_sample",
    )(page_table, q, k, v, jnp.concatenate([cos2, cos2], axis=1), jnp.concatenate([sin2, sin2], axis=1),
      *([_pool_pages(pool_k)] * n_pages), *([_pool_pages(pool_v)] * n_pages))


def _fox_sample_kernel(pt_ref, q_ref, k_ref, v_ref, df_ref, bf_ref, *rest, n_pages):
    del pt_ref
    kp, vp, lp = rest[:n_pages], rest[n_pages:2 * n_pages], rest[2 * n_pages:3 * n_pages]
    lf_ref, o_ref = rest[3 * n_pages:]
    t = q_ref.shape[0]
    nc = t * H_D
    k8, v8 = _pad_rows(k_ref[...], SUBLANES), _pad_rows(v_ref[...], SUBLANES)
    lane = lax.broadcasted_iota(jnp.int32, (nc, W_D), 1)
    rowh = lax.broadcasted_iota(jnp.int32, (nc, W_D), 0) % H_D
    qb = (_query_rows(q_ref[...], lane, rowh) * (HD_D ** -0.5)).astype(BF16)

    lf_row = _log_sigmoid(df_ref[...] + bf_ref[...])
    lf_ref[...] = lf_row
    rr = lax.broadcasted_iota(jnp.int32, (nc, nc), 0)
    cc = lax.broadcasted_iota(jnp.int32, (nc, nc), 1)
    same = (rr % H_D) == (cc % H_D)
    cn_col = jnp.sum(jnp.where(same & (cc // H_D <= rr // H_D), lf_row, 0.0), axis=1, keepdims=True)
    ktok = lax.broadcasted_iota(jnp.int32, (nc, SUBLANES), 1)
    qtok = lax.broadcasted_iota(jnp.int32, (nc, SUBLANES), 0) // H_D
    cn_keys = jnp.zeros((nc, SUBLANES), F32)
    for tp in range(t):
        col = jnp.sum(jnp.where(same & (cc // H_D <= tp), lf_row, 0.0), axis=1, keepdims=True)
        cn_keys = jnp.where(ktok == tp, col, cn_keys)

    lfa = jnp.concatenate([lp[p][...] for p in range(n_pages)], axis=0)
    nr = lfa.shape[0]
    lane_r = lax.broadcasted_iota(jnp.int32, (nr, LANES), 1)
    incl = lfa
    sh = 1
    while sh < LANES:
        incl = incl + jnp.where(lane_r < LANES - sh, pltpu.roll(incl, LANES - sh, 1), 0.0)
        sh *= 2
    r2 = lax.broadcasted_iota(jnp.int32, (nr, nr), 0)
    c2 = lax.broadcasted_iota(jnp.int32, (nr, nr), 1)
    later = ((c2 > r2) & ((c2 % H_D) == (r2 % H_D))).astype(BF16)
    suf = (incl - lfa) + _dot3(later, jnp.broadcast_to(incl[:, 0:1], (nr, LANES)))

    s_pages = []
    for p in range(n_pages):
        bias = jnp.concatenate([suf[H_D * p:H_D * (p + 1)]] * t, axis=0)
        s_pages.append(_dot(qb, kp[p][...].astype(BF16)) + bias + cn_col)
    s_new = _dot_nt(qb, k8.astype(BF16)) + (cn_col - cn_keys)
    s_new = jnp.where((ktok <= qtok) & (ktok < t), s_new, NEG_INF)
    _sample_softmax_pv(s_new, s_pages, v8, vp, o_ref, lane, rowh, t)


def _fox_sample(q, k, v, df, bf, pool_k, pool_v, pool_lf, page_table):
    n_seq, t, w = q.shape
    nc = t * H_D
    n_pages = page_table.shape[1]
    bft = jnp.tile(bf.astype(F32), t)
    row = pl.BlockSpec((None, t, w), lambda n, pt: (n, 0, 0))
    frow = pl.BlockSpec((None, 1, nc), lambda n, pt: (n, 0, 0))
    page = (w, PAGE_SIZE)
    lf, o = pl.pallas_call(
        functools.partial(_fox_sample_kernel, n_pages=n_pages),
        grid_spec=pltpu.PrefetchScalarGridSpec(
            num_scalar_prefetch=1,
            grid=(n_seq,),
            in_specs=[row, row, row, frow, pl.BlockSpec((1, nc), lambda n, pt: (0, 0))]
            + _paged_specs(n_pages, page) + _paged_specs(n_pages, page) + _paged_specs(n_pages, (H_D, PAGE_SIZE)),
            out_specs=[frow, row],
        ),
        out_shape=[jax.ShapeDtypeStruct((n_seq, 1, nc), F32), jax.ShapeDtypeStruct((n_seq, t, w), F32)],
        compiler_params=_cparams("arbitrary"),
        name="fox_sample",
    )(page_table, q, k, v, df.reshape(n_seq, 1, nc), bft.reshape(1, nc),
      *([_pool_pages(pool_k)] * n_pages), *([_pool_pages(pool_v)] * n_pages),
      *([jnp.transpose(pool_lf, (0, 2, 1))] * n_pages))
    return lf, o


def _shift_rows(x, d, fill):
    return jnp.concatenate([jnp.full((d, x.shape[1]), fill, x.dtype), x[:-d]], axis=0)


def _rglru_gates(xc, wa_ref, ba_ref, wx_ref, bx_ref, sp_ref, first_row_pos0):
    xb = xc.astype(BF16)
    r = _sigmoid(_dot(xb, wa_ref[...]) + ba_ref[...])
    i = _sigmoid(_dot(xb, wx_ref[...]) + bx_ref[...])
    log_a = -RG_C * r * sp_ref[...]
    a = jnp.exp(log_a)
    mult = jnp.sqrt(1.0 - jnp.exp(2.0 * log_a))
    if first_row_pos0 is not None:
        mult = jnp.where(first_row_pos0, 1.0, mult)
    return a, xc * i * mult


def _rglru_kernel(cx_ref, cg_ref, cw_ref, cb_ref, wa_ref, ba_ref, wx_ref, bx_ref, sp_ref,
                  o_ref, buf_ref, h_ref, tail_scr, h_scr):
    t = pl.program_id(1)
    tt = cx_ref.shape[0]

    @pl.when(t == 0)
    def _():
        tail_scr[...] = jnp.zeros_like(tail_scr)
        h_scr[...] = jnp.zeros_like(h_scr)

    cx = cx_ref[...]
    ext = jnp.concatenate([tail_scr[...], cx], axis=0)
    xc = cb_ref[...] + cx * cw_ref[CONV_W - 1:CONV_W, :]
    for d in range(1, CONV_W):
        xc = xc + ext[SUBLANES - d:SUBLANES - d + tt] * cw_ref[CONV_W - 1 - d:CONV_W - d, :]
    row = lax.broadcasted_iota(jnp.int32, (tt, W_C), 0)
    a, b = _rglru_gates(xc, wa_ref, ba_ref, wx_ref, bx_ref, sp_ref, (row == 0) & (t == 0))
    d = 1
    while d < tt:
        b = b + a * _shift_rows(b, d, 0.0)
        a = a * _shift_rows(a, d, 1.0)
        d *= 2
    h = a * h_scr[0:1, :] + b
    o_ref[...] = h * _gelu_tanh(cg_ref[...])
    h_scr[0:1, :] = h[tt - 1:tt, :]
    tail_scr[...] = cx[tt - SUBLANES:tt, :]

    @pl.when(t == pl.num_programs(1) - 1)
    def _():
        buf_ref[...] = cx[tt - (CONV_W - 1):tt, :]
        h_ref[...] = h[tt - 1:tt, :]


def _rglru_prompt(proj, batch, seq, cw, cb, wa, ba, wx, bx, sp, tt=256):
    nt = seq // tt
    vec = pl.BlockSpec((1, W_C), lambda b, t: (0, 0))
    mat = pl.BlockSpec((W_C, W_C), lambda b, t: (0, 0))
    o, buf, h = pl.pallas_call(
        _rglru_kernel,
        grid=(batch, nt),
        in_specs=[pl.BlockSpec((tt, W_C), lambda b, t: (b * nt + t, 0)),
                  pl.BlockSpec((tt, W_C), lambda b, t: (b * nt + t, 1)),
                  pl.BlockSpec((CONV_W, W_C), lambda b, t: (0, 0)), vec, mat, vec, mat, vec, vec],
        out_specs=[pl.BlockSpec((tt, W_C), lambda b, t: (b * nt + t, 0)),
                   pl.BlockSpec((None, CONV_W - 1, W_C), lambda b, t: (b, 0, 0)),
                   pl.BlockSpec((None, 1, W_C), lambda b, t: (b, 0, 0))],
        out_shape=[jax.ShapeDtypeStruct((batch * seq, W_C), F32),
                   jax.ShapeDtypeStruct((batch, CONV_W - 1, W_C), F32),
                   jax.ShapeDtypeStruct((batch, 1, W_C), F32)],
        scratch_shapes=[pltpu.VMEM((SUBLANES, W_C), F32), pltpu.VMEM((SUBLANES, W_C), F32)],
        compiler_params=_cparams("parallel", "arbitrary"),
        name="rglru",
    )(proj, proj, cw, cb, wa, ba, wx, bx, sp)
    return o, buf, h.reshape(batch, W_C)


def _rglru_sample_kernel(cx_ref, cg_ref, buf_ref, h0_ref, cw_ref, cb_ref, wa_ref, ba_ref, wx_ref, bx_ref, sp_ref,
                         o_ref, h_ref, *, pos0_is_zero):
    t, n, _ = cx_ref.shape
    xp = [buf_ref[j] for j in range(CONV_W - 1)] + [cx_ref[j] for j in range(t)]
    xcs = []
    for s in range(t):
        xc = cb_ref[...] + xp[s] * cw_ref[0:1, :]
        for j in range(1, CONV_W):
            xc = xc + xp[s + j] * cw_ref[j:j + 1, :]
        xcs.append(xc)
    xc = jnp.concatenate(xcs, axis=0)
    first = (lax.broadcasted_iota(jnp.int32, xc.shape, 0) < n) if pos0_is_zero else None
    a, b = _rglru_gates(xc, wa_ref, ba_ref, wx_ref, bx_ref, sp_ref, first)
    h = h0_ref[...]
    for s in range(t):
        h = a[s * n:(s + 1) * n] * h + b[s * n:(s + 1) * n]
        o_ref[s] = h * _gelu_tanh(cg_ref[s])
    h_ref[...] = h


def _rglru_sample(cx, cg, buf, h0, pos0_is_zero, cw, cb, wa, ba, wx, bx, sp):
    t, n, _ = cx.shape
    return pl.pallas_call(
        functools.partial(_rglru_sample_kernel, pos0_is_zero=pos0_is_zero),
        out_shape=[jax.ShapeDtypeStruct((t, n, W_C), F32), jax.ShapeDtypeStruct((n, W_C), F32)],
        compiler_params=pltpu.CompilerParams(vmem_limit_bytes=VMEM_LIMIT),
        name="rglru_sample",
    )(cx, cg, buf, h0, cw, cb, wa, ba, wx, bx, sp)


def _odd_weights(cw, cb, wa, ba, wx, bx, lam):
    def bd(w):
        return jax.scipy.linalg.block_diag(*[w[g] for g in range(NB_C)]).astype(BF16)

    def r(v):
        return v.reshape(1, W_C).astype(F32)

    return cw.astype(F32), r(cb), bd(wa), r(ba), bd(wx), r(bx), r(jax.nn.softplus(-lam.astype(F32)))


FOX_ONE_LANE = 3 * H_D


def _fox_selectors():
    selq = np.zeros((H_D // 2, LANES, LANES), np.float32)
    selk = np.zeros((H_D // 2, LANES, LANES), np.float32)
    for cc in range(H_D // 2):
        for h, base in ((2 * cc, HD_D), (2 * cc + 1, 0)):
            for piece in range(3):
                selq[cc, piece * H_D + h, base + piece] = 1.0
                selq[cc, FOX_ONE_LANE, base + 3 + piece] = 1.0
                selk[cc, FOX_ONE_LANE, base + piece] = 1.0
                selk[cc, piece * H_D + h, base + 3 + piece] = -1.0
    return jnp.asarray(selq, BF16), jnp.asarray(selk, BF16)


def _split_features(c, lane):
    hi, mid, lo = _split3(jnp.where(lane < H_D, c, 0.0))
    feat = (hi.astype(F32) + pltpu.roll(mid.astype(F32), H_D, 1) + pltpu.roll(lo.astype(F32), 2 * H_D, 1)
            + jnp.where(lane == FOX_ONE_LANE, 1.0, 0.0))
    return feat.astype(BF16)


def _fox_prep_kernel(q_ref, k_ref, v_ref, df_ref, bf_ref, selq_ref, selk_ref,
                     lf_ref, kt_ref, vl_ref, qa_ref, ka_ref, vt_ref, c_scr):
    t = pl.program_id(1)
    rows = q_ref.shape[0]

    @pl.when(t == 0)
    def _():
        c_scr[...] = jnp.zeros_like(c_scr)

    lane = lax.broadcasted_iota(jnp.int32, (rows, LANES), 1)
    low = lane < HD_D
    logf = jnp.where(lane < H_D, _log_sigmoid(df_ref[...] + bf_ref[...]), 0.0)
    lf_ref[...] = logf.T[:H_D]
    tri =(lax.broadcasted_iota(jnp.int32, (rows, rows), 0) >= lax.broadcasted_iota(jnp.int32, (rows, rows), 1)).astype(BF16)
    c = c_scr[0:1, :] + _dot3(tri, logf)
    c_scr[0:1, :] = c[rows - 1:rows, :]
    feat = _split_features(c * LOG2E, lane)
    scale = HD_D ** -0.5 * LOG2E
    for cc in range(W_D // LANES):
        sl = slice(cc * LANES, (cc + 1) * LANES)
        eq = _dot(feat, selq_ref[cc])
        ek = _dot(feat, selk_ref[cc])
        qs, kc = q_ref[:, sl] * scale, k_ref[:, sl]
        qa_ref[2 * cc] = jnp.where(low, qs, eq).astype(BF16)
        qa_ref[2 * cc + 1] = jnp.where(low, eq, qs).astype(BF16)
        ka_ref[2 * cc] = jnp.where(low, kc, ek).astype(BF16)
        ka_ref[2 * cc + 1] = jnp.where(low, ek, kc).astype(BF16)
        _store_heads_t([kt_ref], cc, kc)
        _store_heads_t([vl_ref, vt_ref], cc, v_ref[:, sl])


def _fox_prep(proj, batch, seq, bf_row, selq, selk):
    rows = ATT_TILE
    nt = seq // rows

    def tok(cb):
        return pl.BlockSpec((rows, W_D), lambda b, t: (b * nt + t, cb))

    specs, shapes = _pair_specs(batch, H_D, seq)
    sel = pl.BlockSpec((H_D // 2, LANES, LANES), lambda b, t: (0, 0, 0))
    cb0 = COLS_O_Q // W_D
    return pl.pallas_call(
        _fox_prep_kernel,
        grid=(batch, nt),
        in_specs=[tok(cb0), tok(cb0 + 1), tok(cb0 + 2),
                  pl.BlockSpec((rows, LANES), lambda b, t: (b * nt + t, COLS_O_F // LANES)),
                  pl.BlockSpec((1, LANES), lambda b, t: (0, 0)), sel, sel],
        out_specs=[pl.BlockSpec((None, H_D, rows), lambda b, t: (b, 0, t))] + specs,
        out_shape=[jax.ShapeDtypeStruct((batch, H_D, seq), F32)] + shapes,
        scratch_shapes=[pltpu.VMEM((SUBLANES, LANES), F32)],
        compiler_params=_cparams("parallel", "arbitrary"),
        name="fox_prep",
    )(proj, proj, proj, proj, bf_row, selq, selk)


def _router_kernel(x_ref, g_ref, r0_ref, r1_ref, r2_ref, o_ref, lg_ref):
    x = x_ref[...]
    y = x * lax.rsqrt(jnp.mean(x * x, axis=-1, keepdims=True) + EPS) * g_ref[...]
    o_ref[...] = y.astype(o_ref.dtype)
    y0, y1, y2 = _split3(y)
    r0, r1, r2 = r0_ref[...], r1_ref[...], r2_ref[...]
    lg_ref[...] = (_dot(y0, r0) + (_dot(y0, r1) + _dot(y1, r0)) + (_dot(y0, r2) + _dot(y1, r1) + _dot(y2, r0)))


def _rmsnorm_router(x, g, router, tm=512):
    m, d = x.shape
    rp = jnp.zeros((d, LANES), F32).at[:, :router.shape[1]].set(router.astype(F32))
    r0 = rp.astype(BF16)
    r1 = (rp - r0.astype(F32)).astype(BF16)
    r2 = (rp - r0.astype(F32) - r1.astype(F32)).astype(BF16)
    rspec = pl.BlockSpec((d, LANES), lambda i: (0, 0))
    return pl.pallas_call(
        _router_kernel,
        grid=(m // tm,),
        in_specs=[pl.BlockSpec((tm, d), lambda i: (i, 0)), pl.BlockSpec((1, d), lambda i: (0, 0)), rspec, rspec, rspec],
        out_specs=[pl.BlockSpec((tm, d), lambda i: (i, 0)), pl.BlockSpec((tm, LANES), lambda i: (i, 0))],
        out_shape=[jax.ShapeDtypeStruct((m, d), F32), jax.ShapeDtypeStruct((m, LANES), F32)],
        compiler_params=_cparams("parallel"),
        name="rmsnorm_router",
    )(x, g.reshape(1, d), r0, r1, r2)


def _combine_norm_kernel(y_ref, ya_ref, yb_ref, gt_ref, g_ref, op_ref, os_ref, *, n_prompt):
    i = pl.program_id(0)
    gt = gt_ref[...]
    out = _rms(y_ref[...] + (gt[:, 0:1] * ya_ref[...] + gt[:, 1:2] * yb_ref[...]), g_ref[...])

    @pl.when(i < n_prompt)
    def _():
        op_ref[...] = out

    @pl.when(i >= n_prompt)
    def _():
        os_ref[...] = out


def _combine_norm(y, ya, yb, gates, g, mp, tm=512):
    m, d = y.shape
    assert mp % tm == 0 and (m - mp) % tm == 0
    n_p = mp // tm
    row = pl.BlockSpec((tm, d), lambda i: (i, 0))
    return pl.pallas_call(
        functools.partial(_combine_norm_kernel, n_prompt=n_p),
        grid=(m // tm,),
        in_specs=[row, row, row, pl.BlockSpec((tm, TOP_K), lambda i: (i, 0)), pl.BlockSpec((1, d), lambda i: (0, 0))],
        out_specs=[pl.BlockSpec((tm, d), lambda i: (jnp.minimum(i, n_p - 1), 0)),
                   pl.BlockSpec((tm, d), lambda i: (jnp.maximum(i - n_p, 0), 0))],
        out_shape=[jax.ShapeDtypeStruct((mp, d), F32), jax.ShapeDtypeStruct((m - mp, d), F32)],
        compiler_params=_cparams("arbitrary"),
        name="combine_norm",
    )(y, ya, yb, gates, g.reshape(1, d))


def _moe_dispatch(logits, tm):
    m, e = logits.shape
    idx = lax.broadcasted_iota(jnp.int32, (m, e), 1)
    m1 = jnp.max(logits, axis=1, keepdims=True)
    i1 = jnp.min(jnp.where(logits == m1, idx, e), axis=1, keepdims=True)
    rest = jnp.where(idx == i1, -jnp.inf, logits)
    m2 = jnp.max(rest, axis=1, keepdims=True)
    i2 = jnp.min(jnp.where(rest == m2, idx, e), axis=1, keepdims=True)
    ex = jnp.exp(m2 - m1)
    gates = jnp.concatenate([1.0 / (1.0 + ex), ex / (1.0 + ex)], axis=1)
    flat_e = jnp.concatenate([i1, i2], axis=1).reshape(m * TOP_K)
    onehot = (flat_e[:, None] == jnp.arange(e, dtype=jnp.int32)[None, :]).astype(jnp.int32)
    csum = jnp.cumsum(onehot, axis=0)
    counts = csum[-1]
    padded = -(-counts // tm) * tm
    ends = jnp.cumsum(padded)
    dest = jnp.sum(onehot * ((ends - padded)[None, :] + csum - 1), axis=1)
    n_tiles = (m * TOP_K) // tm + e
    src_token = jnp.zeros((n_tiles * tm,), jnp.int32).at[dest].set(jnp.arange(m * TOP_K, dtype=jnp.int32) // TOP_K)
    tile_start = jnp.arange(n_tiles, dtype=jnp.int32) * tm
    tile_expert = jnp.minimum(jnp.sum((ends[None, :] <= tile_start[:, None]).astype(jnp.int32), axis=1), e - 1)
    return gates, src_token, dest.reshape(m, TOP_K), tile_expert, (ends[-1:] // tm).astype(jnp.int32)


def kernel(x_prompt, x_sample, cache_k_e, cache_v_e, state_s_e, state_conv_o, state_h_o, cache_k_o, cache_v_o,
           cache_logf_o, page_table, w_in_e, lb_logits, gnorm_a, w_out_e, ffn_w1, ffn_w3, ffn_w2, w_in_o, conv_w,
           conv_b, rg_wa, rg_ba, rg_wx, rg_bx, rg_lambda, fox_bf, w_out_o, moe_router, moe_w1, moe_w3, moe_w2,
           norm_mix, norm_ffn, norm_final):
    batch, seq, d = x_prompt.shape
    n_seq, ts, _ = x_sample.shape
    mp, ms = batch * seq, n_seq * ts
    m = mp + ms
    n_pages = page_table.shape[1]
    past = n_pages * PAGE_SIZE
    chunk_a = 64
    assert d == D_MODEL and ts <= SUBLANES and seq % chunk_a == 0 and m % chunk_a == 0

    x = jnp.concatenate([x_prompt.reshape(mp, d), x_sample.reshape(ms, d)], axis=0)
    pos_p = jnp.arange(seq, dtype=jnp.int32)
    pos_s = past + jnp.arange(ts, dtype=jnp.int32)

    def seq_rows(a):
        return a.reshape(n_seq, ts, a.shape[1])

    proj = _matmul([x], w_in_e[0].astype(BF16), norm_g=norm_mix[0])
    proj_s = proj[mp:]
    lb = jnp.cumsum(jax.nn.softmax(lb_logits.astype(F32), axis=0), axis=0)[0]
    oa_p, s_p = _hgrn2(proj.reshape(m // chunk_a, chunk_a, proj.shape[1]), batch, seq // chunk_a,
                       jnp.zeros((batch, H_A, DK_A, DV_A), F32), lb, gnorm_a[0])
    oa_s, s_s = _hgrn2(proj_s.reshape(n_seq, ts, proj.shape[1]), n_seq, 1, state_s_e[0], lb, gnorm_a[0])
    cos_p, sin_p = _rope_tables(pos_p)
    cos_s, sin_s = _rope_tables(pos_s)
    ke_p, ve_p, qa, ka, vt = _moba_prep(proj, batch, seq, cos_p, sin_p)
    ob_p = _flash(qa, ka, vt)
    ve_s = proj_s[:, COLS_E_Q + 2 * W_B:COLS_E_Q + 3 * W_B]
    ke_s, ob_s = _moba_sample(seq_rows(proj_s[:, COLS_E_Q:COLS_E_Q + W_B]),
                              seq_rows(proj_s[:, COLS_E_Q + W_B:COLS_E_Q + 2 * W_B]), seq_rows(ve_s),
                              cache_k_e[0], cache_v_e[0], page_table, cos_s, sin_s)
    y = _matmul([(oa_p, oa_s), (ob_p, ob_s.reshape(ms, W_B))], w_out_e[0].astype(BF16), res=x)
    y = _ffn(y, ffn_w1.astype(BF16), ffn_w3.astype(BF16), ffn_w2.astype(BF16), residual=True, norm_g=norm_ffn[0],
             tf=ffn_w1.shape[2] // 2)

    cols_o = w_in_o.shape[2]
    cols_pad = -(-cols_o // LANES) * LANES
    w_in_o_p = jnp.zeros((d, cols_pad), BF16).at[:, :cols_o].set(w_in_o[0].astype(BF16))
    proj_o = _matmul([y], w_in_o_p, norm_g=norm_mix[1])
    proj_os = proj_o[mp:]
    ow = _odd_weights(conv_w[0], conv_b[0], rg_wa[0], rg_ba[0], rg_wx[0], rg_bx[0], rg_lambda[0])
    oc_p, buf_p, h_p = _rglru_prompt(proj_o, batch, seq, *ow)
    ps = proj_os.reshape(n_seq, ts, cols_pad)
    cx_s = ps[..., :W_C]
    oc_s, h_s = _rglru_sample(cx_s.swapaxes(0, 1), ps[..., W_C:2 * W_C].swapaxes(0, 1), state_conv_o[0].swapaxes(0, 1),
                              state_h_o[0], past == 0, *ow)
    oc_s = oc_s.swapaxes(0, 1).reshape(ms, W_C)
    buf_s = jnp.concatenate([state_conv_o[0].astype(F32), cx_s], axis=1)[:, ts:]
    bf_row = jnp.zeros((1, LANES), F32).at[0, :H_D].set(fox_bf[0].astype(F32))
    selq, selk = _fox_selectors()
    lf_p, ko_p, vo_p, qa, ka, vt = _fox_prep(proj_o, batch, seq, bf_row, selq, selk)
    od_p = _flash(qa, ka, vt)
    ko_s = proj_os[:, COLS_O_Q + W_D:COLS_O_Q + 2 * W_D]
    vo_s = proj_os[:, COLS_O_Q + 2 * W_D:COLS_O_Q + 3 * W_D]
    lf_s, od_s = _fox_sample(seq_rows(proj_os[:, COLS_O_Q:COLS_O_Q + W_D]), seq_rows(ko_s), seq_rows(vo_s),
                             proj_os[:, COLS_O_F:COLS_O_F + H_D].reshape(n_seq, ts * H_D), fox_bf[0],
                             cache_k_o[0], cache_v_o[0], cache_logf_o[0], page_table)
    y = _matmul([(oc_p, oc_s), (od_p, od_s.reshape(ms, W_D))], w_out_o[0].astype(BF16), res=y)
    tm_e = 768 if (m * TOP_K) % 768 == 0 else 512
    hn, logits = _rmsnorm_router(y, norm_ffn[1], moe_router[0])
    gates, src_token, pair_pos, tile_expert, n_used = _moe_dispatch(logits[:, :N_EXPERTS], tm_e)
    y_e = _ffn(jnp.take(hn, src_token, axis=0), moe_w1[0], moe_w3[0], moe_w2[0], tile_expert=tile_expert, n_used=n_used,
               tm=tm_e, tf=512)
    out_p, out_s = _combine_norm(y, jnp.take(y_e, pair_pos[:, 0], axis=0), jnp.take(y_e, pair_pos[:, 1], axis=0),
                                 gates, norm_final, mp)

    def heads(a, h):
        return a.reshape(1, n_seq, ts, h, a.shape[-1] // h)

    def heads_t(a):
        return jnp.transpose(a, (0, 3, 1, 2))[None]

    return (out_p.reshape(batch, seq, d), out_s.reshape(n_seq, ts, d),
            s_p[None], s_s[None],
            heads_t(ke_p), heads_t(ve_p), heads(ke_s, H_B), heads(ve_s, H_B),
            buf_p[None], buf_s[None], h_p[None], h_s[None],
            heads_t(ko_p), heads_t(vo_p), jnp.transpose(lf_p, (0, 2, 1))[None],
            heads(ko_s, H_D), heads(vo_s, H_D), lf_s.reshape(1, n_seq, ts, H_D))
```

```python
import functools
import math

import numpy as np
import jax
import jax.numpy as jnp
from jax import lax
from jax.experimental import pallas as pl
from jax.experimental.pallas import tpu as pltpu

F32 = jnp.float32
BF16 = jnp.bfloat16

D_MODEL = 1024
PAGE_SIZE = 128
H_A, DK_A, DV_A = 4, 128, 128
F_A, W_A = H_A * DK_A, H_A * DV_A
H_B, HD_B = 8, 64
W_B = H_B * HD_B
MOBA_BLOCK, MOBA_TOPK = 256, 3
W_C, NB_C, CONV_W, RG_C = 512, 8, 4, 8.0
BW_C = W_C // NB_C
H_D, HD_D = 8, 64
W_D = H_D * HD_D
N_EXPERTS, TOP_K = 8, 2
ROPE_THETA = 10000.0
EPS = 1e-6
NEG_INF = -1e30
LOG2E = math.log2(math.e)

COLS_E_Q = 2 * F_A + 2 * W_A
COLS_O_Q = 2 * W_C
COLS_O_F = 2 * W_C + 3 * W_D

LANES = 128
SUBLANES = 8
VMEM_LIMIT = 56 * 1024 * 1024
ATT_TILE = 256


def _cparams(*sem):
    return pltpu.CompilerParams(dimension_semantics=sem, vmem_limit_bytes=VMEM_LIMIT)


def _split3(x):
    hi = x.astype(BF16)
    r1 = x - hi.astype(F32)
    mid = r1.astype(BF16)
    lo = (r1 - mid.astype(F32)).astype(BF16)
    return hi, mid, lo


def _dot(a, b):
    return jnp.dot(a, b, preferred_element_type=F32)


def _dot_nt(a, b):
    return lax.dot_general(a, b, (((1,), (1,)), ((), ())), preferred_element_type=F32)


def _dot_tn(a, b):
    return lax.dot_general(a, b, (((0,), (0,)), ((), ())), preferred_element_type=F32)


def _dot3(a, b):
    b0, b1, b2 = _split3(b)
    return _dot(a, b0) + _dot(a, b1) + _dot(a, b2)


def _sigmoid(x):
    return 1.0 / (1.0 + jnp.exp(-x))


def _log_sigmoid(x):
    return jnp.minimum(x, 0.0) - jnp.log(1.0 + jnp.exp(-jnp.abs(x)))


def _gelu_tanh(x):
    return 0.5 * x * (1.0 + jnp.tanh(math.sqrt(2.0 / math.pi) * (x + 0.044715 * (x * x * x))))


def _rms(x, g):
    return x * lax.rsqrt(jnp.mean(x * x, axis=-1, keepdims=True) + EPS) * g


def _matmul_kernel(*refs, parts, n_prompt, has_norm, has_res):
    i = pl.program_id(0)
    refs = list(refs)
    g_ref = refs.pop(0) if has_norm else None
    xs = []
    for split in parts:
        if split:
            p_ref, s_ref = refs.pop(0), refs.pop(0)
            xs.append(jnp.where(i < n_prompt, p_ref[...], s_ref[...]))
        else:
            xs.append(refs.pop(0)[...])
    w_ref = refs.pop(0)
    res = None
    if has_res == "split":
        p_ref, s_ref = refs.pop(0), refs.pop(0)
        res = jnp.where(i < n_prompt, p_ref[...], s_ref[...])
    elif has_res:
        res = refs.pop(0)[...]
    o_ref = refs.pop(0)
    acc = None
    k0 = 0
    for x in xs:
        if has_norm:
            x = _rms(x, g_ref[...])
        kk = x.shape[1]
        part = _dot(x.astype(BF16), w_ref[k0:k0 + kk, :])
        acc = part if acc is None else acc + part
        k0 += kk
    if res is not None:
        acc = acc + res
    o_ref[...] = acc


def _matmul(xs, w, res=None, norm_g=None, tm=512):
    kt, n = w.shape
    parts = tuple(isinstance(x, tuple) for x in xs)
    m = sum(a.shape[0] for a in xs[0]) if parts[0] else xs[0].shape[0]
    in_specs, args = [], []
    n_prompt = [0]

    def add(x):
        if isinstance(x, tuple):
            xp, xsm = x
            assert xp.shape[0] % tm == 0 and xsm.shape[0] % tm == 0 and xp.shape[0] + xsm.shape[0] == m
            n_p = n_prompt[0] = xp.shape[0] // tm
            in_specs.append(pl.BlockSpec((tm, xp.shape[1]), lambda i: (jnp.minimum(i, n_p - 1), 0)))
            in_specs.append(pl.BlockSpec((tm, xp.shape[1]), lambda i: (jnp.maximum(i - n_p, 0), 0)))
            args.extend([xp, xsm])
        else:
            in_specs.append(pl.BlockSpec((tm, x.shape[1]), lambda i: (i, 0)))
            args.append(x)

    if norm_g is not None:
        assert len(xs) == 1
        in_specs.append(pl.BlockSpec((1, kt), lambda i: (0, 0)))
        args.append(norm_g.reshape(1, kt).astype(F32))
    for x in xs:
        add(x)
    in_specs.append(pl.BlockSpec((kt, n), lambda i: (0, 0)))
    args.append(w)
    if res is not None:
        add(res)
    has_res = "split" if isinstance(res, tuple) else res is not None
    return pl.pallas_call(
        functools.partial(_matmul_kernel, parts=parts, n_prompt=n_prompt[0], has_norm=norm_g is not None,
                          has_res=has_res),
        grid=(m // tm,),
        in_specs=in_specs,
        out_specs=pl.BlockSpec((tm, n), lambda i: (i, 0)),
        out_shape=jax.ShapeDtypeStruct((m, n), F32),
        compiler_params=_cparams("parallel"),
        name="matmul",
    )(*args)


def _ffn_kernel(te_ref, nu_ref, x_ref, w1_ref, w3_ref, w2_ref, *rest, has_norm, has_res):
    del te_ref
    rest = list(rest)
    g_ref = rest.pop(0) if has_norm else None
    res_ref = x_ref if has_res else None
    o_ref, acc_ref, xb_ref = rest
    i, f = pl.program_id(0), pl.program_id(1)

    @pl.when(i < nu_ref[0])
    def _():
        @pl.when(f == 0)
        def _():
            x = x_ref[...]
            if has_norm:
                x = _rms(x, g_ref[...])
            xb_ref[...] = x.astype(BF16)
            acc_ref[...] = jnp.zeros_like(acc_ref)

        x = xb_ref[...]
        a = _dot(x, w1_ref[...].astype(BF16))
        b = _dot(x, w3_ref[...].astype(BF16))
        g = (a * _sigmoid(a) * b).astype(BF16)
        acc_ref[...] += _dot(g, w2_ref[...].astype(BF16))

        @pl.when(f == pl.num_programs(1) - 1)
        def _():
            out = acc_ref[...]
            if has_res:
                out = out + res_ref[...]
            o_ref[...] = out


def _ffn(x, w1, w3, w2, residual=False, norm_g=None, tile_expert=None, n_used=None, tm=512, tf=256):
    m, d = x.shape
    f = w1.shape[-1]
    nf = f // tf
    if tile_expert is None:
        tile_expert = jnp.zeros((m // tm,), jnp.int32)
    if n_used is None:
        n_used = jnp.full((1,), m // tm, jnp.int32)

    def fblk(i, j, nu):
        return jnp.where(i < nu[0], j, nf - 1)

    row = pl.BlockSpec((tm, d), lambda i, j, te, nu: (i, 0))
    in_specs = [
        row,
        pl.BlockSpec((None, d, tf), lambda i, j, te, nu: (te[i], 0, fblk(i, j, nu))),
        pl.BlockSpec((None, d, tf), lambda i, j, te, nu: (te[i], 0, fblk(i, j, nu))),
        pl.BlockSpec((None, tf, d), lambda i, j, te, nu: (te[i], fblk(i, j, nu), 0)),
    ]
    args = [x, w1, w3, w2]
    if norm_g is not None:
        in_specs.append(pl.BlockSpec((1, d), lambda i, j, te, nu: (0, 0)))
        args.append(norm_g.reshape(1, d).astype(F32))
    return pl.pallas_call(
        functools.partial(_ffn_kernel, has_norm=norm_g is not None, has_res=residual),
        grid_spec=pltpu.PrefetchScalarGridSpec(
            num_scalar_prefetch=2,
            grid=(m // tm, nf),
            in_specs=in_specs,
            out_specs=row,
            scratch_shapes=[pltpu.VMEM((tm, d), F32), pltpu.VMEM((tm, d), BF16)],
        ),
        out_shape=jax.ShapeDtypeStruct((m, d), F32),
        compiler_params=_cparams("arbitrary", "arbitrary"),
        name="ffn",
    )(tile_expert, n_used, *args)


def _hgrn2_head(q, k, v, logf, st, c_sub):
    c = q.shape[0]
    ns = c // c_sub
    row = lax.broadcasted_iota(jnp.int32, (c, c), 0)
    col = lax.broadcasted_iota(jnp.int32, (c, c), 1)
    tri = (row >= col).astype(BF16)
    g = _dot3(tri, logf)
    vb = v.astype(BF16)
    o = _dot_nt((q * jnp.exp(g)).astype(BF16), st.astype(BF16))

    lane_c = lax.broadcasted_iota(jnp.int32, (c_sub, c), 1)
    slabs = []
    for i in range(ns):
        r0 = i * c_sub
        qi, ki, gi = q[r0:r0 + c_sub], k[r0:r0 + c_sub], g[r0:r0 + c_sub]
        slab = jnp.zeros((c_sub, c), F32)
        for s in range(c_sub):
            w = jnp.exp(jnp.minimum(gi - gi[s:s + 1, :], 0.0))
            colv = jnp.sum(qi * w * ki[s:s + 1, :], axis=-1, keepdims=True)
            slab = jnp.where(lane_c == r0 + s, colv, slab)
        slabs.append(slab)
    scores = slabs[0] if ns == 1 else jnp.concatenate(slabs, axis=0)
    scores = jnp.where((row >= col) & (row // c_sub == col // c_sub), scores, 0.0)

    w = c // 2
    while w >= c_sub:
        q_parts, k_parts = [], []
        for b in range(c // w):
            r0 = b * w
            if b % 2 == 1:
                q_parts.append(q[r0:r0 + w] * jnp.exp(g[r0:r0 + w] - g[r0 - 1:r0, :]))
                k_parts.append(jnp.zeros((w, q.shape[1]), F32))
            else:
                q_parts.append(jnp.zeros((w, q.shape[1]), F32))
                k_parts.append(k[r0:r0 + w] * jnp.exp(g[r0 + w - 1:r0 + w, :] - g[r0:r0 + w]))
        lvl = _dot_nt(jnp.concatenate(q_parts, axis=0).astype(BF16), jnp.concatenate(k_parts, axis=0).astype(BF16))
        scores = scores + jnp.where(((row // w) % 2 == 1) & (col // w == row // w - 1), lvl, 0.0)
        w //= 2
    o = o + _dot(scores.astype(BF16), vb)
    gl = g[c - 1:c, :]
    ke = (k * jnp.exp(gl - g)).astype(BF16)
    st_new = jnp.exp(gl) * st + _dot_tn(vb, ke)
    return o, st_new


def _hgrn2_kernel(aq_ref, af_ref, ai_ref, ag_ref, lb_ref, gn_ref, s0_ref, o_ref, s_ref, st_scr, *, c_sub, ct):
    t = pl.program_id(1)
    group = aq_ref.shape[0]

    @pl.when(t == 0)
    def _():
        for gi in range(group):
            for h in range(H_A):
                st_scr[gi, h] = s0_ref[gi, h].T

    def padded(x):
        if ct < SUBLANES:
            x = jnp.concatenate([x, jnp.zeros((SUBLANES - ct, x.shape[1]), F32)], axis=0)
        return x

    lb = lb_ref[...]
    for gi in range(group):
        aq, zf, vi = padded(aq_ref[gi]), padded(af_ref[gi]), padded(ai_ref[gi])
        c = aq.shape[0]
        real = lax.broadcasted_iota(jnp.int32, (c, F_A), 0) < ct
        sig = _sigmoid(zf)
        q = aq * _sigmoid(aq)
        logf = jnp.where(real, jnp.log(lb + (1.0 - lb) * sig), 0.0)
        k = jnp.where(real, (1.0 - lb) * _sigmoid(-zf), 0.0)
        outs = []
        for h in range(H_A):
            sl = slice(h * DK_A, (h + 1) * DK_A)
            o_h, st_new = _hgrn2_head(q[:, sl], k[:, sl], vi[:, sl], logf[:, sl], st_scr[gi, h], c_sub)
            st_scr[gi, h] = st_new
            outs.append(o_h)
        o = _rms(jnp.concatenate(outs, axis=1)[:ct], gn_ref[...])
        o_ref[gi] = o * _sigmoid(ag_ref[gi])

    @pl.when(t == pl.num_programs(1) - 1)
    def _():
        for gi in range(group):
            for h in range(H_A):
                s_ref[gi, h] = st_scr[gi, h].T


def _hgrn2(x3, n_seq, n_chunks, s0, lb, gnorm, group=1):
    ct = x3.shape[1]
    assert n_seq % group == 0 and (group == 1 or n_chunks == 1)

    def tok_spec(cb):
        return pl.BlockSpec((group, ct, F_A), lambda n, t: (n * n_chunks + t, 0, cb))

    state = pl.BlockSpec((group, H_A, DK_A, DV_A), lambda n, t: (n, 0, 0, 0))
    o, s = pl.pallas_call(
        functools.partial(_hgrn2_kernel, c_sub=SUBLANES, ct=ct),
        grid=(n_seq // group, n_chunks),
        in_specs=[tok_spec(0), tok_spec(1), tok_spec(2), tok_spec(3),
                  pl.BlockSpec((1, F_A), lambda n, t: (0, 0)),
                  pl.BlockSpec((1, W_A), lambda n, t: (0, 0)), state],
        out_specs=[pl.BlockSpec((group, ct, W_A), lambda n, t: (n * n_chunks + t, 0, 0)), state],
        out_shape=[jax.ShapeDtypeStruct((n_seq * n_chunks, ct, W_A), F32),
                   jax.ShapeDtypeStruct((n_seq, H_A, DK_A, DV_A), F32)],
        scratch_shapes=[pltpu.VMEM((group, H_A, DV_A, DK_A), F32)],
        compiler_params=_cparams("parallel", "arbitrary"),
        name="hgrn2",
    )(x3, x3, x3, x3, lb.reshape(1, F_A), gnorm.reshape(1, W_A), s0)
    return o.reshape(n_seq * n_chunks * ct, W_A), s


def _rope_tables(pos):
    half = HD_B // 2
    inv = ROPE_THETA ** (-jnp.arange(half, dtype=F32) / half)
    ang = pos.astype(F32)[:, None] * inv[None, :]
    cos, sin = jnp.cos(ang), jnp.sin(ang)
    return jnp.concatenate([cos, cos], axis=1), jnp.concatenate([-sin, sin], axis=1)


def _rope(x, cos2, sin2, lane):
    swapped = jnp.where((lane & (HD_B - 1)) >= HD_B // 2, pltpu.roll(x, HD_B // 2, 1), pltpu.roll(x, LANES - HD_B // 2, 1))
    return x * cos2 + swapped * sin2


def _dot_nt_f32(a, b):
    a0, a1, a2 = _split3(a)
    b0, b1, b2 = _split3(b)
    return (_dot_nt(a0, b0) + (_dot_nt(a0, b1) + _dot_nt(a1, b0))
            + (_dot_nt(a0, b2) + _dot_nt(a1, b1) + _dot_nt(a2, b0)))


def _top_select(g, index, axis, n_pick):
    sel = jnp.zeros(g.shape, F32)
    big = jnp.int32(1 << 20)
    for _ in range(n_pick):
        m = jnp.max(g, axis=axis, keepdims=True)
        first = jnp.min(jnp.where(g == m, index, big), axis=axis, keepdims=True)
        pick = jnp.where((index == first) & (m > 0.5 * NEG_INF), 1.0, 0.0)
        sel = sel + pick
        g = jnp.where(pick > 0.5, NEG_INF, g)
    return sel


def _store_heads_t(refs, c, x):
    xt = x.T
    for ref in refs:
        ref[2 * c] = xt[:HD_B].astype(ref.dtype)
        ref[2 * c + 1] = xt[HD_B:].astype(ref.dtype)


def _moba_prep_kernel(q_ref, k_ref, v_ref, cos_ref, sin_ref, kt_ref, vl_ref, qa_ref, ka_ref, vt_ref, mrow_scr):
    t = pl.program_id(1)
    rows = q_ref.shape[0]

    @pl.when(t == 0)
    def _():
        mrow_scr[...] = jnp.zeros_like(mrow_scr)

    lane = lax.broadcasted_iota(jnp.int32, (rows, LANES), 1)
    lane1 = lax.broadcasted_iota(jnp.int32, (1, LANES), 1)
    low = lane < HD_B
    blk = lane & (HD_B - 1)
    nbm = HD_B // 2
    brow = lax.broadcasted_iota(jnp.int32, (nbm, rows), 0)
    cos2, sin2 = cos_ref[...], sin_ref[...]
    scale = HD_B ** -0.5 * LOG2E
    for c in range(W_B // LANES):
        sl = slice(c * LANES, (c + 1) * LANES)
        qr = _rope(q_ref[:, sl], cos2, sin2, lane)
        kr = _rope(k_ref[:, sl], cos2, sin2, lane)
        _store_heads_t([kt_ref], c, kr)
        gate_t = _dot_nt_f32(mrow_scr[c], qr)
        halves = []
        for r0 in (0, HD_B):
            g = jnp.where(brow < t, gate_t[r0:r0 + nbm], NEG_INF)
            keep = (_top_select(g, brow, 0, MOBA_TOPK) > 0.5) | (brow == t)
            halves += [jnp.where(keep, 0.0, NEG_INF), jnp.full((HD_B - nbm, rows), NEG_INF, F32)]
        msel = jnp.concatenate(halves, axis=0).T
        own = jnp.where(blk == t, 1.0, 0.0)
        qs = qr * scale
        qa_ref[2 * c] = jnp.where(low, qs, msel).astype(BF16)
        qa_ref[2 * c + 1] = jnp.where(low, msel, qs).astype(BF16)
        ka_ref[2 * c] = jnp.where(low, kr, own).astype(BF16)
        ka_ref[2 * c + 1] = jnp.where(low, own, kr).astype(BF16)
        _store_heads_t([vl_ref, vt_ref], c, v_ref[:, sl])
        mean = jnp.mean(kr, axis=0, keepdims=True)
        mrow_scr[c, pl.ds(HD_B + t, 1), :] = jnp.where(lane1 < HD_B, mean, 0.0)
        mrow_scr[c, pl.ds(t, 1), :] = jnp.where(lane1 < HD_B, 0.0, mean)


def _pair_specs(batch, heads, seq):
    nt = seq // ATT_TILE
    pair = pl.BlockSpec((None, heads, ATT_TILE, LANES), lambda b, t: (b, 0, t, 0))
    pair_shape = jax.ShapeDtypeStruct((batch, heads, seq, LANES), BF16)
    vt = pl.BlockSpec((None, heads, None, LANES // 2, ATT_TILE), lambda b, t: (b, 0, t, 0, 0))
    vt_shape = jax.ShapeDtypeStruct((batch, heads, nt, LANES // 2, ATT_TILE), BF16)
    leaf = pl.BlockSpec((None, heads, LANES // 2, ATT_TILE), lambda b, t: (b, 0, 0, t))
    leaf_shape = jax.ShapeDtypeStruct((batch, heads, LANES // 2, seq), F32)
    return [leaf, leaf, pair, pair, vt], [leaf_shape, leaf_shape, pair_shape, pair_shape, vt_shape]


def _moba_prep(proj, batch, seq, cos2, sin2):
    nb = seq // MOBA_BLOCK
    assert seq % MOBA_BLOCK == 0 and nb <= HD_B // 2 and MOBA_BLOCK == ATT_TILE
    rows = MOBA_BLOCK

    def tok(cb):
        return pl.BlockSpec((rows, W_B), lambda b, t: (b * nb + t, cb))

    specs, shapes = _pair_specs(batch, H_B, seq)
    cb0 = COLS_E_Q // W_B
    return pl.pallas_call(
        _moba_prep_kernel,
        grid=(batch, nb),
        in_specs=[tok(cb0), tok(cb0 + 1), tok(cb0 + 2),
                  pl.BlockSpec((rows, LANES), lambda b, t: (t, 0)),
                  pl.BlockSpec((rows, LANES), lambda b, t: (t, 0))],
        out_specs=specs,
        out_shape=shapes,
        scratch_shapes=[pltpu.VMEM((W_B // LANES, LANES, LANES), F32)],
        compiler_params=_cparams("parallel", "arbitrary"),
        name="moba_prep",
    )(proj, proj, proj, jnp.concatenate([cos2, cos2], axis=1), jnp.concatenate([sin2, sin2], axis=1))


def _flash_kernel(qa_ref, ka_ref, vt_ref, o_ref, s0_scr, s1_scr, acc_scr):
    i = pl.program_id(2)
    tq = qa_ref.shape[1]
    tv = vt_ref.shape[3]
    sub = tq // tv
    qs = [qa_ref[0], qa_ref[1]]

    def produce(j, s_ref):
        for hh in range(2):
            k = ka_ref[hh, pl.ds(pl.multiple_of(j * tq, tq), tq), :]
            s_ref[hh] = _dot_nt(k, qs[hh])

    def fold(x, op):
        return op(x.reshape(tq // SUBLANES, SUBLANES, tq), axis=0)

    def consume(j, s_ref, carry, diag=False):
        new = []
        for hh in range(2):
            m, lacc = carry[hh]
            if diag:
                krow = lax.broadcasted_iota(jnp.int32, (tq, tq), 0)
                qcol = lax.broadcasted_iota(jnp.int32, (tq, tq), 1)
                s_ref[hh] = jnp.where(krow <= qcol, s_ref[hh], NEG_INF)
            m_new = jnp.maximum(m, jnp.max(fold(s_ref[hh], jnp.max), axis=0, keepdims=True))
            alpha = jnp.exp2(m - m_new)
            p = jnp.exp2(s_ref[hh] - m_new)
            lacc = alpha * lacc + fold(p, jnp.sum)
            pb = p.astype(BF16)
            pv = _dot(vt_ref[hh, j * sub], pb[:tv])
            for u in range(1, sub):
                pv = pv + _dot(vt_ref[hh, j * sub + u], pb[u * tv:(u + 1) * tv])
            acc_scr[hh] = alpha * acc_scr[hh] + pv
            new.append((m_new, lacc))
        return tuple(new)

    def pair(u, carry):
        j = 2 * u
        produce(j + 1, s1_scr)
        carry = consume(j, s0_scr, carry)
        produce(j + 2, s0_scr)
        return consume(j + 1, s1_scr, carry)

    def odd_tail(carry):
        produce(i, s1_scr)
        carry = consume(i - 1, s0_scr, carry)
        return consume(i, s1_scr, carry, diag=True)

    def even_tail(carry):
        return consume(i, s0_scr, carry, diag=True)

    acc_scr[...] = jnp.zeros_like(acc_scr)
    produce(0, s0_scr)
    init = tuple((jnp.full((1, tq), -jnp.inf, F32), jnp.zeros((SUBLANES, tq), F32)) for _ in range(2))
    carry = lax.fori_loop(0, i // 2, pair, init)
    carry = lax.cond(i % 2 == 1, odd_tail, even_tail, carry)
    ot = jnp.concatenate([acc_scr[hh] / jnp.sum(carry[hh][1], axis=0, keepdims=True) for hh in range(2)], axis=0)
    o_ref[...] = ot.T


def _flash(qa, ka, vt, tq=512):
    batch, heads, seq, _ = qa.shape
    tq = min(tq, seq)
    nq = seq // tq
    nv, hd, tv = vt.shape[2:]
    return pl.pallas_call(
        _flash_kernel,
        grid=(batch, heads // 2, nq),
        in_specs=[pl.BlockSpec((None, 2, tq, LANES), lambda b, c, i: (b, c, i, 0)),
                  pl.BlockSpec((None, 2, seq, LANES), lambda b, c, i: (b, c, 0, 0)),
                  pl.BlockSpec((None, 2, nv, hd, tv), lambda b, c, i: (b, c, 0, 0, 0))],
        out_specs=pl.BlockSpec((tq, LANES), lambda b, c, i: (b * nq + i, c)),
        out_shape=jax.ShapeDtypeStruct((batch * seq, heads * LANES // 2), F32),
        scratch_shapes=[pltpu.VMEM((2, tq, tq), F32), pltpu.VMEM((2, tq, tq), F32), pltpu.VMEM((2, hd, tq), F32)],
        compiler_params=_cparams("parallel", "parallel", "arbitrary"),
        name="flash",
    )(qa, ka, vt)


def _paged_specs(n_pages, tail):
    zeros = (0,) * len(tail)
    return [pl.BlockSpec((None,) + tail, functools.partial(lambda n, pt, p: (pt[n, p],) + zeros, p=p))
            for p in range(n_pages)]


def _pool_pages(pool):
    n_pool, page, h, dd = pool.shape
    return jnp.transpose(pool, (0, 2, 3, 1)).reshape(n_pool, h * dd, page)


def _pad_rows(x, rows):
    return jnp.concatenate([x, jnp.zeros((rows - x.shape[0], x.shape[1]), x.dtype)], axis=0)


def _query_rows(q, lane, rowh):
    t, w = q.shape
    rep = jnp.broadcast_to(q[:, None, :], (t, H_B, w)).reshape(t * H_B, w)
    return jnp.where((lane // HD_B) == rowh, rep, 0.0)


def _head_rows_to_tokens(o, lane, rowh, t):
    o = jnp.where((lane // HD_B) == rowh, o, 0.0)
    return jnp.sum(o.reshape(t, H_B, o.shape[1]), axis=1)


def _sample_softmax_pv(s_new, s_pages, v_new, vp, o_ref, lane, rowh, t):
    m = jnp.max(s_new, axis=1, keepdims=True)
    mm = s_pages[0]
    for s in s_pages[1:]:
        mm = jnp.maximum(mm, s)
    m = jnp.maximum(m, jnp.max(mm, axis=1, keepdims=True))
    p_new = jnp.exp(s_new - m)
    o = _dot(p_new.astype(BF16), v_new.astype(BF16))
    lsum = None
    for p, s in enumerate(s_pages):
        pp = jnp.exp(s - m)
        lsum = pp if lsum is None else lsum + pp
        o = o + _dot_nt(pp.astype(BF16), vp[p][...].astype(BF16))
    l = jnp.sum(p_new, axis=1, keepdims=True) + jnp.sum(lsum, axis=1, keepdims=True)
    o_ref[...] = _head_rows_to_tokens(o / l, lane, rowh, t)


def _moba_sample_kernel(pt_ref, q_ref, k_ref, v_ref, cos_ref, sin_ref, *rest, n_pages):
    del pt_ref
    kp, vp = rest[:n_pages], rest[n_pages:2 * n_pages]
    ke_ref, o_ref = rest[2 * n_pages:]
    t = q_ref.shape[0]
    nc = t * H_B
    nbp = n_pages * PAGE_SIZE // MOBA_BLOCK
    ppb = MOBA_BLOCK // PAGE_SIZE
    lane8 = lax.broadcasted_iota(jnp.int32, (SUBLANES, LANES), 1)
    cos2, sin2 = _pad_rows(cos_ref[...], SUBLANES), _pad_rows(sin_ref[...], SUBLANES)
    q8, k8, v8 = _pad_rows(q_ref[...], SUBLANES), _pad_rows(k_ref[...], SUBLANES), _pad_rows(v_ref[...], SUBLANES)
    qr = jnp.concatenate([_rope(q8[:, c * LANES:(c + 1) * LANES], cos2, sin2, lane8) for c in range(W_B // LANES)], axis=1)
    kr = jnp.concatenate([_rope(k8[:, c * LANES:(c + 1) * LANES], cos2, sin2, lane8) for c in range(W_B // LANES)], axis=1)
    ke_ref[...] = kr[:t]

    lane = lax.broadcasted_iota(jnp.int32, (nc, W_B), 1)
    rowh = lax.broadcasted_iota(jnp.int32, (nc, W_B), 0) % H_B
    qrows = _query_rows(qr[:t], lane, rowh)

    lanem = lax.broadcasted_iota(jnp.int32, (W_B, LANES), 1)
    means = jnp.zeros((W_B, LANES), F32)
    for j in range(nbp):
        blk_sum = kp[ppb * j][...]
        for u in range(1, ppb):
            blk_sum = blk_sum + kp[ppb * j + u][...]
        means = jnp.where(lanem == j, jnp.sum(blk_sum, axis=1, keepdims=True) * (1.0 / MOBA_BLOCK), means)
    q0, q1, q2 = _split3(qrows)
    m0, m1, m2 = _split3(means)
    gate = _dot(q0, m0) + (_dot(q0, m1) + _dot(q1, m0)) + (_dot(q0, m2) + _dot(q1, m1) + _dot(q2, m0))
    lg = lax.broadcasted_iota(jnp.int32, (nc, LANES), 1)
    sel = _top_select(jnp.where(lg < nbp, gate, NEG_INF), lg, 1, MOBA_TOPK)

    qb = (qrows * (HD_B ** -0.5)).astype(BF16)
    s_pages = []
    for p in range(n_pages):
        s = _dot(qb, kp[p][...].astype(BF16))
        j = p // ppb
        s_pages.append(jnp.where(sel[:, j:j + 1] > 0.5, s, NEG_INF))
    s_new = _dot_nt(qb, kr.astype(BF16))
    ktok = lax.broadcasted_iota(jnp.int32, (nc, SUBLANES), 1)
    qtok = lax.broadcasted_iota(jnp.int32, (nc, SUBLANES), 0) // H_B
    s_new = jnp.where((ktok <= qtok) & (ktok < t), s_new, NEG_INF)
    _sample_softmax_pv(s_new, s_pages, v8, vp, o_ref, lane, rowh, t)


def _moba_sample(q, k, v, pool_k, pool_v, page_table, cos2, sin2):
    n_seq, t, w = q.shape
    n_pages = page_table.shape[1]
    assert (n_pages * PAGE_SIZE) % MOBA_BLOCK == 0
    row = pl.BlockSpec((None, t, w), lambda n, pt: (n, 0, 0))
    tab = pl.BlockSpec((t, LANES), lambda n, pt: (0, 0))
    page = (w, PAGE_SIZE)
    return pl.pallas_call(
        functools.partial(_moba_sample_kernel, n_pages=n_pages),
        grid_spec=pltpu.PrefetchScalarGridSpec(
            num_scalar_prefetch=1,
            grid=(n_seq,),
            in_specs=[row, row, row, tab, tab] + _paged_specs(n_pages, page) + _paged_specs(n_pages, page),
            out_specs=[row, row],
        ),
        out_shape=[jax.ShapeDtypeStruct((n_seq, t, w), F32)] * 2,
        compiler_params=_cparams("arbitrary"),
        name="moba_sample",
    )(page_table, q, k, v, jnp.concatenate([cos2, cos2], axis=1), jnp.concatenate([sin2, sin2], axis=1),
      *([_pool_pages(pool_k)] * n_pages), *([_pool_pages(pool_v)] * n_pages))


def _fox_sample_kernel(pt_ref, q_ref, k_ref, v_ref, df_ref, bf_ref, *rest, n_pages):
    del pt_ref
    kp, vp, lp = rest[:n_pages], rest[n_pages:2 * n_pages], rest[2 * n_pages:3 * n_pages]
    lf_ref, o_ref = rest[3 * n_pages:]
    t = q_ref.shape[0]
    nc = t * H_D
    k8, v8 = _pad_rows(k_ref[...], SUBLANES), _pad_rows(v_ref[...], SUBLANES)
    lane = lax.broadcasted_iota(jnp.int32, (nc, W_D), 1)
    rowh = lax.broadcasted_iota(jnp.int32, (nc, W_D), 0) % H_D
    qb = (_query_rows(q_ref[...], lane, rowh) * (HD_D ** -0.5)).astype(BF16)

    lf_row = _log_sigmoid(df_ref[...] + bf_ref[...])
    lf_ref[...] = lf_row
    rr = lax.broadcasted_iota(jnp.int32, (nc, nc), 0)
    cc = lax.broadcasted_iota(jnp.int32, (nc, nc), 1)
    same = (rr % H_D) == (cc % H_D)
    cn_col = jnp.sum(jnp.where(same & (cc // H_D <= rr // H_D), lf_row, 0.0), axis=1, keepdims=True)
    ktok = lax.broadcasted_iota(jnp.int32, (nc, SUBLANES), 1)
    qtok = lax.broadcasted_iota(jnp.int32, (nc, SUBLANES), 0) // H_D
    cn_keys = jnp.zeros((nc, SUBLANES), F32)
    for tp in range(t):
        col = jnp.sum(jnp.where(same & (cc // H_D <= tp), lf_row, 0.0), axis=1, keepdims=True)
        cn_keys = jnp.where(ktok == tp, col, cn_keys)

    lfa = jnp.concatenate([lp[p][...] for p in range(n_pages)], axis=0)
    nr = lfa.shape[0]
    lane_r = lax.broadcasted_iota(jnp.int32, (nr, LANES), 1)
    incl = lfa
    sh = 1
    while sh < LANES:
        incl = incl + jnp.where(lane_r < LANES - sh, pltpu.roll(incl, LANES - sh, 1), 0.0)
        sh *= 2
    r2 = lax.broadcasted_iota(jnp.int32, (nr, nr), 0)
    c2 = lax.broadcasted_iota(jnp.int32, (nr, nr), 1)
    later = ((c2 > r2) & ((c2 % H_D) == (r2 % H_D))).astype(BF16)
    suf = (incl - lfa) + _dot3(later, jnp.broadcast_to(incl[:, 0:1], (nr, LANES)))

    s_pages = []
    for p in range(n_pages):
        bias = jnp.concatenate([suf[H_D * p:H_D * (p + 1)]] * t, axis=0)
        s_pages.append(_dot(qb, kp[p][...].astype(BF16)) + bias + cn_col)
    s_new = _dot_nt(qb, k8.astype(BF16)) + (cn_col - cn_keys)
    s_new = jnp.where((ktok <= qtok) & (ktok < t), s_new, NEG_INF)
    _sample_softmax_pv(s_new, s_pages, v8, vp, o_ref, lane, rowh, t)


def _fox_sample(q, k, v, df, bf, pool_k, pool_v, pool_lf, page_table):
    n_seq, t, w = q.shape
    nc = t * H_D
    n_pages = page_table.shape[1]
    bft = jnp.tile(bf.astype(F32), t)
    row = pl.BlockSpec((None, t, w), lambda n, pt: (n, 0, 0))
    frow = pl.BlockSpec((None, 1, nc), lambda n, pt: (n, 0, 0))
    page = (w, PAGE_SIZE)
    lf, o = pl.pallas_call(
        functools.partial(_fox_sample_kernel, n_pages=n_pages),
        grid_spec=pltpu.PrefetchScalarGridSpec(
            num_scalar_prefetch=1,
            grid=(n_seq,),
            in_specs=[row, row, row, frow, pl.BlockSpec((1, nc), lambda n, pt: (0, 0))]
            + _paged_specs(n_pages, page) + _paged_specs(n_pages, page) + _paged_specs(n_pages, (H_D, PAGE_SIZE)),
            out_specs=[frow, row],
        ),
        out_shape=[jax.ShapeDtypeStruct((n_seq, 1, nc), F32), jax.ShapeDtypeStruct((n_seq, t, w), F32)],
        compiler_params=_cparams("arbitrary"),
        name="fox_sample",
    )(page_table, q, k, v, df.reshape(n_seq, 1, nc), bft.reshape(1, nc),
      *([_pool_pages(pool_k)] * n_pages), *([_pool_pages(pool_v)] * n_pages),
      *([jnp.transpose(pool_lf, (0, 2, 1))] * n_pages))
    return lf, o


def _shift_rows(x, d, fill):
    return jnp.concatenate([jnp.full((d, x.shape[1]), fill, x.dtype), x[:-d]], axis=0)


def _rglru_gates(xc, wa_ref, ba_ref, wx_ref, bx_ref, sp_ref, first_row_pos0):
    xb = xc.astype(BF16)
    r = _sigmoid(_dot(xb, wa_ref[...]) + ba_ref[...])
    i = _sigmoid(_dot(xb, wx_ref[...]) + bx_ref[...])
    log_a = -RG_C * r * sp_ref[...]
    a = jnp.exp(log_a)
    mult = jnp.sqrt(1.0 - jnp.exp(2.0 * log_a))
    if first_row_pos0 is not None:
        mult = jnp.where(first_row_pos0, 1.0, mult)
    return a, xc * i * mult


def _rglru_kernel(cx_ref, cg_ref, cw_ref, cb_ref, wa_ref, ba_ref, wx_ref, bx_ref, sp_ref,
                  o_ref, buf_ref, h_ref, tail_scr, h_scr):
    t = pl.program_id(1)
    tt = cx_ref.shape[0]

    @pl.when(t == 0)
    def _():
        tail_scr[...] = jnp.zeros_like(tail_scr)
        h_scr[...] = jnp.zeros_like(h_scr)

    cx = cx_ref[...]
    ext = jnp.concatenate([tail_scr[...], cx], axis=0)
    xc = cb_ref[...] + cx * cw_ref[CONV_W - 1:CONV_W, :]
    for d in range(1, CONV_W):
        xc = xc + ext[SUBLANES - d:SUBLANES - d + tt] * cw_ref[CONV_W - 1 - d:CONV_W - d, :]
    row = lax.broadcasted_iota(jnp.int32, (tt, W_C), 0)
    a, b = _rglru_gates(xc, wa_ref, ba_ref, wx_ref, bx_ref, sp_ref, (row == 0) & (t == 0))
    d = 1
    while d < tt:
        b = b + a * _shift_rows(b, d, 0.0)
        a = a * _shift_rows(a, d, 1.0)
        d *= 2
    h = a * h_scr[0:1, :] + b
    o_ref[...] = h * _gelu_tanh(cg_ref[...])
    h_scr[0:1, :] = h[tt - 1:tt, :]
    tail_scr[...] = cx[tt - SUBLANES:tt, :]

    @pl.when(t == pl.num_programs(1) - 1)
    def _():
        buf_ref[...] = cx[tt - (CONV_W - 1):tt, :]
        h_ref[...] = h[tt - 1:tt, :]


def _rglru_prompt(proj, batch, seq, cw, cb, wa, ba, wx, bx, sp, tt=256):
    nt = seq // tt
    vec = pl.BlockSpec((1, W_C), lambda b, t: (0, 0))
    mat = pl.BlockSpec((W_C, W_C), lambda b, t: (0, 0))
    o, buf, h = pl.pallas_call(
        _rglru_kernel,
        grid=(batch, nt),
        in_specs=[pl.BlockSpec((tt, W_C), lambda b, t: (b * nt + t, 0)),
                  pl.BlockSpec((tt, W_C), lambda b, t: (b * nt + t, 1)),
                  pl.BlockSpec((CONV_W, W_C), lambda b, t: (0, 0)), vec, mat, vec, mat, vec, vec],
        out_specs=[pl.BlockSpec((tt, W_C), lambda b, t: (b * nt + t, 0)),
                   pl.BlockSpec((None, CONV_W - 1, W_C), lambda b, t: (b, 0, 0)),
                   pl.BlockSpec((None, 1, W_C), lambda b, t: (b, 0, 0))],
        out_shape=[jax.ShapeDtypeStruct((batch * seq, W_C), F32),
                   jax.ShapeDtypeStruct((batch, CONV_W - 1, W_C), F32),
                   jax.ShapeDtypeStruct((batch, 1, W_C), F32)],
        scratch_shapes=[pltpu.VMEM((SUBLANES, W_C), F32), pltpu.VMEM((SUBLANES, W_C), F32)],
        compiler_params=_cparams("parallel", "arbitrary"),
        name="rglru",
    )(proj, proj, cw, cb, wa, ba, wx, bx, sp)
    return o, buf, h.reshape(batch, W_C)


def _rglru_sample_kernel(cx_ref, cg_ref, buf_ref, h0_ref, cw_ref, cb_ref, wa_ref, ba_ref, wx_ref, bx_ref, sp_ref,
                         o_ref, h_ref, *, pos0_is_zero):
    t, n, _ = cx_ref.shape
    xp = [buf_ref[j] for j in range(CONV_W - 1)] + [cx_ref[j] for j in range(t)]
    xcs = []
    for s in range(t):
        xc = cb_ref[...] + xp[s] * cw_ref[0:1, :]
        for j in range(1, CONV_W):
            xc = xc + xp[s + j] * cw_ref[j:j + 1, :]
        xcs.append(xc)
    xc = jnp.concatenate(xcs, axis=0)
    first = (lax.broadcasted_iota(jnp.int32, xc.shape, 0) < n) if pos0_is_zero else None
    a, b = _rglru_gates(xc, wa_ref, ba_ref, wx_ref, bx_ref, sp_ref, first)
    h = h0_ref[...]
    for s in range(t):
        h = a[s * n:(s + 1) * n] * h + b[s * n:(s + 1) * n]
        o_ref[s] = h * _gelu_tanh(cg_ref[s])
    h_ref[...] = h


def _rglru_sample(cx, cg, buf, h0, pos0_is_zero, cw, cb, wa, ba, wx, bx, sp):
    t, n, _ = cx.shape
    return pl.pallas_call(
        functools.partial(_rglru_sample_kernel, pos0_is_zero=pos0_is_zero),
        out_shape=[jax.ShapeDtypeStruct((t, n, W_C), F32), jax.ShapeDtypeStruct((n, W_C), F32)],
        compiler_params=pltpu.CompilerParams(vmem_limit_bytes=VMEM_LIMIT),
        name="rglru_sample",
    )(cx, cg, buf, h0, cw, cb, wa, ba, wx, bx, sp)


def _odd_weights(cw, cb, wa, ba, wx, bx, lam):
    def bd(w):
        return jax.scipy.linalg.block_diag(*[w[g] for g in range(NB_C)]).astype(BF16)

    def r(v):
        return v.reshape(1, W_C).astype(F32)

    return cw.astype(F32), r(cb), bd(wa), r(ba), bd(wx), r(bx), r(jax.nn.softplus(-lam.astype(F32)))


FOX_ONE_LANE = 3 * H_D


def _fox_selectors():
    selq = np.zeros((H_D // 2, LANES, LANES), np.float32)
    selk = np.zeros((H_D // 2, LANES, LANES), np.float32)
    for cc in range(H_D // 2):
        for h, base in ((2 * cc, HD_D), (2 * cc + 1, 0)):
            for piece in range(3):
                selq[cc, piece * H_D + h, base + piece] = 1.0
                selq[cc, FOX_ONE_LANE, base + 3 + piece] = 1.0
                selk[cc, FOX_ONE_LANE, base + piece] = 1.0
                selk[cc, piece * H_D + h, base + 3 + piece] = -1.0
    return jnp.asarray(selq, BF16), jnp.asarray(selk, BF16)


def _split_features(c, lane):
    hi, mid, lo = _split3(jnp.where(lane < H_D, c, 0.0))
    feat = (hi.astype(F32) + pltpu.roll(mid.astype(F32), H_D, 1) + pltpu.roll(lo.astype(F32), 2 * H_D, 1)
            + jnp.where(lane == FOX_ONE_LANE, 1.0, 0.0))
    return feat.astype(BF16)


def _fox_prep_kernel(q_ref, k_ref, v_ref, df_ref, bf_ref, selq_ref, selk_ref,
                     lf_ref, kt_ref, vl_ref, qa_ref, ka_ref, vt_ref, c_scr):
    t = pl.program_id(1)
    rows = q_ref.shape[0]

    @pl.when(t == 0)
    def _():
        c_scr[...] = jnp.zeros_like(c_scr)

    lane = lax.broadcasted_iota(jnp.int32, (rows, LANES), 1)
    low = lane < HD_D
    logf = jnp.where(lane < H_D, _log_sigmoid(df_ref[...] + bf_ref[...]), 0.0)
    lf_ref[...] = logf.T[:H_D]
    tri =(lax.broadcasted_iota(jnp.int32, (rows, rows), 0) >= lax.broadcasted_iota(jnp.int32, (rows, rows), 1)).astype(BF16)
    c = c_scr[0:1, :] + _dot3(tri, logf)
    c_scr[0:1, :] = c[rows - 1:rows, :]
    feat = _split_features(c * LOG2E, lane)
    scale = HD_D ** -0.5 * LOG2E
    for cc in range(W_D // LANES):
        sl = slice(cc * LANES, (cc + 1) * LANES)
        eq = _dot(feat, selq_ref[cc])
        ek = _dot(feat, selk_ref[cc])
        qs, kc = q_ref[:, sl] * scale, k_ref[:, sl]
        qa_ref[2 * cc] = jnp.where(low, qs, eq).astype(BF16)
        qa_ref[2 * cc + 1] = jnp.where(low, eq, qs).astype(BF16)
        ka_ref[2 * cc] = jnp.where(low, kc, ek).astype(BF16)
        ka_ref[2 * cc + 1] = jnp.where(low, ek, kc).astype(BF16)
        _store_heads_t([kt_ref], cc, kc)
        _store_heads_t([vl_ref, vt_ref], cc, v_ref[:, sl])


def _fox_prep(proj, batch, seq, bf_row, selq, selk):
    rows = ATT_TILE
    nt = seq // rows

    def tok(cb):
        return pl.BlockSpec((rows, W_D), lambda b, t: (b * nt + t, cb))

    specs, shapes = _pair_specs(batch, H_D, seq)
    sel = pl.BlockSpec((H_D // 2, LANES, LANES), lambda b, t: (0, 0, 0))
    cb0 = COLS_O_Q // W_D
    return pl.pallas_call(
        _fox_prep_kernel,
        grid=(batch, nt),
        in_specs=[tok(cb0), tok(cb0 + 1), tok(cb0 + 2),
                  pl.BlockSpec((rows, LANES), lambda b, t: (b * nt + t, COLS_O_F // LANES)),
                  pl.BlockSpec((1, LANES), lambda b, t: (0, 0)), sel, sel],
        out_specs=[pl.BlockSpec((None, H_D, rows), lambda b, t: (b, 0, t))] + specs,
        out_shape=[jax.ShapeDtypeStruct((batch, H_D, seq), F32)] + shapes,
        scratch_shapes=[pltpu.VMEM((SUBLANES, LANES), F32)],
        compiler_params=_cparams("parallel", "arbitrary"),
        name="fox_prep",
    )(proj, proj, proj, proj, bf_row, selq, selk)


def _router_kernel(x_ref, g_ref, r0_ref, r1_ref, r2_ref, o_ref, lg_ref):
    x = x_ref[...]
    y = x * lax.rsqrt(jnp.mean(x * x, axis=-1, keepdims=True) + EPS) * g_ref[...]
    o_ref[...] = y.astype(o_ref.dtype)
    y0, y1, y2 = _split3(y)
    r0, r1, r2 = r0_ref[...], r1_ref[...], r2_ref[...]
    lg_ref[...] = (_dot(y0, r0) + (_dot(y0, r1) + _dot(y1, r0)) + (_dot(y0, r2) + _dot(y1, r1) + _dot(y2, r0)))


def _rmsnorm_router(x, g, router, tm=512):
    m, d = x.shape
    rp = jnp.zeros((d, LANES), F32).at[:, :router.shape[1]].set(router.astype(F32))
    r0 = rp.astype(BF16)
    r1 = (rp - r0.astype(F32)).astype(BF16)
    r2 = (rp - r0.astype(F32) - r1.astype(F32)).astype(BF16)
    rspec = pl.BlockSpec((d, LANES), lambda i: (0, 0))
    return pl.pallas_call(
        _router_kernel,
        grid=(m // tm,),
        in_specs=[pl.BlockSpec((tm, d), lambda i: (i, 0)), pl.BlockSpec((1, d), lambda i: (0, 0)), rspec, rspec, rspec],
        out_specs=[pl.BlockSpec((tm, d), lambda i: (i, 0)), pl.BlockSpec((tm, LANES), lambda i: (i, 0))],
        out_shape=[jax.ShapeDtypeStruct((m, d), F32), jax.ShapeDtypeStruct((m, LANES), F32)],
        compiler_params=_cparams("parallel"),
        name="rmsnorm_router",
    )(x, g.reshape(1, d), r0, r1, r2)


def _combine_norm_kernel(y_ref, ya_ref, yb_ref, gt_ref, g_ref, op_ref, os_ref, *, n_prompt):
    i = pl.program_id(0)
    gt = gt_ref[...]
    out = _rms(y_ref[...] + (gt[:, 0:1] * ya_ref[...] + gt[:, 1:2] * yb_ref[...]), g_ref[...])

    @pl.when(i < n_prompt)
    def _():
        op_ref[...] = out

    @pl.when(i >= n_prompt)
    def _():
        os_ref[...] = out


def _combine_norm(y, ya, yb, gates, g, mp, tm=512):
    m, d = y.shape
    assert mp % tm == 0 and (m - mp) % tm == 0
    n_p = mp // tm
    row = pl.BlockSpec((tm, d), lambda i: (i, 0))
    return pl.pallas_call(
        functools.partial(_combine_norm_kernel, n_prompt=n_p),
        grid=(m // tm,),
        in_specs=[row, row, row, pl.BlockSpec((tm, TOP_K), lambda i: (i, 0)), pl.BlockSpec((1, d), lambda i: (0, 0))],
        out_specs=[pl.BlockSpec((tm, d), lambda i: (jnp.minimum(i, n_p - 1), 0)),
                   pl.BlockSpec((tm, d), lambda i: (jnp.maximum(i - n_p, 0), 0))],
        out_shape=[jax.ShapeDtypeStruct((mp, d), F32), jax.ShapeDtypeStruct((m - mp, d), F32)],
        compiler_params=_cparams("arbitrary"),
        name="combine_norm",
    )(y, ya, yb, gates, g.reshape(1, d))


def _moe_dispatch(logits, tm):
    m, e = logits.shape
    idx = lax.broadcasted_iota(jnp.int32, (m, e), 1)
    m1 = jnp.max(logits, axis=1, keepdims=True)
    i1 = jnp.min(jnp.where(logits == m1, idx, e), axis=1, keepdims=True)
    rest = jnp.where(idx == i1, -jnp.inf, logits)
    m2 = jnp.max(rest, axis=1, keepdims=True)
    i2 = jnp.min(jnp.where(rest == m2, idx, e), axis=1, keepdims=True)
    ex = jnp.exp(m2 - m1)
    gates = jnp.concatenate([1.0 / (1.0 + ex), ex / (1.0 + ex)], axis=1)
    flat_e = jnp.concatenate([i1, i2], axis=1).reshape(m * TOP_K)
    onehot = (flat_e[:, None] == jnp.arange(e, dtype=jnp.int32)[None, :]).astype(jnp.int32)
    csum = jnp.cumsum(onehot, axis=0)
    counts = csum[-1]
    padded = -(-counts // tm) * tm
    ends = jnp.cumsum(padded)
    dest = jnp.sum(onehot * ((ends - padded)[None, :] + csum - 1), axis=1)
    n_tiles = (m * TOP_K) // tm + e
    src_token = jnp.zeros((n_tiles * tm,), jnp.int32).at[dest].set(jnp.arange(m * TOP_K, dtype=jnp.int32) // TOP_K)
    tile_start = jnp.arange(n_tiles, dtype=jnp.int32) * tm
    tile_expert = jnp.minimum(jnp.sum((ends[None, :] <= tile_start[:, None]).astype(jnp.int32), axis=1), e - 1)
    return gates, src_token, dest.reshape(m, TOP_K), tile_expert, (ends[-1:] // tm).astype(jnp.int32)


def kernel(x_prompt, x_sample, cache_k_e, cache_v_e, state_s_e, state_conv_o, state_h_o, cache_k_o, cache_v_o,
           cache_logf_o, page_table, w_in_e, lb_logits, gnorm_a, w_out_e, ffn_w1, ffn_w3, ffn_w2, w_in_o, conv_w,
           conv_b, rg_wa, rg_ba, rg_wx, rg_bx, rg_lambda, fox_bf, w_out_o, moe_router, moe_w1, moe_w3, moe_w2,
           norm_mix, norm_ffn, norm_final):
    batch, seq, d = x_prompt.shape
    n_seq, ts, _ = x_sample.shape
    mp, ms = batch * seq, n_seq * ts
    m = mp + ms
    n_pages = page_table.shape[1]
    past = n_pages * PAGE_SIZE
    chunk_a = 128
    assert d == D_MODEL and ts <= SUBLANES and seq % chunk_a == 0 and m % chunk_a == 0

    x = (x_prompt.reshape(mp, d), x_sample.reshape(ms, d))
    pos_p = jnp.arange(seq, dtype=jnp.int32)
    pos_s = past + jnp.arange(ts, dtype=jnp.int32)

    def seq_rows(a):
        return a.reshape(n_seq, ts, a.shape[1])

    proj = _matmul([x], w_in_e[0].astype(BF16), norm_g=norm_mix[0])
    proj_s = proj[mp:]
    lb = jnp.cumsum(jax.nn.softmax(lb_logits.astype(F32), axis=0), axis=0)[0]
    oa_p, s_p = _hgrn2(proj.reshape(m // chunk_a, chunk_a, proj.shape[1]), batch, seq // chunk_a,
                       jnp.zeros((batch, H_A, DK_A, DV_A), F32), lb, gnorm_a[0])
    oa_s, s_s = _hgrn2(proj_s.reshape(n_seq, ts, proj.shape[1]), n_seq, 1, state_s_e[0], lb, gnorm_a[0],
                       group=8 if n_seq % 8 == 0 else 1)
    cos_p, sin_p = _rope_tables(pos_p)
    cos_s, sin_s = _rope_tables(pos_s)
    ke_p, ve_p, qa, ka, vt = _moba_prep(proj, batch, seq, cos_p, sin_p)
    ob_p = _flash(qa, ka, vt)
    ve_s = proj_s[:, COLS_E_Q + 2 * W_B:COLS_E_Q + 3 * W_B]
    ke_s, ob_s = _moba_sample(seq_rows(proj_s[:, COLS_E_Q:COLS_E_Q + W_B]),
                              seq_rows(proj_s[:, COLS_E_Q + W_B:COLS_E_Q + 2 * W_B]), seq_rows(ve_s),
                              cache_k_e[0], cache_v_e[0], page_table, cos_s, sin_s)
    y = _matmul([(oa_p, oa_s), (ob_p, ob_s.reshape(ms, W_B))], w_out_e[0].astype(BF16), res=x)
    y = _ffn(y, ffn_w1.astype(BF16), ffn_w3.astype(BF16), ffn_w2.astype(BF16), residual=True, norm_g=norm_ffn[0],
             tf=ffn_w1.shape[2] // 2)

    cols_o = w_in_o.shape[2]
    cols_pad = -(-cols_o // LANES) * LANES
    w_in_o_p = jnp.zeros((d, cols_pad), BF16).at[:, :cols_o].set(w_in_o[0].astype(BF16))
    proj_o = _matmul([y], w_in_o_p, norm_g=norm_mix[1])
    proj_os = proj_o[mp:]
    ow = _odd_weights(conv_w[0], conv_b[0], rg_wa[0], rg_ba[0], rg_wx[0], rg_bx[0], rg_lambda[0])
    oc_p, buf_p, h_p = _rglru_prompt(proj_o, batch, seq, *ow)
    ps = proj_os.reshape(n_seq, ts, cols_pad)
    cx_s = ps[..., :W_C]
    oc_s, h_s = _rglru_sample(cx_s.swapaxes(0, 1), ps[..., W_C:2 * W_C].swapaxes(0, 1), state_conv_o[0].swapaxes(0, 1),
                              state_h_o[0], past == 0, *ow)
    oc_s = oc_s.swapaxes(0, 1).reshape(ms, W_C)
    buf_s = jnp.concatenate([state_conv_o[0].astype(F32), cx_s], axis=1)[:, ts:]
    bf_row = jnp.zeros((1, LANES), F32).at[0, :H_D].set(fox_bf[0].astype(F32))
    selq, selk = _fox_selectors()
    lf_p, ko_p, vo_p, qa, ka, vt = _fox_prep(proj_o, batch, seq, bf_row, selq, selk)
    od_p = _flash(qa, ka, vt)
    ko_s = proj_os[:, COLS_O_Q + W_D:COLS_O_Q + 2 * W_D]
    vo_s = proj_os[:, COLS_O_Q + 2 * W_D:COLS_O_Q + 3 * W_D]
    lf_s, od_s = _fox_sample(seq_rows(proj_os[:, COLS_O_Q:COLS_O_Q + W_D]), seq_rows(ko_s), seq_rows(vo_s),
                             proj_os[:, COLS_O_F:COLS_O_F + H_D].reshape(n_seq, ts * H_D), fox_bf[0],
                             cache_k_o[0], cache_v_o[0], cache_logf_o[0], page_table)
    y = _matmul([(oc_p, oc_s), (od_p, od_s.reshape(ms, W_D))], w_out_o[0].astype(BF16), res=y)
    tm_e = 768 if (m * TOP_K) % 768 == 0 else 512
    hn, logits = _rmsnorm_router(y, norm_ffn[1], moe_router[0])
    gates, src_token, pair_pos, tile_expert, n_used = _moe_dispatch(logits[:, :N_EXPERTS], tm_e)
    y_e = _ffn(jnp.take(hn, src_token, axis=0, mode="clip"), moe_w1[0], moe_w3[0], moe_w2[0], tile_expert=tile_expert,
               n_used=n_used, tm=tm_e, tf=512)
    out_p, out_s = _combine_norm(y, jnp.take(y_e, pair_pos[:, 0], axis=0, mode="clip"),
                                 jnp.take(y_e, pair_pos[:, 1], axis=0, mode="clip"), gates, norm_final, mp)

    def heads(a, h):
        return a.reshape(1, n_seq, ts, h, a.shape[-1] // h)

    def heads_t(a):
        return jnp.transpose(a, (0, 3, 1, 2))[None]

    return (out_p.reshape(batch, seq, d), out_s.reshape(n_seq, ts, d),
            s_p[None], s_s[None],
            heads_t(ke_p), heads_t(ve_p), heads(ke_s, H_B), heads(ve_s, H_B),
            buf_p[None], buf_s[None], h_p[None], h_s[None],
            heads_t(ko_p), heads_t(vo_p), jnp.transpose(lf_p, (0, 2, 1))[None],
            heads(ko_s, H_D), heads(vo_s, H_D), lf_s.reshape(1, n_seq, ts, H_D))
```

```python
import functools
import math

import numpy as np
import jax
import jax.numpy as jnp
from jax import lax
from jax.experimental import pallas as pl
from jax.experimental.pallas import tpu as pltpu

F32 = jnp.float32
BF16 = jnp.bfloat16

D_MODEL = 1024
PAGE_SIZE = 128
H_A, DK_A, DV_A = 4, 128, 128
F_A, W_A = H_A * DK_A, H_A * DV_A
H_B, HD_B = 8, 64
W_B = H_B * HD_B
MOBA_BLOCK, MOBA_TOPK = 256, 3
W_C, NB_C, CONV_W, RG_C = 512, 8, 4, 8.0
BW_C = W_C // NB_C
H_D, HD_D = 8, 64
W_D = H_D * HD_D
N_EXPERTS, TOP_K = 8, 2
ROPE_THETA = 10000.0
EPS = 1e-6
NEG_INF = -1e30
LOG2E = math.log2(math.e)

COLS_E_Q = 2 * F_A + 2 * W_A
COLS_O_Q = 2 * W_C
COLS_O_F = 2 * W_C + 3 * W_D

LANES = 128
SUBLANES = 8
VMEM_LIMIT = 56 * 1024 * 1024
ATT_TILE = 256


def _cparams(*sem):
    return pltpu.CompilerParams(dimension_semantics=sem, vmem_limit_bytes=VMEM_LIMIT)


def _split3(x):
    hi = x.astype(BF16)
    r1 = x - hi.astype(F32)
    mid = r1.astype(BF16)
    lo = (r1 - mid.astype(F32)).astype(BF16)
    return hi, mid, lo


def _dot(a, b):
    return jnp.dot(a, b, preferred_element_type=F32)


def _dot_nt(a, b):
    return lax.dot_general(a, b, (((1,), (1,)), ((), ())), preferred_element_type=F32)


def _dot_tn(a, b):
    return lax.dot_general(a, b, (((0,), (0,)), ((), ())), preferred_element_type=F32)


def _dot3(a, b):
    b0, b1, b2 = _split3(b)
    return _dot(a, b0) + _dot(a, b1) + _dot(a, b2)


def _sigmoid(x):
    return 1.0 / (1.0 + jnp.exp(-x))


def _log_sigmoid(x):
    return jnp.minimum(x, 0.0) - jnp.log(1.0 + jnp.exp(-jnp.abs(x)))


def _gelu_tanh(x):
    return 0.5 * x * (1.0 + jnp.tanh(math.sqrt(2.0 / math.pi) * (x + 0.044715 * (x * x * x))))


def _rms(x, g):
    return x * lax.rsqrt(jnp.mean(x * x, axis=-1, keepdims=True) + EPS) * g


def _matmul_kernel(*refs, parts, n_prompt, has_norm, has_res):
    i = pl.program_id(0)
    refs = list(refs)
    g_ref = refs.pop(0) if has_norm else None
    xs = []
    for split in parts:
        if split:
            p_ref, s_ref = refs.pop(0), refs.pop(0)
            xs.append(jnp.where(i < n_prompt, p_ref[...], s_ref[...]))
        else:
            xs.append(refs.pop(0)[...])
    w_ref = refs.pop(0)
    res = None
    if has_res == "split":
        p_ref, s_ref = refs.pop(0), refs.pop(0)
        res = jnp.where(i < n_prompt, p_ref[...], s_ref[...])
    elif has_res:
        res = refs.pop(0)[...]
    o_ref = refs.pop(0)
    acc = None
    k0 = 0
    for x in xs:
        if has_norm:
            x = _rms(x, g_ref[...])
        kk = x.shape[1]
        part = _dot(x.astype(BF16), w_ref[k0:k0 + kk, :])
        acc = part if acc is None else acc + part
        k0 += kk
    if res is not None:
        acc = acc + res
    o_ref[...] = acc


def _matmul(xs, w, res=None, norm_g=None, tm=512):
    kt, n = w.shape
    parts = tuple(isinstance(x, tuple) for x in xs)
    m = sum(a.shape[0] for a in xs[0]) if parts[0] else xs[0].shape[0]
    in_specs, args = [], []
    n_prompt = [0]

    def add(x):
        if isinstance(x, tuple):
            xp, xsm = x
            assert xp.shape[0] % tm == 0 and xsm.shape[0] % tm == 0 and xp.shape[0] + xsm.shape[0] == m
            n_p = n_prompt[0] = xp.shape[0] // tm
            in_specs.append(pl.BlockSpec((tm, xp.shape[1]), lambda i: (jnp.minimum(i, n_p - 1), 0)))
            in_specs.append(pl.BlockSpec((tm, xp.shape[1]), lambda i: (jnp.maximum(i - n_p, 0), 0)))
            args.extend([xp, xsm])
        else:
            in_specs.append(pl.BlockSpec((tm, x.shape[1]), lambda i: (i, 0)))
            args.append(x)

    if norm_g is not None:
        assert len(xs) == 1
        in_specs.append(pl.BlockSpec((1, kt), lambda i: (0, 0)))
        args.append(norm_g.reshape(1, kt).astype(F32))
    for x in xs:
        add(x)
    in_specs.append(pl.BlockSpec((kt, n), lambda i: (0, 0)))
    args.append(w)
    if res is not None:
        add(res)
    has_res = "split" if isinstance(res, tuple) else res is not None
    return pl.pallas_call(
        functools.partial(_matmul_kernel, parts=parts, n_prompt=n_prompt[0], has_norm=norm_g is not None,
                          has_res=has_res),
        grid=(m // tm,),
        in_specs=in_specs,
        out_specs=pl.BlockSpec((tm, n), lambda i: (i, 0)),
        out_shape=jax.ShapeDtypeStruct((m, n), F32),
        compiler_params=_cparams("parallel"),
        name="matmul",
    )(*args)


def _ffn_kernel(te_ref, nu_ref, x_ref, w1_ref, w3_ref, w2_ref, *rest, has_norm, has_res):
    del te_ref
    rest = list(rest)
    g_ref = rest.pop(0) if has_norm else None
    res_ref = x_ref if has_res else None
    o_ref, acc_ref, xb_ref = rest
    i, f = pl.program_id(0), pl.program_id(1)

    @pl.when(i < nu_ref[0])
    def _():
        @pl.when(f == 0)
        def _():
            x = x_ref[...]
            if has_norm:
                x = _rms(x, g_ref[...])
            xb_ref[...] = x.astype(BF16)
            acc_ref[...] = jnp.zeros_like(acc_ref)

        x = xb_ref[...]
        a = _dot(x, w1_ref[...].astype(BF16))
        b = _dot(x, w3_ref[...].astype(BF16))
        g = (a * _sigmoid(a) * b).astype(BF16)
        acc_ref[...] += _dot(g, w2_ref[...].astype(BF16))

        @pl.when(f == pl.num_programs(1) - 1)
        def _():
            out = acc_ref[...]
            if has_res:
                out = out + res_ref[...]
            o_ref[...] = out

    @pl.when((i >= nu_ref[0]) & (f == pl.num_programs(1) - 1))
    def _():
        o_ref[...] = jnp.zeros_like(o_ref)


def _ffn(x, w1, w3, w2, residual=False, norm_g=None, tile_expert=None, n_used=None, tm=512, tf=256):
    m, d = x.shape
    f = w1.shape[-1]
    nf = f // tf
    if tile_expert is None:
        tile_expert = jnp.zeros((m // tm,), jnp.int32)
    if n_used is None:
        n_used = jnp.full((1,), m // tm, jnp.int32)

    def fblk(i, j, nu):
        return jnp.where(i < nu[0], j, nf - 1)

    row = pl.BlockSpec((tm, d), lambda i, j, te, nu: (i, 0))
    in_specs = [
        row,
        pl.BlockSpec((None, d, tf), lambda i, j, te, nu: (te[i], 0, fblk(i, j, nu))),
        pl.BlockSpec((None, d, tf), lambda i, j, te, nu: (te[i], 0, fblk(i, j, nu))),
        pl.BlockSpec((None, tf, d), lambda i, j, te, nu: (te[i], fblk(i, j, nu), 0)),
    ]
    args = [x, w1, w3, w2]
    if norm_g is not None:
        in_specs.append(pl.BlockSpec((1, d), lambda i, j, te, nu: (0, 0)))
        args.append(norm_g.reshape(1, d).astype(F32))
    return pl.pallas_call(
        functools.partial(_ffn_kernel, has_norm=norm_g is not None, has_res=residual),
        grid_spec=pltpu.PrefetchScalarGridSpec(
            num_scalar_prefetch=2,
            grid=(m // tm, nf),
            in_specs=in_specs,
            out_specs=row,
            scratch_shapes=[pltpu.VMEM((tm, d), F32), pltpu.VMEM((tm, d), BF16)],
        ),
        out_shape=jax.ShapeDtypeStruct((m, d), F32),
        compiler_params=_cparams("arbitrary", "arbitrary"),
        name="ffn",
    )(tile_expert, n_used, *args)


def _hgrn2_head(q, k, v, logf, st, c_sub):
    c = q.shape[0]
    ns = c // c_sub
    row = lax.broadcasted_iota(jnp.int32, (c, c), 0)
    col = lax.broadcasted_iota(jnp.int32, (c, c), 1)
    tri = (row >= col).astype(BF16)
    g = _dot3(tri, logf)
    vb = v.astype(BF16)
    o = _dot_nt((q * jnp.exp(g)).astype(BF16), st.astype(BF16))

    lane_c = lax.broadcasted_iota(jnp.int32, (c_sub, c), 1)
    slabs = []
    for i in range(ns):
        r0 = i * c_sub
        qi, ki, gi = q[r0:r0 + c_sub], k[r0:r0 + c_sub], g[r0:r0 + c_sub]
        slab = jnp.zeros((c_sub, c), F32)
        for s in range(c_sub):
            w = jnp.exp(jnp.minimum(gi - gi[s:s + 1, :], 0.0))
            colv = jnp.sum(qi * w * ki[s:s + 1, :], axis=-1, keepdims=True)
            slab = jnp.where(lane_c == r0 + s, colv, slab)
        slabs.append(slab)
    scores = slabs[0] if ns == 1 else jnp.concatenate(slabs, axis=0)
    scores = jnp.where((row >= col) & (row // c_sub == col // c_sub), scores, 0.0)

    w = c // 2
    while w >= c_sub:
        q_parts, k_parts = [], []
        for b in range(c // w):
            r0 = b * w
            if b % 2 == 1:
                q_parts.append(q[r0:r0 + w] * jnp.exp(g[r0:r0 + w] - g[r0 - 1:r0, :]))
                k_parts.append(jnp.zeros((w, q.shape[1]), F32))
            else:
                q_parts.append(jnp.zeros((w, q.shape[1]), F32))
                k_parts.append(k[r0:r0 + w] * jnp.exp(g[r0 + w - 1:r0 + w, :] - g[r0:r0 + w]))
        lvl = _dot_nt(jnp.concatenate(q_parts, axis=0).astype(BF16), jnp.concatenate(k_parts, axis=0).astype(BF16))
        scores = scores + jnp.where(((row // w) % 2 == 1) & (col // w == row // w - 1), lvl, 0.0)
        w //= 2
    o = o + _dot(scores.astype(BF16), vb)
    gl = g[c - 1:c, :]
    ke = (k * jnp.exp(gl - g)).astype(BF16)
    st_new = jnp.exp(gl) * st + _dot_tn(vb, ke)
    return o, st_new


def _hgrn2_kernel(aq_ref, af_ref, ai_ref, ag_ref, lb_ref, gn_ref, s0_ref, o_ref, s_ref, st_scr, *, c_sub, ct):
    t = pl.program_id(1)
    group = aq_ref.shape[0]

    @pl.when(t == 0)
    def _():
        for gi in range(group):
            for h in range(H_A):
                st_scr[gi, h] = s0_ref[gi, h].T

    def padded(x):
        if ct < SUBLANES:
            x = jnp.concatenate([x, jnp.zeros((SUBLANES - ct, x.shape[1]), F32)], axis=0)
        return x

    lb = lb_ref[...]
    for gi in range(group):
        aq, zf, vi = padded(aq_ref[gi]), padded(af_ref[gi]), padded(ai_ref[gi])
        c = aq.shape[0]
        real = lax.broadcasted_iota(jnp.int32, (c, F_A), 0) < ct
        sig = _sigmoid(zf)
        q = aq * _sigmoid(aq)
        logf = jnp.where(real, jnp.log(lb + (1.0 - lb) * sig), 0.0)
        k = jnp.where(real, (1.0 - lb) * _sigmoid(-zf), 0.0)
        outs = []
        for h in range(H_A):
            sl = slice(h * DK_A, (h + 1) * DK_A)
            o_h, st_new = _hgrn2_head(q[:, sl], k[:, sl], vi[:, sl], logf[:, sl], st_scr[gi, h], c_sub)
            st_scr[gi, h] = st_new
            outs.append(o_h)
        o = _rms(jnp.concatenate(outs, axis=1)[:ct], gn_ref[...])
        o_ref[gi] = o * _sigmoid(ag_ref[gi])

    @pl.when(t == pl.num_programs(1) - 1)
    def _():
        for gi in range(group):
            for h in range(H_A):
                s_ref[gi, h] = st_scr[gi, h].T


def _hgrn2(x3, n_seq, n_chunks, s0, lb, gnorm, group=1):
    ct = x3.shape[1]
    assert n_seq % group == 0 and (group == 1 or n_chunks == 1)

    def tok_spec(cb):
        return pl.BlockSpec((group, ct, F_A), lambda n, t: (n * n_chunks + t, 0, cb))

    state = pl.BlockSpec((group, H_A, DK_A, DV_A), lambda n, t: (n, 0, 0, 0))
    o, s = pl.pallas_call(
        functools.partial(_hgrn2_kernel, c_sub=SUBLANES, ct=ct),
        grid=(n_seq // group, n_chunks),
        in_specs=[tok_spec(0), tok_spec(1), tok_spec(2), tok_spec(3),
                  pl.BlockSpec((1, F_A), lambda n, t: (0, 0)),
                  pl.BlockSpec((1, W_A), lambda n, t: (0, 0)), state],
        out_specs=[pl.BlockSpec((group, ct, W_A), lambda n, t: (n * n_chunks + t, 0, 0)), state],
        out_shape=[jax.ShapeDtypeStruct((n_seq * n_chunks, ct, W_A), F32),
                   jax.ShapeDtypeStruct((n_seq, H_A, DK_A, DV_A), F32)],
        scratch_shapes=[pltpu.VMEM((group, H_A, DV_A, DK_A), F32)],
        compiler_params=_cparams("parallel", "arbitrary"),
        name="hgrn2",
    )(x3, x3, x3, x3, lb.reshape(1, F_A), gnorm.reshape(1, W_A), s0)
    return o.reshape(n_seq * n_chunks * ct, W_A), s


def _rope_tables(pos):
    half = HD_B // 2
    inv = ROPE_THETA ** (-jnp.arange(half, dtype=F32) / half)
    ang = pos.astype(F32)[:, None] * inv[None, :]
    cos, sin = jnp.cos(ang), jnp.sin(ang)
    return jnp.concatenate([cos, cos], axis=1), jnp.concatenate([-sin, sin], axis=1)


def _rope(x, cos2, sin2, lane):
    swapped = jnp.where((lane & (HD_B - 1)) >= HD_B // 2, pltpu.roll(x, HD_B // 2, 1), pltpu.roll(x, LANES - HD_B // 2, 1))
    return x * cos2 + swapped * sin2


def _dot_nt_f32(a, b):
    a0, a1, a2 = _split3(a)
    b0, b1, b2 = _split3(b)
    return (_dot_nt(a0, b0) + (_dot_nt(a0, b1) + _dot_nt(a1, b0))
            + (_dot_nt(a0, b2) + _dot_nt(a1, b1) + _dot_nt(a2, b0)))


def _top_select(g, index, axis, n_pick):
    sel = jnp.zeros(g.shape, F32)
    big = jnp.int32(1 << 20)
    for _ in range(n_pick):
        m = jnp.max(g, axis=axis, keepdims=True)
        first = jnp.min(jnp.where(g == m, index, big), axis=axis, keepdims=True)
        pick = jnp.where((index == first) & (m > 0.5 * NEG_INF), 1.0, 0.0)
        sel = sel + pick
        g = jnp.where(pick > 0.5, NEG_INF, g)
    return sel


VT_ROWS = HD_B + 16


def _store_heads_t(refs, c, x):
    xt = x.T
    for ref in refs:
        for hh in range(2):
            part = xt[hh * HD_B:(hh + 1) * HD_B]
            extra = ref.shape[1] - HD_B
            if extra:
                r = lax.broadcasted_iota(jnp.int32, (extra, part.shape[1]), 0)
                part = jnp.concatenate([part, jnp.where(r == 0, 1.0, 0.0)], axis=0)
            ref[2 * c + hh] = part.astype(ref.dtype)


def _moba_prep_kernel(q_ref, k_ref, v_ref, cos_ref, sin_ref, kt_ref, vl_ref, qa_ref, ka_ref, vt_ref, mrow_scr):
    t = pl.program_id(1)
    rows = q_ref.shape[0]

    @pl.when(t == 0)
    def _():
        mrow_scr[...] = jnp.zeros_like(mrow_scr)

    lane = lax.broadcasted_iota(jnp.int32, (rows, LANES), 1)
    lane1 = lax.broadcasted_iota(jnp.int32, (1, LANES), 1)
    low = lane < HD_B
    blk = lane & (HD_B - 1)
    nbm = HD_B // 2
    brow = lax.broadcasted_iota(jnp.int32, (nbm, rows), 0)
    cos2, sin2 = cos_ref[...], sin_ref[...]
    scale = HD_B ** -0.5 * LOG2E
    for c in range(W_B // LANES):
        sl = slice(c * LANES, (c + 1) * LANES)
        qr = _rope(q_ref[:, sl], cos2, sin2, lane)
        kr = _rope(k_ref[:, sl], cos2, sin2, lane)
        _store_heads_t([kt_ref], c, kr)
        gate_t = _dot_nt_f32(mrow_scr[c], qr)
        halves = []
        for r0 in (0, HD_B):
            g = jnp.where(brow < t, gate_t[r0:r0 + nbm], NEG_INF)
            keep = (_top_select(g, brow, 0, MOBA_TOPK) > 0.5) | (brow == t)
            halves += [jnp.where(keep, 0.0, NEG_INF), jnp.full((HD_B - nbm, rows), NEG_INF, F32)]
        msel = jnp.concatenate(halves, axis=0).T
        own = jnp.where(blk == t, 1.0, 0.0)
        qs = qr * scale
        qa_ref[2 * c] = jnp.where(low, qs, msel).astype(BF16)
        qa_ref[2 * c + 1] = jnp.where(low, msel, qs).astype(BF16)
        ka_ref[2 * c] = jnp.where(low, kr, own).astype(BF16)
        ka_ref[2 * c + 1] = jnp.where(low, own, kr).astype(BF16)
        _store_heads_t([vl_ref, vt_ref], c, v_ref[:, sl])
        mean = jnp.mean(kr, axis=0, keepdims=True)
        mrow_scr[c, pl.ds(HD_B + t, 1), :] = jnp.where(lane1 < HD_B, mean, 0.0)
        mrow_scr[c, pl.ds(t, 1), :] = jnp.where(lane1 < HD_B, 0.0, mean)


def _pair_specs(batch, heads, seq):
    nt = seq // ATT_TILE
    pair = pl.BlockSpec((None, heads, ATT_TILE, LANES), lambda b, t: (b, 0, t, 0))
    pair_shape = jax.ShapeDtypeStruct((batch, heads, seq, LANES), BF16)
    vt = pl.BlockSpec((None, heads, None, VT_ROWS, ATT_TILE), lambda b, t: (b, 0, t, 0, 0))
    vt_shape = jax.ShapeDtypeStruct((batch, heads, nt, VT_ROWS, ATT_TILE), BF16)
    leaf = pl.BlockSpec((None, heads, LANES // 2, ATT_TILE), lambda b, t: (b, 0, 0, t))
    leaf_shape = jax.ShapeDtypeStruct((batch, heads, LANES // 2, seq), F32)
    return [leaf, leaf, pair, pair, vt], [leaf_shape, leaf_shape, pair_shape, pair_shape, vt_shape]


def _moba_prep(proj, batch, seq, cos2, sin2):
    nb = seq // MOBA_BLOCK
    assert seq % MOBA_BLOCK == 0 and nb <= HD_B // 2 and MOBA_BLOCK == ATT_TILE
    rows = MOBA_BLOCK

    def tok(cb):
        return pl.BlockSpec((rows, W_B), lambda b, t: (b * nb + t, cb))

    specs, shapes = _pair_specs(batch, H_B, seq)
    cb0 = COLS_E_Q // W_B
    return pl.pallas_call(
        _moba_prep_kernel,
        grid=(batch, nb),
        in_specs=[tok(cb0), tok(cb0 + 1), tok(cb0 + 2),
                  pl.BlockSpec((rows, LANES), lambda b, t: (t, 0)),
                  pl.BlockSpec((rows, LANES), lambda b, t: (t, 0))],
        out_specs=specs,
        out_shape=shapes,
        scratch_shapes=[pltpu.VMEM((W_B // LANES, LANES, LANES), F32)],
        compiler_params=_cparams("parallel", "arbitrary"),
        name="moba_prep",
    )(proj, proj, proj, jnp.concatenate([cos2, cos2], axis=1), jnp.concatenate([sin2, sin2], axis=1))


def _flash_kernel(qa_ref, ka_ref, vt_ref, o_ref, s0_scr, s1_scr, acc_scr):
    i = pl.program_id(2)
    tq = qa_ref.shape[1]
    tv = vt_ref.shape[3]
    sub = tq // tv
    qs = [qa_ref[0], qa_ref[1]]

    def produce(j, s_ref):
        for hh in range(2):
            k = ka_ref[hh, pl.ds(pl.multiple_of(j * tq, tq), tq), :]
            s_ref[hh] = _dot_nt(k, qs[hh])

    def fold(x, op):
        return op(x.reshape(tq // SUBLANES, SUBLANES, tq), axis=0)

    def consume(j, s_ref, carry, diag=False):
        new = []
        for hh in range(2):
            m = carry[hh]
            if diag:
                krow = lax.broadcasted_iota(jnp.int32, (tq, tq), 0)
                qcol = lax.broadcasted_iota(jnp.int32, (tq, tq), 1)
                s_ref[hh] = jnp.where(krow <= qcol, s_ref[hh], NEG_INF)
            m_new = jnp.maximum(m, jnp.max(fold(s_ref[hh], jnp.max), axis=0, keepdims=True))
            pb = jnp.exp2(s_ref[hh] - m_new).astype(BF16)
            pv = _dot(vt_ref[hh, j * sub], pb[:tv])
            for u in range(1, sub):
                pv = pv + _dot(vt_ref[hh, j * sub + u], pb[u * tv:(u + 1) * tv])
            acc_scr[hh] = jnp.exp2(m - m_new) * acc_scr[hh] + pv
            new.append(m_new)
        return tuple(new)

    def pair(u, carry):
        j = 2 * u
        produce(j + 1, s1_scr)
        carry = consume(j, s0_scr, carry)
        produce(j + 2, s0_scr)
        return consume(j + 1, s1_scr, carry)

    def odd_tail(carry):
        produce(i, s1_scr)
        carry = consume(i - 1, s0_scr, carry)
        return consume(i, s1_scr, carry, diag=True)

    def even_tail(carry):
        return consume(i, s0_scr, carry, diag=True)

    acc_scr[...] = jnp.zeros_like(acc_scr)
    produce(0, s0_scr)
    init = tuple(jnp.full((1, tq), -jnp.inf, F32) for _ in range(2))
    carry = lax.fori_loop(0, i // 2, pair, init)
    lax.cond(i % 2 == 1, odd_tail, even_tail, carry)
    ot = jnp.concatenate([acc_scr[hh, :HD_B] / acc_scr[hh, HD_B:HD_B + 1] for hh in range(2)], axis=0)
    o_ref[...] = ot.T


def _flash(qa, ka, vt, tq=512):
    batch, heads, seq, _ = qa.shape
    tq = min(tq, seq)
    nq = seq // tq
    nv, hd, tv = vt.shape[2:]
    return pl.pallas_call(
        _flash_kernel,
        grid=(batch, heads // 2, nq),
        in_specs=[pl.BlockSpec((None, 2, tq, LANES), lambda b, c, i: (b, c, i, 0)),
                  pl.BlockSpec((None, 2, seq, LANES), lambda b, c, i: (b, c, 0, 0)),
                  pl.BlockSpec((None, 2, nv, hd, tv), lambda b, c, i: (b, c, 0, 0, 0))],
        out_specs=pl.BlockSpec((tq, LANES), lambda b, c, i: (b * nq + i, c)),
        out_shape=jax.ShapeDtypeStruct((batch * seq, heads * LANES // 2), F32),
        scratch_shapes=[pltpu.VMEM((2, tq, tq), F32), pltpu.VMEM((2, tq, tq), F32), pltpu.VMEM((2, hd, tq), F32)],
        compiler_params=_cparams("parallel", "parallel", "arbitrary"),
        name="flash",
    )(qa, ka, vt)


SAMPLE_GROUP = 2


def _paged_specs(n_pages, tail, group):
    zeros = (0,) * len(tail)
    return [pl.BlockSpec((None,) + tail, functools.partial(lambda n, pt, g, p: (pt[n * group + g, p],) + zeros, g=g, p=p))
            for g in range(group) for p in range(n_pages)]


def _group_size(n_seq):
    return SAMPLE_GROUP if n_seq % SAMPLE_GROUP == 0 else 1


def _pool_pages(pool):
    n_pool, page, h, dd = pool.shape
    return jnp.transpose(pool, (0, 2, 3, 1)).reshape(n_pool, h * dd, page)


def _pad_rows(x, rows):
    return jnp.concatenate([x, jnp.zeros((rows - x.shape[0], x.shape[1]), x.dtype)], axis=0)


def _query_rows(q, lane, rowh):
    t, w = q.shape
    rep = jnp.broadcast_to(q[:, None, :], (t, H_B, w)).reshape(t * H_B, w)
    return jnp.where((lane // HD_B) == rowh, rep, 0.0)


def _head_rows_to_tokens(o, lane, rowh, t):
    o = jnp.where((lane // HD_B) == rowh, o, 0.0)
    return jnp.sum(o.reshape(t, H_B, o.shape[1]), axis=1)


def _sample_softmax_pv(s_new, s_pages, v_new, vp, o_ref, lane, rowh, t):
    m = jnp.max(s_new, axis=1, keepdims=True)
    mm = s_pages[0]
    for s in s_pages[1:]:
        mm = jnp.maximum(mm, s)
    m = jnp.maximum(m, jnp.max(mm, axis=1, keepdims=True))
    p_new = jnp.exp(s_new - m)
    o = _dot(p_new.astype(BF16), v_new.astype(BF16))
    lsum = None
    for p, s in enumerate(s_pages):
        pp = jnp.exp(s - m)
        lsum = pp if lsum is None else lsum + pp
        o = o + _dot_nt(pp.astype(BF16), vp[p][...].astype(BF16))
    l = jnp.sum(p_new, axis=1, keepdims=True) + jnp.sum(lsum, axis=1, keepdims=True)
    o_ref[...] = _head_rows_to_tokens(o / l, lane, rowh, t)


def _moba_sample_kernel(pt_ref, q_ref, k_ref, v_ref, cos_ref, sin_ref, *rest, n_pages):
    del pt_ref
    group = q_ref.shape[0]
    ke_ref, o_ref = rest[2 * group * n_pages:]
    for g in range(group):
        kp = rest[g * n_pages:(g + 1) * n_pages]
        vp = rest[(group + g) * n_pages:(group + g + 1) * n_pages]
        _moba_sample_one(q_ref.at[g], k_ref.at[g], v_ref.at[g], cos_ref, sin_ref, kp, vp, ke_ref.at[g], o_ref.at[g], n_pages)


def _moba_sample_one(q_ref, k_ref, v_ref, cos_ref, sin_ref, kp, vp, ke_ref, o_ref, n_pages):
    t = q_ref.shape[0]
    nc = t * H_B
    nbp = n_pages * PAGE_SIZE // MOBA_BLOCK
    ppb = MOBA_BLOCK // PAGE_SIZE
    lane8 = lax.broadcasted_iota(jnp.int32, (SUBLANES, LANES), 1)
    cos2, sin2 = _pad_rows(cos_ref[...], SUBLANES), _pad_rows(sin_ref[...], SUBLANES)
    q8, k8, v8 = _pad_rows(q_ref[...], SUBLANES), _pad_rows(k_ref[...], SUBLANES), _pad_rows(v_ref[...], SUBLANES)
    qr = jnp.concatenate([_rope(q8[:, c * LANES:(c + 1) * LANES], cos2, sin2, lane8) for c in range(W_B // LANES)], axis=1)
    kr = jnp.concatenate([_rope(k8[:, c * LANES:(c + 1) * LANES], cos2, sin2, lane8) for c in range(W_B // LANES)], axis=1)
    ke_ref[...] = kr[:t]

    lane = lax.broadcasted_iota(jnp.int32, (nc, W_B), 1)
    rowh = lax.broadcasted_iota(jnp.int32, (nc, W_B), 0) % H_B
    qrows = _query_rows(qr[:t], lane, rowh)

    lanem = lax.broadcasted_iota(jnp.int32, (W_B, LANES), 1)
    means = jnp.zeros((W_B, LANES), F32)
    for j in range(nbp):
        blk_sum = kp[ppb * j][...]
        for u in range(1, ppb):
            blk_sum = blk_sum + kp[ppb * j + u][...]
        means = jnp.where(lanem == j, jnp.sum(blk_sum, axis=1, keepdims=True) * (1.0 / MOBA_BLOCK), means)
    q0, q1, q2 = _split3(qrows)
    m0, m1, m2 = _split3(means)
    gate = _dot(q0, m0) + (_dot(q0, m1) + _dot(q1, m0)) + (_dot(q0, m2) + _dot(q1, m1) + _dot(q2, m0))
    lg = lax.broadcasted_iota(jnp.int32, (nc, LANES), 1)
    sel = _top_select(jnp.where(lg < nbp, gate, NEG_INF), lg, 1, MOBA_TOPK)

    qb = (qrows * (HD_B ** -0.5)).astype(BF16)
    s_pages = []
    for p in range(n_pages):
        s = _dot(qb, kp[p][...].astype(BF16))
        j = p // ppb
        s_pages.append(jnp.where(sel[:, j:j + 1] > 0.5, s, NEG_INF))
    s_new = _dot_nt(qb, kr.astype(BF16))
    ktok = lax.broadcasted_iota(jnp.int32, (nc, SUBLANES), 1)
    qtok = lax.broadcasted_iota(jnp.int32, (nc, SUBLANES), 0) // H_B
    s_new = jnp.where((ktok <= qtok) & (ktok < t), s_new, NEG_INF)
    _sample_softmax_pv(s_new, s_pages, v8, vp, o_ref, lane, rowh, t)


def _moba_sample(q, k, v, pool_k, pool_v, page_table, cos2, sin2):
    n_seq, t, w = q.shape
    n_pages = page_table.shape[1]
    assert (n_pages * PAGE_SIZE) % MOBA_BLOCK == 0
    group = _group_size(n_seq)
    row = pl.BlockSpec((group, t, w), lambda n, pt: (n, 0, 0))
    tab = pl.BlockSpec((t, LANES), lambda n, pt: (0, 0))
    page = (w, PAGE_SIZE)
    return pl.pallas_call(
        functools.partial(_moba_sample_kernel, n_pages=n_pages),
        grid_spec=pltpu.PrefetchScalarGridSpec(
            num_scalar_prefetch=1,
            grid=(n_seq // group,),
            in_specs=[row, row, row, tab, tab] + _paged_specs(n_pages, page, group) + _paged_specs(n_pages, page, group),
            out_specs=[row, row],
        ),
        out_shape=[jax.ShapeDtypeStruct((n_seq, t, w), F32)] * 2,
        compiler_params=_cparams("arbitrary"),
        name="moba_sample",
    )(page_table, q, k, v, jnp.concatenate([cos2, cos2], axis=1), jnp.concatenate([sin2, sin2], axis=1),
      *([_pool_pages(pool_k)] * (group * n_pages)), *([_pool_pages(pool_v)] * (group * n_pages)))


def _fox_sample_kernel(pt_ref, q_ref, k_ref, v_ref, df_ref, bf_ref, *rest, n_pages):
    del pt_ref
    group = q_ref.shape[0]
    lf_ref, o_ref = rest[3 * group * n_pages:]
    for g in range(group):
        kp, vp, lp = (rest[(kind * group + g) * n_pages:(kind * group + g + 1) * n_pages] for kind in range(3))
        _fox_sample_one(q_ref.at[g], k_ref.at[g], v_ref.at[g], df_ref.at[g], bf_ref, kp, vp, lp, lf_ref.at[g], o_ref.at[g],
                        n_pages)


def _fox_sample_one(q_ref, k_ref, v_ref, df_ref, bf_ref, kp, vp, lp, lf_ref, o_ref, n_pages):
    t = q_ref.shape[0]
    nc = t * H_D
    k8, v8 = _pad_rows(k_ref[...], SUBLANES), _pad_rows(v_ref[...], SUBLANES)
    lane = lax.broadcasted_iota(jnp.int32, (nc, W_D), 1)
    rowh = lax.broadcasted_iota(jnp.int32, (nc, W_D), 0) % H_D
    qb = (_query_rows(q_ref[...], lane, rowh) * (HD_D ** -0.5)).astype(BF16)

    lf_row = _log_sigmoid(df_ref[...] + bf_ref[...])
    lf_ref[...] = lf_row
    rr = lax.broadcasted_iota(jnp.int32, (nc, nc), 0)
    cc = lax.broadcasted_iota(jnp.int32, (nc, nc), 1)
    same = (rr % H_D) == (cc % H_D)
    cn_col = jnp.sum(jnp.where(same & (cc // H_D <= rr // H_D), lf_row, 0.0), axis=1, keepdims=True)
    ktok = lax.broadcasted_iota(jnp.int32, (nc, SUBLANES), 1)
    qtok = lax.broadcasted_iota(jnp.int32, (nc, SUBLANES), 0) // H_D
    cn_keys = jnp.zeros((nc, SUBLANES), F32)
    for tp in range(t):
        col = jnp.sum(jnp.where(same & (cc // H_D <= tp), lf_row, 0.0), axis=1, keepdims=True)
        cn_keys = jnp.where(ktok == tp, col, cn_keys)

    lfa = jnp.concatenate([lp[p][...] for p in range(n_pages)], axis=0)
    nr = lfa.shape[0]
    lane_r = lax.broadcasted_iota(jnp.int32, (nr, LANES), 1)
    incl = lfa
    sh = 1
    while sh < LANES:
        incl = incl + jnp.where(lane_r < LANES - sh, pltpu.roll(incl, LANES - sh, 1), 0.0)
        sh *= 2
    r2 = lax.broadcasted_iota(jnp.int32, (nr, nr), 0)
    c2 = lax.broadcasted_iota(jnp.int32, (nr, nr), 1)
    later = ((c2 > r2) & ((c2 % H_D) == (r2 % H_D))).astype(BF16)
    suf = (incl - lfa) + _dot3(later, jnp.broadcast_to(incl[:, 0:1], (nr, LANES)))

    s_pages = []
    for p in range(n_pages):
        bias = jnp.concatenate([suf[H_D * p:H_D * (p + 1)]] * t, axis=0)
        s_pages.append(_dot(qb, kp[p][...].astype(BF16)) + bias + cn_col)
    s_new = _dot_nt(qb, k8.astype(BF16)) + (cn_col - cn_keys)
    s_new = jnp.where((ktok <= qtok) & (ktok < t), s_new, NEG_INF)
    _sample_softmax_pv(s_new, s_pages, v8, vp, o_ref, lane, rowh, t)


def _fox_sample(q, k, v, df, bf, pool_k, pool_v, pool_lf, page_table):
    n_seq, t, w = q.shape
    nc = t * H_D
    n_pages = page_table.shape[1]
    bft = jnp.tile(bf.astype(F32), t)
    group = _group_size(n_seq)
    row = pl.BlockSpec((group, t, w), lambda n, pt: (n, 0, 0))
    frow = pl.BlockSpec((group, 1, nc), lambda n, pt: (n, 0, 0))
    page = (w, PAGE_SIZE)
    lf, o = pl.pallas_call(
        functools.partial(_fox_sample_kernel, n_pages=n_pages),
        grid_spec=pltpu.PrefetchScalarGridSpec(
            num_scalar_prefetch=1,
            grid=(n_seq // group,),
            in_specs=[row, row, row, frow, pl.BlockSpec((1, nc), lambda n, pt: (0, 0))]
            + _paged_specs(n_pages, page, group) + _paged_specs(n_pages, page, group)
            + _paged_specs(n_pages, (H_D, PAGE_SIZE), group),
            out_specs=[frow, row],
        ),
        out_shape=[jax.ShapeDtypeStruct((n_seq, 1, nc), F32), jax.ShapeDtypeStruct((n_seq, t, w), F32)],
        compiler_params=_cparams("arbitrary"),
        name="fox_sample",
    )(page_table, q, k, v, df.reshape(n_seq, 1, nc), bft.reshape(1, nc),
      *([_pool_pages(pool_k)] * (group * n_pages)), *([_pool_pages(pool_v)] * (group * n_pages)),
      *([jnp.transpose(pool_lf, (0, 2, 1))] * (group * n_pages)))
    return lf, o


def _shift_rows(x, d, fill):
    return jnp.concatenate([jnp.full((d, x.shape[1]), fill, x.dtype), x[:-d]], axis=0)


def _rglru_gates(xc, wa_ref, ba_ref, wx_ref, bx_ref, sp_ref, first_row_pos0):
    xb = xc.astype(BF16)
    r = _sigmoid(_dot(xb, wa_ref[...]) + ba_ref[...])
    i = _sigmoid(_dot(xb, wx_ref[...]) + bx_ref[...])
    log_a = -RG_C * r * sp_ref[...]
    a = jnp.exp(log_a)
    mult = jnp.sqrt(1.0 - jnp.exp(2.0 * log_a))
    if first_row_pos0 is not None:
        mult = jnp.where(first_row_pos0, 1.0, mult)
    return a, xc * i * mult


def _rglru_kernel(cx_ref, cg_ref, cw_ref, cb_ref, wa_ref, ba_ref, wx_ref, bx_ref, sp_ref,
                  o_ref, buf_ref, h_ref, tail_scr, h_scr):
    t = pl.program_id(1)
    tt = cx_ref.shape[0]

    @pl.when(t == 0)
    def _():
        tail_scr[...] = jnp.zeros_like(tail_scr)
        h_scr[...] = jnp.zeros_like(h_scr)

    cx = cx_ref[...]
    ext = jnp.concatenate([tail_scr[...], cx], axis=0)
    xc = cb_ref[...] + cx * cw_ref[CONV_W - 1:CONV_W, :]
    for d in range(1, CONV_W):
        xc = xc + ext[SUBLANES - d:SUBLANES - d + tt] * cw_ref[CONV_W - 1 - d:CONV_W - d, :]
    row = lax.broadcasted_iota(jnp.int32, (tt, W_C), 0)
    a, b = _rglru_gates(xc, wa_ref, ba_ref, wx_ref, bx_ref, sp_ref, (row == 0) & (t == 0))
    d = 1
    while d < tt:
        b = b + a * _shift_rows(b, d, 0.0)
        a = a * _shift_rows(a, d, 1.0)
        d *= 2
    h = a * h_scr[0:1, :] + b
    o_ref[...] = h * _gelu_tanh(cg_ref[...])
    h_scr[0:1, :] = h[tt - 1:tt, :]
    tail_scr[...] = cx[tt - SUBLANES:tt, :]

    @pl.when(t == pl.num_programs(1) - 1)
    def _():
        buf_ref[...] = cx[tt - (CONV_W - 1):tt, :]
        h_ref[...] = h[tt - 1:tt, :]


def _rglru_prompt(proj, batch, seq, cw, cb, wa, ba, wx, bx, sp, tt=256):
    nt = seq // tt
    vec = pl.BlockSpec((1, W_C), lambda b, t: (0, 0))
    mat = pl.BlockSpec((W_C, W_C), lambda b, t: (0, 0))
    o, buf, h = pl.pallas_call(
        _rglru_kernel,
        grid=(batch, nt),
        in_specs=[pl.BlockSpec((tt, W_C), lambda b, t: (b * nt + t, 0)),
                  pl.BlockSpec((tt, W_C), lambda b, t: (b * nt + t, 1)),
                  pl.BlockSpec((CONV_W, W_C), lambda b, t: (0, 0)), vec, mat, vec, mat, vec, vec],
        out_specs=[pl.BlockSpec((tt, W_C), lambda b, t: (b * nt + t, 0)),
                   pl.BlockSpec((None, CONV_W - 1, W_C), lambda b, t: (b, 0, 0)),
                   pl.BlockSpec((None, 1, W_C), lambda b, t: (b, 0, 0))],
        out_shape=[jax.ShapeDtypeStruct((batch * seq, W_C), F32),
                   jax.ShapeDtypeStruct((batch, CONV_W - 1, W_C), F32),
                   jax.ShapeDtypeStruct((batch, 1, W_C), F32)],
        scratch_shapes=[pltpu.VMEM((SUBLANES, W_C), F32), pltpu.VMEM((SUBLANES, W_C), F32)],
        compiler_params=_cparams("parallel", "arbitrary"),
        name="rglru",
    )(proj, proj, cw, cb, wa, ba, wx, bx, sp)
    return o, buf, h.reshape(batch, W_C)


def _rglru_sample_kernel(cx_ref, cg_ref, buf_ref, h0_ref, cw_ref, cb_ref, wa_ref, ba_ref, wx_ref, bx_ref, sp_ref,
                         o_ref, h_ref, *, pos0_is_zero):
    t, n, _ = cx_ref.shape
    xp = [buf_ref[j] for j in range(CONV_W - 1)] + [cx_ref[j] for j in range(t)]
    xcs = []
    for s in range(t):
        xc = cb_ref[...] + xp[s] * cw_ref[0:1, :]
        for j in range(1, CONV_W):
            xc = xc + xp[s + j] * cw_ref[j:j + 1, :]
        xcs.append(xc)
    xc = jnp.concatenate(xcs, axis=0)
    first = (lax.broadcasted_iota(jnp.int32, xc.shape, 0) < n) if pos0_is_zero else None
    a, b = _rglru_gates(xc, wa_ref, ba_ref, wx_ref, bx_ref, sp_ref, first)
    h = h0_ref[...]
    for s in range(t):
        h = a[s * n:(s + 1) * n] * h + b[s * n:(s + 1) * n]
        o_ref[s] = h * _gelu_tanh(cg_ref[s])
    h_ref[...] = h


def _rglru_sample(cx, cg, buf, h0, pos0_is_zero, cw, cb, wa, ba, wx, bx, sp):
    t, n, _ = cx.shape
    return pl.pallas_call(
        functools.partial(_rglru_sample_kernel, pos0_is_zero=pos0_is_zero),
        out_shape=[jax.ShapeDtypeStruct((t, n, W_C), F32), jax.ShapeDtypeStruct((n, W_C), F32)],
        compiler_params=pltpu.CompilerParams(vmem_limit_bytes=VMEM_LIMIT),
        name="rglru_sample",
    )(cx, cg, buf, h0, cw, cb, wa, ba, wx, bx, sp)


def _odd_weights(cw, cb, wa, ba, wx, bx, lam):
    def bd(w):
        return jax.scipy.linalg.block_diag(*[w[g] for g in range(NB_C)]).astype(BF16)

    def r(v):
        return v.reshape(1, W_C).astype(F32)

    return cw.astype(F32), r(cb), bd(wa), r(ba), bd(wx), r(bx), r(jax.nn.softplus(-lam.astype(F32)))


FOX_ONE_LANE = 3 * H_D


def _fox_selectors():
    selq = np.zeros((H_D // 2, LANES, LANES), np.float32)
    selk = np.zeros((H_D // 2, LANES, LANES), np.float32)
    for cc in range(H_D // 2):
        for h, base in ((2 * cc, HD_D), (2 * cc + 1, 0)):
            for piece in range(3):
                selq[cc, piece * H_D + h, base + piece] = 1.0
                selq[cc, FOX_ONE_LANE, base + 3 + piece] = 1.0
                selk[cc, FOX_ONE_LANE, base + piece] = 1.0
                selk[cc, piece * H_D + h, base + 3 + piece] = -1.0
    return jnp.asarray(selq, BF16), jnp.asarray(selk, BF16)


def _split_features(c, lane):
    hi, mid, lo = _split3(jnp.where(lane < H_D, c, 0.0))
    feat = (hi.astype(F32) + pltpu.roll(mid.astype(F32), H_D, 1) + pltpu.roll(lo.astype(F32), 2 * H_D, 1)
            + jnp.where(lane == FOX_ONE_LANE, 1.0, 0.0))
    return feat.astype(BF16)


def _fox_prep_kernel(q_ref, k_ref, v_ref, df_ref, bf_ref, selq_ref, selk_ref,
                     lf_ref, kt_ref, vl_ref, qa_ref, ka_ref, vt_ref, c_scr):
    t = pl.program_id(1)
    rows = q_ref.shape[0]

    @pl.when(t == 0)
    def _():
        c_scr[...] = jnp.zeros_like(c_scr)

    lane = lax.broadcasted_iota(jnp.int32, (rows, LANES), 1)
    low = lane < HD_D
    logf = jnp.where(lane < H_D, _log_sigmoid(df_ref[...] + bf_ref[...]), 0.0)
    lf_ref[...] = logf.T[:H_D]
    tri =(lax.broadcasted_iota(jnp.int32, (rows, rows), 0) >= lax.broadcasted_iota(jnp.int32, (rows, rows), 1)).astype(BF16)
    c = c_scr[0:1, :] + _dot3(tri, logf)
    c_scr[0:1, :] = c[rows - 1:rows, :]
    feat = _split_features(c * LOG2E, lane)
    scale = HD_D ** -0.5 * LOG2E
    for cc in range(W_D // LANES):
        sl = slice(cc * LANES, (cc + 1) * LANES)
        eq = _dot(feat, selq_ref[cc])
        ek = _dot(feat, selk_ref[cc])
        qs, kc = q_ref[:, sl] * scale, k_ref[:, sl]
        qa_ref[2 * cc] = jnp.where(low, qs, eq).astype(BF16)
        qa_ref[2 * cc + 1] = jnp.where(low, eq, qs).astype(BF16)
        ka_ref[2 * cc] = jnp.where(low, kc, ek).astype(BF16)
        ka_ref[2 * cc + 1] = jnp.where(low, ek, kc).astype(BF16)
        _store_heads_t([kt_ref], cc, kc)
        _store_heads_t([vl_ref, vt_ref], cc, v_ref[:, sl])


def _fox_prep(proj, batch, seq, bf_row, selq, selk):
    rows = ATT_TILE
    nt = seq // rows

    def tok(cb):
        return pl.BlockSpec((rows, W_D), lambda b, t: (b * nt + t, cb))

    specs, shapes = _pair_specs(batch, H_D, seq)
    sel = pl.BlockSpec((H_D // 2, LANES, LANES), lambda b, t: (0, 0, 0))
    cb0 = COLS_O_Q // W_D
    return pl.pallas_call(
        _fox_prep_kernel,
        grid=(batch, nt),
        in_specs=[tok(cb0), tok(cb0 + 1), tok(cb0 + 2),
                  pl.BlockSpec((rows, LANES), lambda b, t: (b * nt + t, COLS_O_F // LANES)),
                  pl.BlockSpec((1, LANES), lambda b, t: (0, 0)), sel, sel],
        out_specs=[pl.BlockSpec((None, H_D, rows), lambda b, t: (b, 0, t))] + specs,
        out_shape=[jax.ShapeDtypeStruct((batch, H_D, seq), F32)] + shapes,
        scratch_shapes=[pltpu.VMEM((SUBLANES, LANES), F32)],
        compiler_params=_cparams("parallel", "arbitrary"),
        name="fox_prep",
    )(proj, proj, proj, proj, bf_row, selq, selk)


def _router_kernel(x_ref, g_ref, r0_ref, r1_ref, r2_ref, o_ref, lg_ref):
    x = x_ref[...]
    y = x * lax.rsqrt(jnp.mean(x * x, axis=-1, keepdims=True) + EPS) * g_ref[...]
    o_ref[...] = y.astype(o_ref.dtype)
    y0, y1, y2 = _split3(y)
    r0, r1, r2 = r0_ref[...], r1_ref[...], r2_ref[...]
    lg_ref[...] = (_dot(y0, r0) + (_dot(y0, r1) + _dot(y1, r0)) + (_dot(y0, r2) + _dot(y1, r1) + _dot(y2, r0)))


def _rmsnorm_router(x, g, router, tm=512):
    m, d = x.shape
    rp = jnp.zeros((d, LANES), F32).at[:, :router.shape[1]].set(router.astype(F32))
    r0 = rp.astype(BF16)
    r1 = (rp - r0.astype(F32)).astype(BF16)
    r2 = (rp - r0.astype(F32) - r1.astype(F32)).astype(BF16)
    rspec = pl.BlockSpec((d, LANES), lambda i: (0, 0))
    return pl.pallas_call(
        _router_kernel,
        grid=(m // tm,),
        in_specs=[pl.BlockSpec((tm, d), lambda i: (i, 0)), pl.BlockSpec((1, d), lambda i: (0, 0)), rspec, rspec, rspec],
        out_specs=[pl.BlockSpec((tm, d), lambda i: (i, 0)), pl.BlockSpec((tm, LANES), lambda i: (i, 0))],
        out_shape=[jax.ShapeDtypeStruct((m, d), F32), jax.ShapeDtypeStruct((m, LANES), F32)],
        compiler_params=_cparams("parallel"),
        name="rmsnorm_router",
    )(x, g.reshape(1, d), r0, r1, r2)


def _combine_norm_kernel(y_ref, ya_ref, yb_ref, gt_ref, g_ref, op_ref, os_ref, *, n_prompt):
    i = pl.program_id(0)
    gt = gt_ref[...]
    out = _rms(y_ref[...] + (gt[:, 0:1] * ya_ref[...] + gt[:, 1:2] * yb_ref[...]), g_ref[...])

    @pl.when(i < n_prompt)
    def _():
        op_ref[...] = out

    @pl.when(i >= n_prompt)
    def _():
        os_ref[...] = out


def _combine_norm(y, ya, yb, gates, g, mp, tm=512):
    m, d = y.shape
    assert mp % tm == 0 and (m - mp) % tm == 0
    n_p = mp // tm
    row = pl.BlockSpec((tm, d), lambda i: (i, 0))
    return pl.pallas_call(
        functools.partial(_combine_norm_kernel, n_prompt=n_p),
        grid=(m // tm,),
        in_specs=[row, row, row, pl.BlockSpec((tm, TOP_K), lambda i: (i, 0)), pl.BlockSpec((1, d), lambda i: (0, 0))],
        out_specs=[pl.BlockSpec((tm, d), lambda i: (jnp.minimum(i, n_p - 1), 0)),
                   pl.BlockSpec((tm, d), lambda i: (jnp.maximum(i - n_p, 0), 0))],
        out_shape=[jax.ShapeDtypeStruct((mp, d), F32), jax.ShapeDtypeStruct((m - mp, d), F32)],
        compiler_params=_cparams("arbitrary"),
        name="combine_norm",
    )(y, ya, yb, gates, g.reshape(1, d))


def _scatter_rows_kernel(dest_ref, x_ref, init_ref, o_ref, sem):
    del init_ref
    tm = x_ref.shape[0]

    def row_copy(r, d):
        return pltpu.make_async_copy(x_ref.at[pl.ds(r, 1)], o_ref.at[pl.ds(d, 1)], sem)

    def body(r, carry):
        for k in range(TOP_K):
            row_copy(r, dest_ref[TOP_K * r + k]).start()
        return carry

    lax.fori_loop(0, tm, body, 0)
    for _ in range(TOP_K):
        pltpu.make_async_copy(x_ref, o_ref.at[pl.ds(0, tm)], sem).wait()


def _scatter_rows(x, dest, n_rows, tm=512):
    m, d = x.shape
    return pl.pallas_call(
        _scatter_rows_kernel,
        grid=(m // tm,),
        in_specs=[pl.BlockSpec((TOP_K * tm,), lambda i: (i,), memory_space=pltpu.SMEM),
                  pl.BlockSpec((tm, d), lambda i: (i, 0)),
                  pl.BlockSpec(memory_space=pl.ANY)],
        out_specs=pl.BlockSpec(memory_space=pl.ANY),
        out_shape=jax.ShapeDtypeStruct((n_rows, d), x.dtype),
        scratch_shapes=[pltpu.SemaphoreType.DMA(())],
        input_output_aliases={2: 0},
        compiler_params=_cparams("arbitrary"),
        name="scatter_rows",
    )(dest, x, jnp.zeros((n_rows, d), x.dtype))


def _moe_dispatch(logits, tm):
    m, e = logits.shape
    idx = lax.broadcasted_iota(jnp.int32, (m, e), 1)
    m1 = jnp.max(logits, axis=1, keepdims=True)
    i1 = jnp.min(jnp.where(logits == m1, idx, e), axis=1, keepdims=True)
    rest = jnp.where(idx == i1, -jnp.inf, logits)
    m2 = jnp.max(rest, axis=1, keepdims=True)
    i2 = jnp.min(jnp.where(rest == m2, idx, e), axis=1, keepdims=True)
    ex = jnp.exp(m2 - m1)
    gates = jnp.concatenate([1.0 / (1.0 + ex), ex / (1.0 + ex)], axis=1)
    flat_e = jnp.concatenate([i1, i2], axis=1).reshape(m * TOP_K)
    onehot = (flat_e[:, None] == jnp.arange(e, dtype=jnp.int32)[None, :]).astype(jnp.int32)
    csum = jnp.cumsum(onehot, axis=0)
    counts = csum[-1]
    padded = -(-counts // tm) * tm
    ends = jnp.cumsum(padded)
    dest = jnp.sum(onehot * ((ends - padded)[None, :] + csum - 1), axis=1)
    n_tiles = (m * TOP_K) // tm + e
    tile_start = jnp.arange(n_tiles, dtype=jnp.int32) * tm
    tile_expert = jnp.minimum(jnp.sum((ends[None, :] <= tile_start[:, None]).astype(jnp.int32), axis=1), e - 1)
    return gates, dest.astype(jnp.int32), tile_expert, (ends[-1:] // tm).astype(jnp.int32)


def kernel(x_prompt, x_sample, cache_k_e, cache_v_e, state_s_e, state_conv_o, state_h_o, cache_k_o, cache_v_o,
           cache_logf_o, page_table, w_in_e, lb_logits, gnorm_a, w_out_e, ffn_w1, ffn_w3, ffn_w2, w_in_o, conv_w,
           conv_b, rg_wa, rg_ba, rg_wx, rg_bx, rg_lambda, fox_bf, w_out_o, moe_router, moe_w1, moe_w3, moe_w2,
           norm_mix, norm_ffn, norm_final):
    batch, seq, d = x_prompt.shape
    n_seq, ts, _ = x_sample.shape
    mp, ms = batch * seq, n_seq * ts
    m = mp + ms
    n_pages = page_table.shape[1]
    past = n_pages * PAGE_SIZE
    chunk_a = 128
    assert d == D_MODEL and ts <= SUBLANES and seq % chunk_a == 0 and m % chunk_a == 0

    x = (x_prompt.reshape(mp, d), x_sample.reshape(ms, d))
    pos_p = jnp.arange(seq, dtype=jnp.int32)
    pos_s = past + jnp.arange(ts, dtype=jnp.int32)

    def seq_rows(a):
        return a.reshape(n_seq, ts, a.shape[1])

    proj = _matmul([x], w_in_e[0].astype(BF16), norm_g=norm_mix[0])
    proj_s = proj[mp:]
    lb = jnp.cumsum(jax.nn.softmax(lb_logits.astype(F32), axis=0), axis=0)[0]
    oa_p, s_p = _hgrn2(proj.reshape(m // chunk_a, chunk_a, proj.shape[1]), batch, seq // chunk_a,
                       jnp.zeros((batch, H_A, DK_A, DV_A), F32), lb, gnorm_a[0])
    oa_s, s_s = _hgrn2(proj_s.reshape(n_seq, ts, proj.shape[1]), n_seq, 1, state_s_e[0], lb, gnorm_a[0],
                       group=8 if n_seq % 8 == 0 else 1)
    cos_p, sin_p = _rope_tables(pos_p)
    cos_s, sin_s = _rope_tables(pos_s)
    ke_p, ve_p, qa, ka, vt = _moba_prep(proj, batch, seq, cos_p, sin_p)
    ob_p = _flash(qa, ka, vt)
    ve_s = proj_s[:, COLS_E_Q + 2 * W_B:COLS_E_Q + 3 * W_B]
    ke_s, ob_s = _moba_sample(seq_rows(proj_s[:, COLS_E_Q:COLS_E_Q + W_B]),
                              seq_rows(proj_s[:, COLS_E_Q + W_B:COLS_E_Q + 2 * W_B]), seq_rows(ve_s),
                              cache_k_e[0], cache_v_e[0], page_table, cos_s, sin_s)
    y = _matmul([(oa_p, oa_s), (ob_p, ob_s.reshape(ms, W_B))], w_out_e[0].astype(BF16), res=x)
    y = _ffn(y, ffn_w1.astype(BF16), ffn_w3.astype(BF16), ffn_w2.astype(BF16), residual=True, norm_g=norm_ffn[0],
             tf=ffn_w1.shape[2] // 2)

    cols_o = w_in_o.shape[2]
    cols_pad = -(-cols_o // LANES) * LANES
    w_in_o_p = jnp.zeros((d, cols_pad), BF16).at[:, :cols_o].set(w_in_o[0].astype(BF16))
    proj_o = _matmul([y], w_in_o_p, norm_g=norm_mix[1])
    proj_os = proj_o[mp:]
    ow = _odd_weights(conv_w[0], conv_b[0], rg_wa[0], rg_ba[0], rg_wx[0], rg_bx[0], rg_lambda[0])
    oc_p, buf_p, h_p = _rglru_prompt(proj_o, batch, seq, *ow)
    ps = proj_os.reshape(n_seq, ts, cols_pad)
    cx_s = ps[..., :W_C]
    oc_s, h_s = _rglru_sample(cx_s.swapaxes(0, 1), ps[..., W_C:2 * W_C].swapaxes(0, 1), state_conv_o[0].swapaxes(0, 1),
                              state_h_o[0], past == 0, *ow)
    oc_s = oc_s.swapaxes(0, 1).reshape(ms, W_C)
    buf_s = jnp.concatenate([state_conv_o[0].astype(F32), cx_s], axis=1)[:, ts:]
    bf_row = jnp.zeros((1, LANES), F32).at[0, :H_D].set(fox_bf[0].astype(F32))
    selq, selk = _fox_selectors()
    lf_p, ko_p, vo_p, qa, ka, vt = _fox_prep(proj_o, batch, seq, bf_row, selq, selk)
    od_p = _flash(qa, ka, vt)
    ko_s = proj_os[:, COLS_O_Q + W_D:COLS_O_Q + 2 * W_D]
    vo_s = proj_os[:, COLS_O_Q + 2 * W_D:COLS_O_Q + 3 * W_D]
    lf_s, od_s = _fox_sample(seq_rows(proj_os[:, COLS_O_Q:COLS_O_Q + W_D]), seq_rows(ko_s), seq_rows(vo_s),
                             proj_os[:, COLS_O_F:COLS_O_F + H_D].reshape(n_seq, ts * H_D), fox_bf[0],
                             cache_k_o[0], cache_v_o[0], cache_logf_o[0], page_table)
    y = _matmul([(oc_p, oc_s), (od_p, od_s.reshape(ms, W_D))], w_out_o[0].astype(BF16), res=y)
    tm_e = 768 if (m * TOP_K) % 768 == 0 else 512
    hn, logits = _rmsnorm_router(y, norm_ffn[1], moe_router[0])
    gates, dest, tile_expert, n_used = _moe_dispatch(logits[:, :N_EXPERTS], tm_e)
    pair_pos = dest.reshape(m, TOP_K)
    y_e = _ffn(_scatter_rows(hn, dest, tile_expert.shape[0] * tm_e), moe_w1[0], moe_w3[0], moe_w2[0],
               tile_expert=tile_expert, n_used=n_used, tm=tm_e, tf=512)
    out_p, out_s = _combine_norm(y, jnp.take(y_e, pair_pos[:, 0], axis=0, mode="clip"),
                                 jnp.take(y_e, pair_pos[:, 1], axis=0, mode="clip"), gates, norm_final, mp)

    def heads(a, h):
        return a.reshape(1, n_seq, ts, h, a.shape[-1] // h)

    def heads_t(a):
        return jnp.transpose(a, (0, 3, 1, 2))[None]

    return (out_p.reshape(batch, seq, d), out_s.reshape(n_seq, ts, d),
            s_p[None], s_s[None],
            heads_t(ke_p), heads_t(ve_p), heads(ke_s, H_B), heads(ve_s, H_B),
            buf_p[None], buf_s[None], h_p[None], h_s[None],
            heads_t(ko_p), heads_t(vo_p), jnp.transpose(lf_p, (0, 2, 1))[None],
            heads(ko_s, H_D), heads(vo_s, H_D), lf_s.reshape(1, n_seq, ts, H_D))
```

```python
import functools
import math

import numpy as np
import jax
import jax.numpy as jnp
from jax import lax
from jax.experimental import pallas as pl
from jax.experimental.pallas import tpu as pltpu

F32 = jnp.float32
BF16 = jnp.bfloat16

D_MODEL = 1024
PAGE_SIZE = 128
H_A, DK_A, DV_A = 4, 128, 128
F_A, W_A = H_A * DK_A, H_A * DV_A
H_B, HD_B = 8, 64
W_B = H_B * HD_B
MOBA_BLOCK, MOBA_TOPK = 256, 3
W_C, NB_C, CONV_W, RG_C = 512, 8, 4, 8.0
BW_C = W_C // NB_C
H_D, HD_D = 8, 64
W_D = H_D * HD_D
N_EXPERTS, TOP_K = 8, 2
ROPE_THETA = 10000.0
EPS = 1e-6
NEG_INF = -1e30
LOG2E = math.log2(math.e)

COLS_E_Q = 2 * F_A + 2 * W_A
COLS_O_Q = 2 * W_C
COLS_O_F = 2 * W_C + 3 * W_D

LANES = 128
SUBLANES = 8
VMEM_LIMIT = 56 * 1024 * 1024
ATT_TILE = 256
TM_DENSE = 512
TM_EXPERT = 768
CHUNK_A = 128


def _cparams(*sem):
    return pltpu.CompilerParams(dimension_semantics=sem, vmem_limit_bytes=VMEM_LIMIT)


def _split3(x):
    hi = x.astype(BF16)
    r1 = x - hi.astype(F32)
    mid = r1.astype(BF16)
    lo = (r1 - mid.astype(F32)).astype(BF16)
    return hi, mid, lo


def _dot(a, b):
    return jnp.dot(a, b, preferred_element_type=F32)


def _dot_nt(a, b):
    return lax.dot_general(a, b, (((1,), (1,)), ((), ())), preferred_element_type=F32)


def _dot_tn(a, b):
    return lax.dot_general(a, b, (((0,), (0,)), ((), ())), preferred_element_type=F32)


def _dot3(a, b):
    b0, b1, b2 = _split3(b)
    return _dot(a, b0) + _dot(a, b1) + _dot(a, b2)


def _sigmoid(x):
    return 1.0 / (1.0 + jnp.exp(-x))


def _log_sigmoid(x):
    return jnp.minimum(x, 0.0) - jnp.log(1.0 + jnp.exp(-jnp.abs(x)))


def _gelu_tanh(x):
    return 0.5 * x * (1.0 + jnp.tanh(math.sqrt(2.0 / math.pi) * (x + 0.044715 * (x * x * x))))


def _rms(x, g):
    return x * lax.rsqrt(jnp.mean(x * x, axis=-1, keepdims=True) + EPS) * g


def _matmul_kernel(*refs, parts, n_prompt, has_norm, has_res):
    i = pl.program_id(0)
    refs = list(refs)
    g_ref = refs.pop(0) if has_norm else None
    xs = []
    for split in parts:
        if split:
            p_ref, s_ref = refs.pop(0), refs.pop(0)
            xs.append(jnp.where(i < n_prompt, p_ref[...], s_ref[...]))
        else:
            xs.append(refs.pop(0)[...])
    w_ref = refs.pop(0)
    res = None
    if has_res == "split":
        p_ref, s_ref = refs.pop(0), refs.pop(0)
        res = jnp.where(i < n_prompt, p_ref[...], s_ref[...])
    elif has_res:
        res = refs.pop(0)[...]
    o_ref = refs.pop(0)
    acc = None
    k0 = 0
    for x in xs:
        if has_norm:
            x = _rms(x, g_ref[...])
        kk = x.shape[1]
        part = _dot(x.astype(BF16), w_ref[k0:k0 + kk, :])
        acc = part if acc is None else acc + part
        k0 += kk
    if res is not None:
        acc = acc + res
    o_ref[...] = acc


def _matmul(xs, w, res=None, norm_g=None, tm=TM_DENSE):
    kt, n = w.shape
    parts = tuple(isinstance(x, tuple) for x in xs)
    m = sum(a.shape[0] for a in xs[0]) if parts[0] else xs[0].shape[0]
    in_specs, args = [], []
    n_prompt = [0]

    def add(x):
        if isinstance(x, tuple):
            xp, xsm = x
            assert xp.shape[0] % tm == 0 and xsm.shape[0] % tm == 0 and xp.shape[0] + xsm.shape[0] == m
            n_p = n_prompt[0] = xp.shape[0] // tm
            in_specs.append(pl.BlockSpec((tm, xp.shape[1]), lambda i: (jnp.minimum(i, n_p - 1), 0)))
            in_specs.append(pl.BlockSpec((tm, xp.shape[1]), lambda i: (jnp.maximum(i - n_p, 0), 0)))
            args.extend([xp, xsm])
        else:
            in_specs.append(pl.BlockSpec((tm, x.shape[1]), lambda i: (i, 0)))
            args.append(x)

    if norm_g is not None:
        assert len(xs) == 1
        in_specs.append(pl.BlockSpec((1, kt), lambda i: (0, 0)))
        args.append(norm_g.reshape(1, kt).astype(F32))
    for x in xs:
        add(x)
    in_specs.append(pl.BlockSpec((kt, n), lambda i: (0, 0)))
    args.append(w)
    if res is not None:
        add(res)
    has_res = "split" if isinstance(res, tuple) else res is not None
    return pl.pallas_call(
        functools.partial(_matmul_kernel, parts=parts, n_prompt=n_prompt[0], has_norm=norm_g is not None,
                          has_res=has_res),
        grid=(m // tm,),
        in_specs=in_specs,
        out_specs=pl.BlockSpec((tm, n), lambda i: (i, 0)),
        out_shape=jax.ShapeDtypeStruct((m, n), F32),
        compiler_params=_cparams("parallel"),
        name="matmul",
    )(*args)


def _ffn_kernel(te_ref, nu_ref, x_ref, w1_ref, w3_ref, w2_ref, *rest, has_norm, has_res):
    del te_ref
    rest = list(rest)
    g_ref = rest.pop(0) if has_norm else None
    res_ref = x_ref if has_res else None
    o_ref, acc_ref, xb_ref = rest
    i, f = pl.program_id(0), pl.program_id(1)

    @pl.when(i < nu_ref[0])
    def _():
        @pl.when(f == 0)
        def _():
            x = x_ref[...]
            if has_norm:
                x = _rms(x, g_ref[...])
            xb_ref[...] = x.astype(BF16)
            acc_ref[...] = jnp.zeros_like(acc_ref)

        x = xb_ref[...]
        a = _dot(x, w1_ref[...].astype(BF16))
        b = _dot(x, w3_ref[...].astype(BF16))
        g = (a * _sigmoid(a) * b).astype(BF16)
        acc_ref[...] += _dot(g, w2_ref[...].astype(BF16))

        @pl.when(f == pl.num_programs(1) - 1)
        def _():
            out = acc_ref[...]
            if has_res:
                out = out + res_ref[...]
            o_ref[...] = out

    @pl.when((i >= nu_ref[0]) & (f == pl.num_programs(1) - 1))
    def _():
        o_ref[...] = jnp.zeros_like(o_ref)


def _ffn(x, w1, w3, w2, residual=False, norm_g=None, tile_expert=None, n_used=None, tm=TM_DENSE, tf=256):
    m, d = x.shape
    f = w1.shape[-1]
    nf = f // tf
    if tile_expert is None:
        tile_expert = jnp.zeros((m // tm,), jnp.int32)
    if n_used is None:
        n_used = jnp.full((1,), m // tm, jnp.int32)

    def fblk(i, j, nu):
        return jnp.where(i < nu[0], j, nf - 1)

    row = pl.BlockSpec((tm, d), lambda i, j, te, nu: (i, 0))
    in_specs = [
        row,
        pl.BlockSpec((None, d, tf), lambda i, j, te, nu: (te[i], 0, fblk(i, j, nu))),
        pl.BlockSpec((None, d, tf), lambda i, j, te, nu: (te[i], 0, fblk(i, j, nu))),
        pl.BlockSpec((None, tf, d), lambda i, j, te, nu: (te[i], fblk(i, j, nu), 0)),
    ]
    args = [x, w1, w3, w2]
    if norm_g is not None:
        in_specs.append(pl.BlockSpec((1, d), lambda i, j, te, nu: (0, 0)))
        args.append(norm_g.reshape(1, d).astype(F32))
    return pl.pallas_call(
        functools.partial(_ffn_kernel, has_norm=norm_g is not None, has_res=residual),
        grid_spec=pltpu.PrefetchScalarGridSpec(
            num_scalar_prefetch=2,
            grid=(m // tm, nf),
            in_specs=in_specs,
            out_specs=row,
            scratch_shapes=[pltpu.VMEM((tm, d), F32), pltpu.VMEM((tm, d), BF16)],
        ),
        out_shape=jax.ShapeDtypeStruct((m, d), F32),
        compiler_params=_cparams("arbitrary", "arbitrary"),
        name="ffn",
    )(tile_expert, n_used, *args)


def _hgrn2_head(q, k, v, logf, st, c_sub):
    c = q.shape[0]
    ns = c // c_sub
    row = lax.broadcasted_iota(jnp.int32, (c, c), 0)
    col = lax.broadcasted_iota(jnp.int32, (c, c), 1)
    tri = (row >= col).astype(BF16)
    g = _dot3(tri, logf)
    vb = v.astype(BF16)
    o = _dot_nt((q * jnp.exp(g)).astype(BF16), st.astype(BF16))

    lane_c = lax.broadcasted_iota(jnp.int32, (c_sub, c), 1)
    slabs = []
    for i in range(ns):
        r0 = i * c_sub
        qi, ki, gi = q[r0:r0 + c_sub], k[r0:r0 + c_sub], g[r0:r0 + c_sub]
        slab = jnp.zeros((c_sub, c), F32)
        for s in range(c_sub):
            w = jnp.exp(jnp.minimum(gi - gi[s:s + 1, :], 0.0))
            colv = jnp.sum(qi * w * ki[s:s + 1, :], axis=-1, keepdims=True)
            slab = jnp.where(lane_c == r0 + s, colv, slab)
        slabs.append(slab)
    scores = slabs[0] if ns == 1 else jnp.concatenate(slabs, axis=0)
    scores = jnp.where((row >= col) & (row // c_sub == col // c_sub), scores, 0.0)

    w = c // 2
    while w >= c_sub:
        q_parts, k_parts = [], []
        for b in range(c // w):
            r0 = b * w
            if b % 2 == 1:
                q_parts.append(q[r0:r0 + w] * jnp.exp(g[r0:r0 + w] - g[r0 - 1:r0, :]))
                k_parts.append(jnp.zeros((w, q.shape[1]), F32))
            else:
                q_parts.append(jnp.zeros((w, q.shape[1]), F32))
                k_parts.append(k[r0:r0 + w] * jnp.exp(g[r0 + w - 1:r0 + w, :] - g[r0:r0 + w]))
        lvl = _dot_nt(jnp.concatenate(q_parts, axis=0).astype(BF16), jnp.concatenate(k_parts, axis=0).astype(BF16))
        scores = scores + jnp.where(((row // w) % 2 == 1) & (col // w == row // w - 1), lvl, 0.0)
        w //= 2
    o = o + _dot(scores.astype(BF16), vb)
    gl = g[c - 1:c, :]
    ke = (k * jnp.exp(gl - g)).astype(BF16)
    st_new = jnp.exp(gl) * st + _dot_tn(vb, ke)
    return o, st_new


def _hgrn2_kernel(aq_ref, af_ref, ai_ref, ag_ref, lb_ref, gn_ref, s0_ref, o_ref, s_ref, st_scr, *, c_sub, ct):
    t = pl.program_id(1)
    group = aq_ref.shape[0]

    @pl.when(t == 0)
    def _():
        for gi in range(group):
            for h in range(H_A):
                st_scr[gi, h] = s0_ref[gi, h].T

    def padded(x):
        if ct < SUBLANES:
            x = jnp.concatenate([x, jnp.zeros((SUBLANES - ct, x.shape[1]), F32)], axis=0)
        return x

    lb = lb_ref[...]
    for gi in range(group):
        aq, zf, vi = padded(aq_ref[gi]), padded(af_ref[gi]), padded(ai_ref[gi])
        c = aq.shape[0]
        real = lax.broadcasted_iota(jnp.int32, (c, F_A), 0) < ct
        sig = _sigmoid(zf)
        q = aq * _sigmoid(aq)
        logf = jnp.where(real, jnp.log(lb + (1.0 - lb) * sig), 0.0)
        k = jnp.where(real, (1.0 - lb) * _sigmoid(-zf), 0.0)
        outs = []
        for h in range(H_A):
            sl = slice(h * DK_A, (h + 1) * DK_A)
            o_h, st_new = _hgrn2_head(q[:, sl], k[:, sl], vi[:, sl], logf[:, sl], st_scr[gi, h], c_sub)
            st_scr[gi, h] = st_new
            outs.append(o_h)
        o = _rms(jnp.concatenate(outs, axis=1)[:ct], gn_ref[...])
        o_ref[gi] = o * _sigmoid(ag_ref[gi])

    @pl.when(t == pl.num_programs(1) - 1)
    def _():
        for gi in range(group):
            for h in range(H_A):
                s_ref[gi, h] = st_scr[gi, h].T


def _hgrn2(x3, n_seq, n_chunks, s0, lb, gnorm, group=1):
    ct = x3.shape[1]
    assert n_seq % group == 0 and (group == 1 or n_chunks == 1)

    def tok_spec(cb):
        return pl.BlockSpec((group, ct, F_A), lambda n, t: (n * n_chunks + t, 0, cb))

    state = pl.BlockSpec((group, H_A, DK_A, DV_A), lambda n, t: (n, 0, 0, 0))
    o, s = pl.pallas_call(
        functools.partial(_hgrn2_kernel, c_sub=SUBLANES, ct=ct),
        grid=(n_seq // group, n_chunks),
        in_specs=[tok_spec(0), tok_spec(1), tok_spec(2), tok_spec(3),
                  pl.BlockSpec((1, F_A), lambda n, t: (0, 0)),
                  pl.BlockSpec((1, W_A), lambda n, t: (0, 0)), state],
        out_specs=[pl.BlockSpec((group, ct, W_A), lambda n, t: (n * n_chunks + t, 0, 0)), state],
        out_shape=[jax.ShapeDtypeStruct((n_seq * n_chunks, ct, W_A), F32),
                   jax.ShapeDtypeStruct((n_seq, H_A, DK_A, DV_A), F32)],
        scratch_shapes=[pltpu.VMEM((group, H_A, DV_A, DK_A), F32)],
        compiler_params=_cparams("parallel", "arbitrary"),
        name="hgrn2",
    )(x3, x3, x3, x3, lb.reshape(1, F_A), gnorm.reshape(1, W_A), s0)
    return o.reshape(n_seq * n_chunks * ct, W_A), s


def _rope_tables(pos):
    half = HD_B // 2
    inv = ROPE_THETA ** (-jnp.arange(half, dtype=F32) / half)
    ang = pos.astype(F32)[:, None] * inv[None, :]
    cos, sin = jnp.cos(ang), jnp.sin(ang)
    return jnp.concatenate([cos, cos], axis=1), jnp.concatenate([-sin, sin], axis=1)


def _rope(x, cos2, sin2, lane):
    swapped = jnp.where((lane & (HD_B - 1)) >= HD_B // 2, pltpu.roll(x, HD_B // 2, 1), pltpu.roll(x, LANES - HD_B // 2, 1))
    return x * cos2 + swapped * sin2


def _dot_nt_f32(a, b):
    a0, a1, a2 = _split3(a)
    b0, b1, b2 = _split3(b)
    return (_dot_nt(a0, b0) + (_dot_nt(a0, b1) + _dot_nt(a1, b0))
            + (_dot_nt(a0, b2) + _dot_nt(a1, b1) + _dot_nt(a2, b0)))


def _top_select(g, index, axis, n_pick):
    sel = jnp.zeros(g.shape, F32)
    big = jnp.int32(1 << 20)
    for _ in range(n_pick):
        m = jnp.max(g, axis=axis, keepdims=True)
        first = jnp.min(jnp.where(g == m, index, big), axis=axis, keepdims=True)
        pick = jnp.where((index == first) & (m > 0.5 * NEG_INF), 1.0, 0.0)
        sel = sel + pick
        g = jnp.where(pick > 0.5, NEG_INF, g)
    return sel


VT_ROWS = HD_B + 16


def _store_heads_t(refs, c, x):
    xt = x.T
    for ref in refs:
        for hh in range(2):
            part = xt[hh * HD_B:(hh + 1) * HD_B]
            extra = ref.shape[1] - HD_B
            if extra:
                r = lax.broadcasted_iota(jnp.int32, (extra, part.shape[1]), 0)
                part = jnp.concatenate([part, jnp.where(r == 0, 1.0, 0.0)], axis=0)
            ref[2 * c + hh] = part.astype(ref.dtype)


def _moba_prep_kernel(q_ref, k_ref, v_ref, cos_ref, sin_ref, kt_ref, vl_ref, qa_ref, ka_ref, vt_ref, mrow_scr):
    t = pl.program_id(1)
    rows = q_ref.shape[0]

    @pl.when(t == 0)
    def _():
        mrow_scr[...] = jnp.zeros_like(mrow_scr)

    lane = lax.broadcasted_iota(jnp.int32, (rows, LANES), 1)
    lane1 = lax.broadcasted_iota(jnp.int32, (1, LANES), 1)
    low = lane < HD_B
    blk = lane & (HD_B - 1)
    nbm = HD_B // 2
    brow = lax.broadcasted_iota(jnp.int32, (nbm, rows), 0)
    cos2, sin2 = cos_ref[...], sin_ref[...]
    scale = HD_B ** -0.5 * LOG2E
    for c in range(W_B // LANES):
        sl = slice(c * LANES, (c + 1) * LANES)
        qr = _rope(q_ref[:, sl], cos2, sin2, lane)
        kr = _rope(k_ref[:, sl], cos2, sin2, lane)
        _store_heads_t([kt_ref], c, kr)
        gate_t = _dot_nt_f32(mrow_scr[c], qr)
        halves = []
        for r0 in (0, HD_B):
            g = jnp.where(brow < t, gate_t[r0:r0 + nbm], NEG_INF)
            keep = (_top_select(g, brow, 0, MOBA_TOPK) > 0.5) | (brow == t)
            halves += [jnp.where(keep, 0.0, NEG_INF), jnp.full((HD_B - nbm, rows), NEG_INF, F32)]
        msel = jnp.concatenate(halves, axis=0).T
        own = jnp.where(blk == t, 1.0, 0.0)
        qs = qr * scale
        qa_ref[2 * c] = jnp.where(low, qs, msel).astype(BF16)
        qa_ref[2 * c + 1] = jnp.where(low, msel, qs).astype(BF16)
        ka_ref[2 * c] = jnp.where(low, kr, own).astype(BF16)
        ka_ref[2 * c + 1] = jnp.where(low, own, kr).astype(BF16)
        _store_heads_t([vl_ref, vt_ref], c, v_ref[:, sl])
        mean = jnp.mean(kr, axis=0, keepdims=True)
        mrow_scr[c, pl.ds(HD_B + t, 1), :] = jnp.where(lane1 < HD_B, mean, 0.0)
        mrow_scr[c, pl.ds(t, 1), :] = jnp.where(lane1 < HD_B, 0.0, mean)


def _pair_specs(batch, heads, seq):
    nt = seq // ATT_TILE
    pair = pl.BlockSpec((None, heads, ATT_TILE, LANES), lambda b, t: (b, 0, t, 0))
    pair_shape = jax.ShapeDtypeStruct((batch, heads, seq, LANES), BF16)
    vt = pl.BlockSpec((None, heads, None, VT_ROWS, ATT_TILE), lambda b, t: (b, 0, t, 0, 0))
    vt_shape = jax.ShapeDtypeStruct((batch, heads, nt, VT_ROWS, ATT_TILE), BF16)
    leaf = pl.BlockSpec((None, heads, LANES // 2, ATT_TILE), lambda b, t: (b, 0, 0, t))
    leaf_shape = jax.ShapeDtypeStruct((batch, heads, LANES // 2, seq), F32)
    return [leaf, leaf, pair, pair, vt], [leaf_shape, leaf_shape, pair_shape, pair_shape, vt_shape]


def _moba_prep(proj, batch, seq, cos2, sin2):
    nb = seq // MOBA_BLOCK
    assert seq % MOBA_BLOCK == 0 and nb <= HD_B // 2 and MOBA_BLOCK == ATT_TILE
    rows = MOBA_BLOCK

    def tok(cb):
        return pl.BlockSpec((rows, W_B), lambda b, t: (b * nb + t, cb))

    specs, shapes = _pair_specs(batch, H_B, seq)
    cb0 = COLS_E_Q // W_B
    return pl.pallas_call(
        _moba_prep_kernel,
        grid=(batch, nb),
        in_specs=[tok(cb0), tok(cb0 + 1), tok(cb0 + 2),
                  pl.BlockSpec((rows, LANES), lambda b, t: (t, 0)),
                  pl.BlockSpec((rows, LANES), lambda b, t: (t, 0))],
        out_specs=specs,
        out_shape=shapes,
        scratch_shapes=[pltpu.VMEM((W_B // LANES, LANES, LANES), F32)],
        compiler_params=_cparams("parallel", "arbitrary"),
        name="moba_prep",
    )(proj, proj, proj, jnp.concatenate([cos2, cos2], axis=1), jnp.concatenate([sin2, sin2], axis=1))


def _flash_kernel(qa_ref, ka_ref, vt_ref, o_ref, s0_scr, s1_scr, acc_scr):
    i = pl.program_id(2)
    tq = qa_ref.shape[1]
    tv = vt_ref.shape[3]
    sub = tq // tv
    qs = [qa_ref[0], qa_ref[1]]

    def produce(j, s_ref):
        for hh in range(2):
            k = ka_ref[hh, pl.ds(pl.multiple_of(j * tq, tq), tq), :]
            s_ref[hh] = _dot_nt(k, qs[hh])

    def fold(x, op):
        return op(x.reshape(tq // SUBLANES, SUBLANES, tq), axis=0)

    def consume(j, s_ref, carry, diag=False):
        new = []
        for hh in range(2):
            m = carry[hh]
            if diag:
                krow = lax.broadcasted_iota(jnp.int32, (tq, tq), 0)
                qcol = lax.broadcasted_iota(jnp.int32, (tq, tq), 1)
                s_ref[hh] = jnp.where(krow <= qcol, s_ref[hh], NEG_INF)
            m_new = jnp.maximum(m, jnp.max(fold(s_ref[hh], jnp.max), axis=0, keepdims=True))
            pb = jnp.exp2(s_ref[hh] - m_new).astype(BF16)
            pv = _dot(vt_ref[hh, j * sub], pb[:tv])
            for u in range(1, sub):
                pv = pv + _dot(vt_ref[hh, j * sub + u], pb[u * tv:(u + 1) * tv])
            acc_scr[hh] = jnp.exp2(m - m_new) * acc_scr[hh] + pv
            new.append(m_new)
        return tuple(new)

    def pair(u, carry):
        j = 2 * u
        produce(j + 1, s1_scr)
        carry = consume(j, s0_scr, carry)
        produce(j + 2, s0_scr)
        return consume(j + 1, s1_scr, carry)

    def odd_tail(carry):
        produce(i, s1_scr)
        carry = consume(i - 1, s0_scr, carry)
        return consume(i, s1_scr, carry, diag=True)

    def even_tail(carry):
        return consume(i, s0_scr, carry, diag=True)

    acc_scr[...] = jnp.zeros_like(acc_scr)
    produce(0, s0_scr)
    init = tuple(jnp.full((1, tq), -jnp.inf, F32) for _ in range(2))
    carry = lax.fori_loop(0, i // 2, pair, init)
    lax.cond(i % 2 == 1, odd_tail, even_tail, carry)
    ot = jnp.concatenate([acc_scr[hh, :HD_B] / acc_scr[hh, HD_B:HD_B + 1] for hh in range(2)], axis=0)
    o_ref[...] = ot.T


def _flash(qa, ka, vt, tq=512):
    batch, heads, seq, _ = qa.shape
    tq = min(tq, seq)
    nq = seq // tq
    nv, hd, tv = vt.shape[2:]
    return pl.pallas_call(
        _flash_kernel,
        grid=(batch, heads // 2, nq),
        in_specs=[pl.BlockSpec((None, 2, tq, LANES), lambda b, c, i: (b, c, i, 0)),
                  pl.BlockSpec((None, 2, seq, LANES), lambda b, c, i: (b, c, 0, 0)),
                  pl.BlockSpec((None, 2, nv, hd, tv), lambda b, c, i: (b, c, 0, 0, 0))],
        out_specs=pl.BlockSpec((tq, LANES), lambda b, c, i: (b * nq + i, c)),
        out_shape=jax.ShapeDtypeStruct((batch * seq, heads * LANES // 2), F32),
        scratch_shapes=[pltpu.VMEM((2, tq, tq), F32), pltpu.VMEM((2, tq, tq), F32), pltpu.VMEM((2, hd, tq), F32)],
        compiler_params=_cparams("parallel", "parallel", "arbitrary"),
        name="flash",
    )(qa, ka, vt)


SAMPLE_GROUP = 2


def _paged_specs(n_pages, tail, group):
    zeros = (0,) * len(tail)
    return [pl.BlockSpec((None,) + tail, functools.partial(lambda n, pt, g, p: (pt[n * group + g, p],) + zeros, g=g, p=p))
            for g in range(group) for p in range(n_pages)]


def _group_size(n_seq):
    return SAMPLE_GROUP if n_seq % SAMPLE_GROUP == 0 else 1


def _pool_pages(pool):
    n_pool, page, h, dd = pool.shape
    return jnp.transpose(pool, (0, 2, 3, 1)).reshape(n_pool, h * dd, page)


def _pad_rows(x, rows):
    return jnp.concatenate([x, jnp.zeros((rows - x.shape[0], x.shape[1]), x.dtype)], axis=0)


def _query_rows(q, lane, rowh):
    t, w = q.shape
    rep = jnp.broadcast_to(q[:, None, :], (t, H_B, w)).reshape(t * H_B, w)
    return jnp.where((lane // HD_B) == rowh, rep, 0.0)


def _head_rows_to_tokens(o, lane, rowh, t):
    o = jnp.where((lane // HD_B) == rowh, o, 0.0)
    return jnp.sum(o.reshape(t, H_B, o.shape[1]), axis=1)


def _sample_softmax_pv(s_new, s_pages, v_new, vp, o_ref, lane, rowh, t):
    m = jnp.max(s_new, axis=1, keepdims=True)
    mm = s_pages[0]
    for s in s_pages[1:]:
        mm = jnp.maximum(mm, s)
    m = jnp.maximum(m, jnp.max(mm, axis=1, keepdims=True))
    p_new = jnp.exp(s_new - m)
    o = _dot(p_new.astype(BF16), v_new.astype(BF16))
    lsum = None
    for p, s in enumerate(s_pages):
        pp = jnp.exp(s - m)
        lsum = pp if lsum is None else lsum + pp
        o = o + _dot_nt(pp.astype(BF16), vp[p][...].astype(BF16))
    l = jnp.sum(p_new, axis=1, keepdims=True) + jnp.sum(lsum, axis=1, keepdims=True)
    o_ref[...] = _head_rows_to_tokens(o / l, lane, rowh, t)


def _moba_sample_kernel(pt_ref, q_ref, k_ref, v_ref, cos_ref, sin_ref, *rest, n_pages):
    del pt_ref
    group = q_ref.shape[0]
    ke_ref, o_ref = rest[2 * group * n_pages:]
    for g in range(group):
        kp = rest[g * n_pages:(g + 1) * n_pages]
        vp = rest[(group + g) * n_pages:(group + g + 1) * n_pages]
        _moba_sample_one(q_ref.at[g], k_ref.at[g], v_ref.at[g], cos_ref, sin_ref, kp, vp, ke_ref.at[g], o_ref.at[g], n_pages)


def _moba_sample_one(q_ref, k_ref, v_ref, cos_ref, sin_ref, kp, vp, ke_ref, o_ref, n_pages):
    t = q_ref.shape[0]
    nc = t * H_B
    nbp = n_pages * PAGE_SIZE // MOBA_BLOCK
    ppb = MOBA_BLOCK // PAGE_SIZE
    lane8 = lax.broadcasted_iota(jnp.int32, (SUBLANES, LANES), 1)
    cos2, sin2 = _pad_rows(cos_ref[...], SUBLANES), _pad_rows(sin_ref[...], SUBLANES)
    q8, k8, v8 = _pad_rows(q_ref[...], SUBLANES), _pad_rows(k_ref[...], SUBLANES), _pad_rows(v_ref[...], SUBLANES)
    qr = jnp.concatenate([_rope(q8[:, c * LANES:(c + 1) * LANES], cos2, sin2, lane8) for c in range(W_B // LANES)], axis=1)
    kr = jnp.concatenate([_rope(k8[:, c * LANES:(c + 1) * LANES], cos2, sin2, lane8) for c in range(W_B // LANES)], axis=1)
    ke_ref[...] = kr[:t]

    lane = lax.broadcasted_iota(jnp.int32, (nc, W_B), 1)
    rowh = lax.broadcasted_iota(jnp.int32, (nc, W_B), 0) % H_B
    qrows = _query_rows(qr[:t], lane, rowh)

    lanem = lax.broadcasted_iota(jnp.int32, (W_B, LANES), 1)
    means = jnp.zeros((W_B, LANES), F32)
    for j in range(nbp):
        blk_sum = kp[ppb * j][...]
        for u in range(1, ppb):
            blk_sum = blk_sum + kp[ppb * j + u][...]
        means = jnp.where(lanem == j, jnp.sum(blk_sum, axis=1, keepdims=True) * (1.0 / MOBA_BLOCK), means)
    q0, q1, q2 = _split3(qrows)
    m0, m1, m2 = _split3(means)
    gate = _dot(q0, m0) + (_dot(q0, m1) + _dot(q1, m0)) + (_dot(q0, m2) + _dot(q1, m1) + _dot(q2, m0))
    lg = lax.broadcasted_iota(jnp.int32, (nc, LANES), 1)
    sel = _top_select(jnp.where(lg < nbp, gate, NEG_INF), lg, 1, MOBA_TOPK)

    qb = (qrows * (HD_B ** -0.5)).astype(BF16)
    s_pages = []
    for p in range(n_pages):
        s = _dot(qb, kp[p][...].astype(BF16))
        j = p // ppb
        s_pages.append(jnp.where(sel[:, j:j + 1] > 0.5, s, NEG_INF))
    s_new = _dot_nt(qb, kr.astype(BF16))
    ktok = lax.broadcasted_iota(jnp.int32, (nc, SUBLANES), 1)
    qtok = lax.broadcasted_iota(jnp.int32, (nc, SUBLANES), 0) // H_B
    s_new = jnp.where((ktok <= qtok) & (ktok < t), s_new, NEG_INF)
    _sample_softmax_pv(s_new, s_pages, v8, vp, o_ref, lane, rowh, t)


def _moba_sample(q, k, v, pool_k, pool_v, page_table, cos2, sin2):
    n_seq, t, w = q.shape
    n_pages = page_table.shape[1]
    assert (n_pages * PAGE_SIZE) % MOBA_BLOCK == 0
    group = _group_size(n_seq)
    row = pl.BlockSpec((group, t, w), lambda n, pt: (n, 0, 0))
    tab = pl.BlockSpec((t, LANES), lambda n, pt: (0, 0))
    page = (w, PAGE_SIZE)
    return pl.pallas_call(
        functools.partial(_moba_sample_kernel, n_pages=n_pages),
        grid_spec=pltpu.PrefetchScalarGridSpec(
            num_scalar_prefetch=1,
            grid=(n_seq // group,),
            in_specs=[row, row, row, tab, tab] + _paged_specs(n_pages, page, group) + _paged_specs(n_pages, page, group),
            out_specs=[row, row],
        ),
        out_shape=[jax.ShapeDtypeStruct((n_seq, t, w), F32)] * 2,
        compiler_params=_cparams("arbitrary"),
        name="moba_sample",
    )(page_table, q, k, v, jnp.concatenate([cos2, cos2], axis=1), jnp.concatenate([sin2, sin2], axis=1),
      *([_pool_pages(pool_k)] * (group * n_pages)), *([_pool_pages(pool_v)] * (group * n_pages)))


def _fox_sample_kernel(pt_ref, q_ref, k_ref, v_ref, df_ref, bf_ref, *rest, n_pages):
    del pt_ref
    group = q_ref.shape[0]
    lf_ref, o_ref = rest[3 * group * n_pages:]
    for g in range(group):
        kp, vp, lp = (rest[(kind * group + g) * n_pages:(kind * group + g + 1) * n_pages] for kind in range(3))
        _fox_sample_one(q_ref.at[g], k_ref.at[g], v_ref.at[g], df_ref.at[g], bf_ref, kp, vp, lp, lf_ref.at[g], o_ref.at[g],
                        n_pages)


def _fox_sample_one(q_ref, k_ref, v_ref, df_ref, bf_ref, kp, vp, lp, lf_ref, o_ref, n_pages):
    t = q_ref.shape[0]
    nc = t * H_D
    k8, v8 = _pad_rows(k_ref[...], SUBLANES), _pad_rows(v_ref[...], SUBLANES)
    lane = lax.broadcasted_iota(jnp.int32, (nc, W_D), 1)
    rowh = lax.broadcasted_iota(jnp.int32, (nc, W_D), 0) % H_D
    qb = (_query_rows(q_ref[...], lane, rowh) * (HD_D ** -0.5)).astype(BF16)

    lf_row = _log_sigmoid(df_ref[...] + bf_ref[...])
    lf_ref[...] = lf_row
    rr = lax.broadcasted_iota(jnp.int32, (nc, nc), 0)
    cc = lax.broadcasted_iota(jnp.int32, (nc, nc), 1)
    same = (rr % H_D) == (cc % H_D)
    cn_col = jnp.sum(jnp.where(same & (cc // H_D <= rr // H_D), lf_row, 0.0), axis=1, keepdims=True)
    ktok = lax.broadcasted_iota(jnp.int32, (nc, SUBLANES), 1)
    qtok = lax.broadcasted_iota(jnp.int32, (nc, SUBLANES), 0) // H_D
    cn_keys = jnp.zeros((nc, SUBLANES), F32)
    for tp in range(t):
        col = jnp.sum(jnp.where(same & (cc // H_D <= tp), lf_row, 0.0), axis=1, keepdims=True)
        cn_keys = jnp.where(ktok == tp, col, cn_keys)

    lfa = jnp.concatenate([lp[p][...] for p in range(n_pages)], axis=0)
    nr = lfa.shape[0]
    lane_r = lax.broadcasted_iota(jnp.int32, (nr, LANES), 1)
    incl = lfa
    sh = 1
    while sh < LANES:
        incl = incl + jnp.where(lane_r < LANES - sh, pltpu.roll(incl, LANES - sh, 1), 0.0)
        sh *= 2
    r2 = lax.broadcasted_iota(jnp.int32, (nr, nr), 0)
    c2 = lax.broadcasted_iota(jnp.int32, (nr, nr), 1)
    later = ((c2 > r2) & ((c2 % H_D) == (r2 % H_D))).astype(BF16)
    suf = (incl - lfa) + _dot3(later, jnp.broadcast_to(incl[:, 0:1], (nr, LANES)))

    s_pages = []
    for p in range(n_pages):
        bias = jnp.concatenate([suf[H_D * p:H_D * (p + 1)]] * t, axis=0)
        s_pages.append(_dot(qb, kp[p][...].astype(BF16)) + bias + cn_col)
    s_new = _dot_nt(qb, k8.astype(BF16)) + (cn_col - cn_keys)
    s_new = jnp.where((ktok <= qtok) & (ktok < t), s_new, NEG_INF)
    _sample_softmax_pv(s_new, s_pages, v8, vp, o_ref, lane, rowh, t)


def _fox_sample(q, k, v, df, bf, pool_k, pool_v, pool_lf, page_table):
    n_seq, t, w = q.shape
    nc = t * H_D
    n_pages = page_table.shape[1]
    bft = jnp.tile(bf.astype(F32), t)
    group = _group_size(n_seq)
    row = pl.BlockSpec((group, t, w), lambda n, pt: (n, 0, 0))
    frow = pl.BlockSpec((group, 1, nc), lambda n, pt: (n, 0, 0))
    page = (w, PAGE_SIZE)
    lf, o = pl.pallas_call(
        functools.partial(_fox_sample_kernel, n_pages=n_pages),
        grid_spec=pltpu.PrefetchScalarGridSpec(
            num_scalar_prefetch=1,
            grid=(n_seq // group,),
            in_specs=[row, row, row, frow, pl.BlockSpec((1, nc), lambda n, pt: (0, 0))]
            + _paged_specs(n_pages, page, group) + _paged_specs(n_pages, page, group)
            + _paged_specs(n_pages, (H_D, PAGE_SIZE), group),
            out_specs=[frow, row],
        ),
        out_shape=[jax.ShapeDtypeStruct((n_seq, 1, nc), F32), jax.ShapeDtypeStruct((n_seq, t, w), F32)],
        compiler_params=_cparams("arbitrary"),
        name="fox_sample",
    )(page_table, q, k, v, df.reshape(n_seq, 1, nc), bft.reshape(1, nc),
      *([_pool_pages(pool_k)] * (group * n_pages)), *([_pool_pages(pool_v)] * (group * n_pages)),
      *([jnp.transpose(pool_lf, (0, 2, 1))] * (group * n_pages)))
    return lf, o


def _shift_rows(x, d, fill):
    return jnp.concatenate([jnp.full((d, x.shape[1]), fill, x.dtype), x[:-d]], axis=0)


def _rglru_gates(xc, wa_ref, ba_ref, wx_ref, bx_ref, sp_ref, first_row_pos0):
    xb = xc.astype(BF16)
    r = _sigmoid(_dot(xb, wa_ref[...]) + ba_ref[...])
    i = _sigmoid(_dot(xb, wx_ref[...]) + bx_ref[...])
    log_a = -RG_C * r * sp_ref[...]
    a = jnp.exp(log_a)
    mult = jnp.sqrt(1.0 - jnp.exp(2.0 * log_a))
    if first_row_pos0 is not None:
        mult = jnp.where(first_row_pos0, 1.0, mult)
    return a, xc * i * mult


def _rglru_kernel(cx_ref, cg_ref, cw_ref, cb_ref, wa_ref, ba_ref, wx_ref, bx_ref, sp_ref,
                  o_ref, buf_ref, h_ref, tail_scr, h_scr):
    t = pl.program_id(1)
    tt = cx_ref.shape[0]

    @pl.when(t == 0)
    def _():
        tail_scr[...] = jnp.zeros_like(tail_scr)
        h_scr[...] = jnp.zeros_like(h_scr)

    cx = cx_ref[...]
    ext = jnp.concatenate([tail_scr[...], cx], axis=0)
    xc = cb_ref[...] + cx * cw_ref[CONV_W - 1:CONV_W, :]
    for d in range(1, CONV_W):
        xc = xc + ext[SUBLANES - d:SUBLANES - d + tt] * cw_ref[CONV_W - 1 - d:CONV_W - d, :]
    row = lax.broadcasted_iota(jnp.int32, (tt, W_C), 0)
    a, b = _rglru_gates(xc, wa_ref, ba_ref, wx_ref, bx_ref, sp_ref, (row == 0) & (t == 0))
    d = 1
    while d < tt:
        b = b + a * _shift_rows(b, d, 0.0)
        a = a * _shift_rows(a, d, 1.0)
        d *= 2
    h = a * h_scr[0:1, :] + b
    o_ref[...] = h * _gelu_tanh(cg_ref[...])
    h_scr[0:1, :] = h[tt - 1:tt, :]
    tail_scr[...] = cx[tt - SUBLANES:tt, :]

    @pl.when(t == pl.num_programs(1) - 1)
    def _():
        buf_ref[...] = cx[tt - (CONV_W - 1):tt, :]
        h_ref[...] = h[tt - 1:tt, :]


def _rglru_prompt(proj, batch, seq, cw, cb, wa, ba, wx, bx, sp, tt=256):
    nt = seq // tt
    vec = pl.BlockSpec((1, W_C), lambda b, t: (0, 0))
    mat = pl.BlockSpec((W_C, W_C), lambda b, t: (0, 0))
    o, buf, h = pl.pallas_call(
        _rglru_kernel,
        grid=(batch, nt),
        in_specs=[pl.BlockSpec((tt, W_C), lambda b, t: (b * nt + t, 0)),
                  pl.BlockSpec((tt, W_C), lambda b, t: (b * nt + t, 1)),
                  pl.BlockSpec((CONV_W, W_C), lambda b, t: (0, 0)), vec, mat, vec, mat, vec, vec],
        out_specs=[pl.BlockSpec((tt, W_C), lambda b, t: (b * nt + t, 0)),
                   pl.BlockSpec((None, CONV_W - 1, W_C), lambda b, t: (b, 0, 0)),
                   pl.BlockSpec((None, 1, W_C), lambda b, t: (b, 0, 0))],
        out_shape=[jax.ShapeDtypeStruct((batch * seq, W_C), F32),
                   jax.ShapeDtypeStruct((batch, CONV_W - 1, W_C), F32),
                   jax.ShapeDtypeStruct((batch, 1, W_C), F32)],
        scratch_shapes=[pltpu.VMEM((SUBLANES, W_C), F32), pltpu.VMEM((SUBLANES, W_C), F32)],
        compiler_params=_cparams("parallel", "arbitrary"),
        name="rglru",
    )(proj, proj, cw, cb, wa, ba, wx, bx, sp)
    return o, buf, h.reshape(batch, W_C)


def _rglru_sample_kernel(cx_ref, cg_ref, buf_ref, h0_ref, cw_ref, cb_ref, wa_ref, ba_ref, wx_ref, bx_ref, sp_ref,
                         o_ref, h_ref, *, pos0_is_zero):
    t, n, _ = cx_ref.shape
    xp = [buf_ref[j] for j in range(CONV_W - 1)] + [cx_ref[j] for j in range(t)]
    xcs = []
    for s in range(t):
        xc = cb_ref[...] + xp[s] * cw_ref[0:1, :]
        for j in range(1, CONV_W):
            xc = xc + xp[s + j] * cw_ref[j:j + 1, :]
        xcs.append(xc)
    xc = jnp.concatenate(xcs, axis=0)
    first = (lax.broadcasted_iota(jnp.int32, xc.shape, 0) < n) if pos0_is_zero else None
    a, b = _rglru_gates(xc, wa_ref, ba_ref, wx_ref, bx_ref, sp_ref, first)
    h = h0_ref[...]
    for s in range(t):
        h = a[s * n:(s + 1) * n] * h + b[s * n:(s + 1) * n]
        o_ref[s] = h * _gelu_tanh(cg_ref[s])
    h_ref[...] = h


def _rglru_sample(cx, cg, buf, h0, pos0_is_zero, cw, cb, wa, ba, wx, bx, sp):
    t, n, _ = cx.shape
    return pl.pallas_call(
        functools.partial(_rglru_sample_kernel, pos0_is_zero=pos0_is_zero),
        out_shape=[jax.ShapeDtypeStruct((t, n, W_C), F32), jax.ShapeDtypeStruct((n, W_C), F32)],
        compiler_params=pltpu.CompilerParams(vmem_limit_bytes=VMEM_LIMIT),
        name="rglru_sample",
    )(cx, cg, buf, h0, cw, cb, wa, ba, wx, bx, sp)


def _odd_weights(cw, cb, wa, ba, wx, bx, lam):
    def bd(w):
        return jax.scipy.linalg.block_diag(*[w[g] for g in range(NB_C)]).astype(BF16)

    def r(v):
        return v.reshape(1, W_C).astype(F32)

    return cw.astype(F32), r(cb), bd(wa), r(ba), bd(wx), r(bx), r(jax.nn.softplus(-lam.astype(F32)))


FOX_ONE_LANE = 3 * H_D


def _fox_selectors():
    selq = np.zeros((H_D // 2, LANES, LANES), np.float32)
    selk = np.zeros((H_D // 2, LANES, LANES), np.float32)
    for cc in range(H_D // 2):
        for h, base in ((2 * cc, HD_D), (2 * cc + 1, 0)):
            for piece in range(3):
                selq[cc, piece * H_D + h, base + piece] = 1.0
                selq[cc, FOX_ONE_LANE, base + 3 + piece] = 1.0
                selk[cc, FOX_ONE_LANE, base + piece] = 1.0
                selk[cc, piece * H_D + h, base + 3 + piece] = -1.0
    return jnp.asarray(selq, BF16), jnp.asarray(selk, BF16)


def _split_features(c, lane):
    hi, mid, lo = _split3(jnp.where(lane < H_D, c, 0.0))
    feat = (hi.astype(F32) + pltpu.roll(mid.astype(F32), H_D, 1) + pltpu.roll(lo.astype(F32), 2 * H_D, 1)
            + jnp.where(lane == FOX_ONE_LANE, 1.0, 0.0))
    return feat.astype(BF16)


def _fox_prep_kernel(q_ref, k_ref, v_ref, df_ref, bf_ref, selq_ref, selk_ref,
                     lf_ref, kt_ref, vl_ref, qa_ref, ka_ref, vt_ref, c_scr):
    t = pl.program_id(1)
    rows = q_ref.shape[0]

    @pl.when(t == 0)
    def _():
        c_scr[...] = jnp.zeros_like(c_scr)

    lane = lax.broadcasted_iota(jnp.int32, (rows, LANES), 1)
    low = lane < HD_D
    logf = jnp.where(lane < H_D, _log_sigmoid(df_ref[...] + bf_ref[...]), 0.0)
    lf_ref[...] = logf.T[:H_D]
    tri =(lax.broadcasted_iota(jnp.int32, (rows, rows), 0) >= lax.broadcasted_iota(jnp.int32, (rows, rows), 1)).astype(BF16)
    c = c_scr[0:1, :] + _dot3(tri, logf)
    c_scr[0:1, :] = c[rows - 1:rows, :]
    feat = _split_features(c * LOG2E, lane)
    scale = HD_D ** -0.5 * LOG2E
    for cc in range(W_D // LANES):
        sl = slice(cc * LANES, (cc + 1) * LANES)
        eq = _dot(feat, selq_ref[cc])
        ek = _dot(feat, selk_ref[cc])
        qs, kc = q_ref[:, sl] * scale, k_ref[:, sl]
        qa_ref[2 * cc] = jnp.where(low, qs, eq).astype(BF16)
        qa_ref[2 * cc + 1] = jnp.where(low, eq, qs).astype(BF16)
        ka_ref[2 * cc] = jnp.where(low, kc, ek).astype(BF16)
        ka_ref[2 * cc + 1] = jnp.where(low, ek, kc).astype(BF16)
        _store_heads_t([kt_ref], cc, kc)
        _store_heads_t([vl_ref, vt_ref], cc, v_ref[:, sl])


def _fox_prep(proj, batch, seq, bf_row, selq, selk):
    rows = ATT_TILE
    nt = seq // rows

    def tok(cb):
        return pl.BlockSpec((rows, W_D), lambda b, t: (b * nt + t, cb))

    specs, shapes = _pair_specs(batch, H_D, seq)
    sel = pl.BlockSpec((H_D // 2, LANES, LANES), lambda b, t: (0, 0, 0))
    cb0 = COLS_O_Q // W_D
    return pl.pallas_call(
        _fox_prep_kernel,
        grid=(batch, nt),
        in_specs=[tok(cb0), tok(cb0 + 1), tok(cb0 + 2),
                  pl.BlockSpec((rows, LANES), lambda b, t: (b * nt + t, COLS_O_F // LANES)),
                  pl.BlockSpec((1, LANES), lambda b, t: (0, 0)), sel, sel],
        out_specs=[pl.BlockSpec((None, H_D, rows), lambda b, t: (b, 0, t))] + specs,
        out_shape=[jax.ShapeDtypeStruct((batch, H_D, seq), F32)] + shapes,
        scratch_shapes=[pltpu.VMEM((SUBLANES, LANES), F32)],
        compiler_params=_cparams("parallel", "arbitrary"),
        name="fox_prep",
    )(proj, proj, proj, proj, bf_row, selq, selk)


def _router_kernel(x_ref, g_ref, r0_ref, r1_ref, r2_ref, o_ref, lg_ref):
    x = x_ref[...]
    y = x * lax.rsqrt(jnp.mean(x * x, axis=-1, keepdims=True) + EPS) * g_ref[...]
    o_ref[...] = y.astype(o_ref.dtype)
    y0, y1, y2 = _split3(y)
    r0, r1, r2 = r0_ref[...], r1_ref[...], r2_ref[...]
    lg_ref[...] = (_dot(y0, r0) + (_dot(y0, r1) + _dot(y1, r0)) + (_dot(y0, r2) + _dot(y1, r1) + _dot(y2, r0)))


def _rmsnorm_router(x, g, router, tm=TM_DENSE):
    m, d = x.shape
    rp = jnp.zeros((d, LANES), F32).at[:, :router.shape[1]].set(router.astype(F32))
    r0 = rp.astype(BF16)
    r1 = (rp - r0.astype(F32)).astype(BF16)
    r2 = (rp - r0.astype(F32) - r1.astype(F32)).astype(BF16)
    rspec = pl.BlockSpec((d, LANES), lambda i: (0, 0))
    return pl.pallas_call(
        _router_kernel,
        grid=(m // tm,),
        in_specs=[pl.BlockSpec((tm, d), lambda i: (i, 0)), pl.BlockSpec((1, d), lambda i: (0, 0)), rspec, rspec, rspec],
        out_specs=[pl.BlockSpec((tm, d), lambda i: (i, 0)), pl.BlockSpec((tm, LANES), lambda i: (i, 0))],
        out_shape=[jax.ShapeDtypeStruct((m, d), F32), jax.ShapeDtypeStruct((m, LANES), F32)],
        compiler_params=_cparams("parallel"),
        name="rmsnorm_router",
    )(x, g.reshape(1, d), r0, r1, r2)


def _combine_norm_kernel(y_ref, ya_ref, yb_ref, gt_ref, g_ref, op_ref, os_ref, *, n_prompt):
    i = pl.program_id(0)
    gt = gt_ref[...]
    out = _rms(y_ref[...] + (gt[:, 0:1] * ya_ref[...] + gt[:, 1:2] * yb_ref[...]), g_ref[...])

    @pl.when(i < n_prompt)
    def _():
        op_ref[...] = out

    @pl.when(i >= n_prompt)
    def _():
        os_ref[...] = out


def _combine_norm(y, ya, yb, gates, g, mp, tm=TM_DENSE):
    m, d = y.shape
    assert mp % tm == 0 and (m - mp) % tm == 0
    n_p = mp // tm
    row = pl.BlockSpec((tm, d), lambda i: (i, 0))
    return pl.pallas_call(
        functools.partial(_combine_norm_kernel, n_prompt=n_p),
        grid=(m // tm,),
        in_specs=[row, row, row, pl.BlockSpec((tm, TOP_K), lambda i: (i, 0)), pl.BlockSpec((1, d), lambda i: (0, 0))],
        out_specs=[pl.BlockSpec((tm, d), lambda i: (jnp.minimum(i, n_p - 1), 0)),
                   pl.BlockSpec((tm, d), lambda i: (jnp.maximum(i - n_p, 0), 0))],
        out_shape=[jax.ShapeDtypeStruct((mp, d), F32), jax.ShapeDtypeStruct((m - mp, d), F32)],
        compiler_params=_cparams("arbitrary"),
        name="combine_norm",
    )(y, ya, yb, gates, g.reshape(1, d))


def _scatter_rows_kernel(dest_ref, x_ref, init_ref, o_ref, sem):
    del init_ref
    tm = x_ref.shape[0]

    def row_copy(r, d):
        return pltpu.make_async_copy(x_ref.at[pl.ds(r, 1)], o_ref.at[pl.ds(d, 1)], sem)

    def body(r, carry):
        for k in range(TOP_K):
            row_copy(r, dest_ref[TOP_K * r + k]).start()
        return carry

    lax.fori_loop(0, tm, body, 0, unroll=8)
    for _ in range(TOP_K):
        pltpu.make_async_copy(x_ref, o_ref.at[pl.ds(0, tm)], sem).wait()


def _scatter_rows(x, dest, n_rows, tm=TM_DENSE):
    m, d = x.shape
    return pl.pallas_call(
        _scatter_rows_kernel,
        grid=(m // tm,),
        in_specs=[pl.BlockSpec((TOP_K * tm,), lambda i: (i,), memory_space=pltpu.SMEM),
                  pl.BlockSpec((tm, d), lambda i: (i, 0)),
                  pl.BlockSpec(memory_space=pl.ANY)],
        out_specs=pl.BlockSpec(memory_space=pl.ANY),
        out_shape=jax.ShapeDtypeStruct((n_rows, d), x.dtype),
        scratch_shapes=[pltpu.SemaphoreType.DMA(())],
        input_output_aliases={2: 0},
        compiler_params=_cparams("arbitrary"),
        name="scatter_rows",
    )(dest, x, jnp.zeros((n_rows, d), x.dtype))


def _moe_dispatch(logits, tm):
    m, e = logits.shape
    idx = lax.broadcasted_iota(jnp.int32, (m, e), 1)
    m1 = jnp.max(logits, axis=1, keepdims=True)
    i1 = jnp.min(jnp.where(logits == m1, idx, e), axis=1, keepdims=True)
    rest = jnp.where(idx == i1, -jnp.inf, logits)
    m2 = jnp.max(rest, axis=1, keepdims=True)
    i2 = jnp.min(jnp.where(rest == m2, idx, e), axis=1, keepdims=True)
    ex = jnp.exp(m2 - m1)
    gates = jnp.concatenate([1.0 / (1.0 + ex), ex / (1.0 + ex)], axis=1)
    flat_e = jnp.concatenate([i1, i2], axis=1).reshape(m * TOP_K)
    onehot = (flat_e[:, None] == jnp.arange(e, dtype=jnp.int32)[None, :]).astype(jnp.int32)
    csum = jnp.cumsum(onehot, axis=0)
    counts = csum[-1]
    padded = -(-counts // tm) * tm
    ends = jnp.cumsum(padded)
    dest = jnp.sum(onehot * ((ends - padded)[None, :] + csum - 1), axis=1)
    n_tiles = (m * TOP_K) // tm + e
    tile_start = jnp.arange(n_tiles, dtype=jnp.int32) * tm
    tile_expert = jnp.minimum(jnp.sum((ends[None, :] <= tile_start[:, None]).astype(jnp.int32), axis=1), e - 1)
    return gates, dest.astype(jnp.int32), tile_expert, (ends[-1:] // tm).astype(jnp.int32)


def kernel(x_prompt, x_sample, cache_k_e, cache_v_e, state_s_e, state_conv_o, state_h_o, cache_k_o, cache_v_o,
           cache_logf_o, page_table, w_in_e, lb_logits, gnorm_a, w_out_e, ffn_w1, ffn_w3, ffn_w2, w_in_o, conv_w,
           conv_b, rg_wa, rg_ba, rg_wx, rg_bx, rg_lambda, fox_bf, w_out_o, moe_router, moe_w1, moe_w3, moe_w2,
           norm_mix, norm_ffn, norm_final):
    batch, seq, d = x_prompt.shape
    n_seq, ts, _ = x_sample.shape
    mp, ms = batch * seq, n_seq * ts
    m = mp + ms
    n_pages = page_table.shape[1]
    past = n_pages * PAGE_SIZE
    chunk_a = CHUNK_A
    assert d == D_MODEL and ts <= SUBLANES and seq % chunk_a == 0 and m % chunk_a == 0

    x = (x_prompt.reshape(mp, d), x_sample.reshape(ms, d))
    pos_p = jnp.arange(seq, dtype=jnp.int32)
    pos_s = past + jnp.arange(ts, dtype=jnp.int32)

    def seq_rows(a):
        return a.reshape(n_seq, ts, a.shape[1])

    proj = _matmul([x], w_in_e[0].astype(BF16), norm_g=norm_mix[0])
    proj_s = proj[mp:]
    lb = jnp.cumsum(jax.nn.softmax(lb_logits.astype(F32), axis=0), axis=0)[0]
    oa_p, s_p = _hgrn2(proj.reshape(m // chunk_a, chunk_a, proj.shape[1]), batch, seq // chunk_a,
                       jnp.zeros((batch, H_A, DK_A, DV_A), F32), lb, gnorm_a[0])
    oa_s, s_s = _hgrn2(proj_s.reshape(n_seq, ts, proj.shape[1]), n_seq, 1, state_s_e[0], lb, gnorm_a[0],
                       group=8 if n_seq % 8 == 0 else 1)
    cos_p, sin_p = _rope_tables(pos_p)
    cos_s, sin_s = _rope_tables(pos_s)
    ke_p, ve_p, qa, ka, vt = _moba_prep(proj, batch, seq, cos_p, sin_p)
    ob_p = _flash(qa, ka, vt)
    ve_s = proj_s[:, COLS_E_Q + 2 * W_B:COLS_E_Q + 3 * W_B]
    ke_s, ob_s = _moba_sample(seq_rows(proj_s[:, COLS_E_Q:COLS_E_Q + W_B]),
                              seq_rows(proj_s[:, COLS_E_Q + W_B:COLS_E_Q + 2 * W_B]), seq_rows(ve_s),
                              cache_k_e[0], cache_v_e[0], page_table, cos_s, sin_s)
    y = _matmul([(oa_p, oa_s), (ob_p, ob_s.reshape(ms, W_B))], w_out_e[0].astype(BF16), res=x)
    y = _ffn(y, ffn_w1.astype(BF16), ffn_w3.astype(BF16), ffn_w2.astype(BF16), residual=True, norm_g=norm_ffn[0],
             tf=ffn_w1.shape[2] // 2)

    cols_o = w_in_o.shape[2]
    cols_pad = -(-cols_o // LANES) * LANES
    w_in_o_p = jnp.zeros((d, cols_pad), BF16).at[:, :cols_o].set(w_in_o[0].astype(BF16))
    proj_o = _matmul([y], w_in_o_p, norm_g=norm_mix[1])
    proj_os = proj_o[mp:]
    ow = _odd_weights(conv_w[0], conv_b[0], rg_wa[0], rg_ba[0], rg_wx[0], rg_bx[0], rg_lambda[0])
    oc_p, buf_p, h_p = _rglru_prompt(proj_o, batch, seq, *ow)
    ps = proj_os.reshape(n_seq, ts, cols_pad)
    cx_s = ps[..., :W_C]
    oc_s, h_s = _rglru_sample(cx_s.swapaxes(0, 1), ps[..., W_C:2 * W_C].swapaxes(0, 1), state_conv_o[0].swapaxes(0, 1),
                              state_h_o[0], past == 0, *ow)
    oc_s = oc_s.swapaxes(0, 1).reshape(ms, W_C)
    buf_s = jnp.concatenate([state_conv_o[0].astype(F32), cx_s], axis=1)[:, ts:]
    bf_row = jnp.zeros((1, LANES), F32).at[0, :H_D].set(fox_bf[0].astype(F32))
    selq, selk = _fox_selectors()
    lf_p, ko_p, vo_p, qa, ka, vt = _fox_prep(proj_o, batch, seq, bf_row, selq, selk)
    od_p = _flash(qa, ka, vt)
    ko_s = proj_os[:, COLS_O_Q + W_D:COLS_O_Q + 2 * W_D]
    vo_s = proj_os[:, COLS_O_Q + 2 * W_D:COLS_O_Q + 3 * W_D]
    lf_s, od_s = _fox_sample(seq_rows(proj_os[:, COLS_O_Q:COLS_O_Q + W_D]), seq_rows(ko_s), seq_rows(vo_s),
                             proj_os[:, COLS_O_F:COLS_O_F + H_D].reshape(n_seq, ts * H_D), fox_bf[0],
                             cache_k_o[0], cache_v_o[0], cache_logf_o[0], page_table)
    y = _matmul([(oc_p, oc_s), (od_p, od_s.reshape(ms, W_D))], w_out_o[0].astype(BF16), res=y)
    tm_e = TM_EXPERT if (m * TOP_K) % TM_EXPERT == 0 else TM_DENSE
    hn, logits = _rmsnorm_router(y, norm_ffn[1], moe_router[0])
    gates, dest, tile_expert, n_used = _moe_dispatch(logits[:, :N_EXPERTS], tm_e)
    pair_pos = dest.reshape(m, TOP_K)
    y_e = _ffn(_scatter_rows(hn, dest, tile_expert.shape[0] * tm_e), moe_w1[0], moe_w3[0], moe_w2[0],
               tile_expert=tile_expert, n_used=n_used, tm=tm_e, tf=512)
    out_p, out_s = _combine_norm(y, jnp.take(y_e, pair_pos[:, 0], axis=0, mode="clip"),
                                 jnp.take(y_e, pair_pos[:, 1], axis=0, mode="clip"), gates, norm_final, mp)

    def heads(a, h):
        return a.reshape(1, n_seq, ts, h, a.shape[-1] // h)

    def heads_t(a):
        return jnp.transpose(a, (0, 3, 1, 2))[None]

    return (out_p.reshape(batch, seq, d), out_s.reshape(n_seq, ts, d),
            s_p[None], s_s[None],
            heads_t(ke_p), heads_t(ve_p), heads(ke_s, H_B), heads(ve_s, H_B),
            buf_p[None], buf_s[None], h_p[None], h_s[None],
            heads_t(ko_p), heads_t(vo_p), jnp.transpose(lf_p, (0, 2, 1))[None],
            heads(ko_s, H_D), heads(vo_s, H_D), lf_s.reshape(1, n_seq, ts, H_D))
```

```python
import functools
import math

import numpy as np
import jax
import jax.numpy as jnp
from jax import lax
from jax.experimental import pallas as pl
from jax.experimental.pallas import tpu as pltpu

F32 = jnp.float32
BF16 = jnp.bfloat16

D_MODEL = 1024
PAGE_SIZE = 128
H_A, DK_A, DV_A = 4, 128, 128
F_A, W_A = H_A * DK_A, H_A * DV_A
H_B, HD_B = 8, 64
W_B = H_B * HD_B
MOBA_BLOCK, MOBA_TOPK = 256, 3
W_C, NB_C, CONV_W, RG_C = 512, 8, 4, 8.0
BW_C = W_C // NB_C
H_D, HD_D = 8, 64
W_D = H_D * HD_D
N_EXPERTS, TOP_K = 8, 2
ROPE_THETA = 10000.0
EPS = 1e-6
NEG_INF = -1e30
LOG2E = math.log2(math.e)

COLS_E_Q = 2 * F_A + 2 * W_A
COLS_O_Q = 2 * W_C
COLS_O_F = 2 * W_C + 3 * W_D

LANES = 128
SUBLANES = 8
VMEM_LIMIT = 56 * 1024 * 1024
ATT_TILE = 256
TM_DENSE = 512
TM_EXPERT = 768
CHUNK_A = 128


def _cparams(*sem):
    return pltpu.CompilerParams(dimension_semantics=sem, vmem_limit_bytes=VMEM_LIMIT)


def _split3(x):
    hi = x.astype(BF16)
    r1 = x - hi.astype(F32)
    mid = r1.astype(BF16)
    lo = (r1 - mid.astype(F32)).astype(BF16)
    return hi, mid, lo


def _dot(a, b):
    return jnp.dot(a, b, preferred_element_type=F32)


def _dot_nt(a, b):
    return lax.dot_general(a, b, (((1,), (1,)), ((), ())), preferred_element_type=F32)


def _dot_tn(a, b):
    return lax.dot_general(a, b, (((0,), (0,)), ((), ())), preferred_element_type=F32)


def _dot3(a, b):
    b0, b1, b2 = _split3(b)
    return _dot(a, b0) + _dot(a, b1) + _dot(a, b2)


def _sigmoid(x):
    return 1.0 / (1.0 + jnp.exp(-x))


def _log_sigmoid(x):
    return jnp.minimum(x, 0.0) - jnp.log(1.0 + jnp.exp(-jnp.abs(x)))


def _gelu_tanh(x):
    return 0.5 * x * (1.0 + jnp.tanh(math.sqrt(2.0 / math.pi) * (x + 0.044715 * (x * x * x))))


def _rms(x, g):
    return x * lax.rsqrt(jnp.mean(x * x, axis=-1, keepdims=True) + EPS) * g


def _matmul_kernel(*refs, parts, n_prompt, has_norm, has_res):
    i = pl.program_id(0)
    refs = list(refs)
    g_ref = refs.pop(0) if has_norm else None
    xs = []
    for split in parts:
        if split:
            p_ref, s_ref = refs.pop(0), refs.pop(0)
            xs.append(jnp.where(i < n_prompt, p_ref[...], s_ref[...]))
        else:
            xs.append(refs.pop(0)[...])
    w_ref = refs.pop(0)
    res = None
    if has_res == "split":
        p_ref, s_ref = refs.pop(0), refs.pop(0)
        res = jnp.where(i < n_prompt, p_ref[...], s_ref[...])
    elif has_res:
        res = refs.pop(0)[...]
    o_ref = refs.pop(0)
    acc = None
    k0 = 0
    for x in xs:
        if has_norm:
            x = _rms(x, g_ref[...])
        kk = x.shape[1]
        part = _dot(x.astype(BF16), w_ref[k0:k0 + kk, :])
        acc = part if acc is None else acc + part
        k0 += kk
    if res is not None:
        acc = acc + res
    o_ref[...] = acc


def _matmul(xs, w, res=None, norm_g=None, tm=TM_DENSE):
    kt, n = w.shape
    parts = tuple(isinstance(x, tuple) for x in xs)
    m = sum(a.shape[0] for a in xs[0]) if parts[0] else xs[0].shape[0]
    in_specs, args = [], []
    n_prompt = [0]

    def add(x):
        if isinstance(x, tuple):
            xp, xsm = x
            assert xp.shape[0] % tm == 0 and xsm.shape[0] % tm == 0 and xp.shape[0] + xsm.shape[0] == m
            n_p = n_prompt[0] = xp.shape[0] // tm
            in_specs.append(pl.BlockSpec((tm, xp.shape[1]), lambda i: (jnp.minimum(i, n_p - 1), 0)))
            in_specs.append(pl.BlockSpec((tm, xp.shape[1]), lambda i: (jnp.maximum(i - n_p, 0), 0)))
            args.extend([xp, xsm])
        else:
            in_specs.append(pl.BlockSpec((tm, x.shape[1]), lambda i: (i, 0)))
            args.append(x)

    if norm_g is not None:
        assert len(xs) == 1
        in_specs.append(pl.BlockSpec((1, kt), lambda i: (0, 0)))
        args.append(norm_g.reshape(1, kt).astype(F32))
    for x in xs:
        add(x)
    in_specs.append(pl.BlockSpec((kt, n), lambda i: (0, 0)))
    args.append(w)
    if res is not None:
        add(res)
    has_res = "split" if isinstance(res, tuple) else res is not None
    return pl.pallas_call(
        functools.partial(_matmul_kernel, parts=parts, n_prompt=n_prompt[0], has_norm=norm_g is not None,
                          has_res=has_res),
        grid=(m // tm,),
        in_specs=in_specs,
        out_specs=pl.BlockSpec((tm, n), lambda i: (i, 0)),
        out_shape=jax.ShapeDtypeStruct((m, n), F32),
        compiler_params=_cparams("parallel"),
        name="matmul",
    )(*args)


def _ffn_kernel(te_ref, nu_ref, x_ref, w1_ref, w3_ref, w2_ref, *rest, has_norm, has_res):
    del te_ref
    rest = list(rest)
    g_ref = rest.pop(0) if has_norm else None
    res_ref = x_ref if has_res else None
    o_ref, acc_ref, xb_ref = rest
    i, f = pl.program_id(0), pl.program_id(1)

    @pl.when(i < nu_ref[0])
    def _():
        @pl.when(f == 0)
        def _():
            x = x_ref[...]
            if has_norm:
                x = _rms(x, g_ref[...])
            xb_ref[...] = x.astype(BF16)
            acc_ref[...] = jnp.zeros_like(acc_ref)

        x = xb_ref[...]
        a = _dot(x, w1_ref[...].astype(BF16))
        b = _dot(x, w3_ref[...].astype(BF16))
        g = (a * _sigmoid(a) * b).astype(BF16)
        acc_ref[...] += _dot(g, w2_ref[...].astype(BF16))

        @pl.when(f == pl.num_programs(1) - 1)
        def _():
            out = acc_ref[...]
            if has_res:
                out = out + res_ref[...]
            o_ref[...] = out

    @pl.when((i >= nu_ref[0]) & (f == pl.num_programs(1) - 1))
    def _():
        o_ref[...] = jnp.zeros_like(o_ref)


def _ffn(x, w1, w3, w2, residual=False, norm_g=None, tile_expert=None, n_used=None, tm=TM_DENSE, tf=256):
    m, d = x.shape
    f = w1.shape[-1]
    nf = f // tf
    if tile_expert is None:
        tile_expert = jnp.zeros((m // tm,), jnp.int32)
    if n_used is None:
        n_used = jnp.full((1,), m // tm, jnp.int32)

    def fblk(i, j, nu):
        return jnp.where(i < nu[0], j, nf - 1)

    row = pl.BlockSpec((tm, d), lambda i, j, te, nu: (i, 0))
    in_specs = [
        row,
        pl.BlockSpec((None, d, tf), lambda i, j, te, nu: (te[i], 0, fblk(i, j, nu))),
        pl.BlockSpec((None, d, tf), lambda i, j, te, nu: (te[i], 0, fblk(i, j, nu))),
        pl.BlockSpec((None, tf, d), lambda i, j, te, nu: (te[i], fblk(i, j, nu), 0)),
    ]
    args = [x, w1, w3, w2]
    if norm_g is not None:
        in_specs.append(pl.BlockSpec((1, d), lambda i, j, te, nu: (0, 0)))
        args.append(norm_g.reshape(1, d).astype(F32))
    return pl.pallas_call(
        functools.partial(_ffn_kernel, has_norm=norm_g is not None, has_res=residual),
        grid_spec=pltpu.PrefetchScalarGridSpec(
            num_scalar_prefetch=2,
            grid=(m // tm, nf),
            in_specs=in_specs,
            out_specs=row,
            scratch_shapes=[pltpu.VMEM((tm, d), F32), pltpu.VMEM((tm, d), BF16)],
        ),
        out_shape=jax.ShapeDtypeStruct((m, d), F32),
        compiler_params=_cparams("arbitrary", "arbitrary"),
        name="ffn",
    )(tile_expert, n_used, *args)


def _hgrn2_head(q, k, v, logf, st, c_sub):
    c = q.shape[0]
    ns = c // c_sub
    row = lax.broadcasted_iota(jnp.int32, (c, c), 0)
    col = lax.broadcasted_iota(jnp.int32, (c, c), 1)
    tri = (row >= col).astype(BF16)
    g = _dot3(tri, logf)
    vb = v.astype(BF16)
    o = _dot_nt((q * jnp.exp(g)).astype(BF16), st.astype(BF16))

    lane_c = lax.broadcasted_iota(jnp.int32, (c_sub, c), 1)
    slabs = []
    for i in range(ns):
        r0 = i * c_sub
        qi, ki, gi = q[r0:r0 + c_sub], k[r0:r0 + c_sub], g[r0:r0 + c_sub]
        slab = jnp.zeros((c_sub, c), F32)
        for s in range(c_sub):
            w = jnp.exp(jnp.minimum(gi - gi[s:s + 1, :], 0.0))
            colv = jnp.sum(qi * w * ki[s:s + 1, :], axis=-1, keepdims=True)
            slab = jnp.where(lane_c == r0 + s, colv, slab)
        slabs.append(slab)
    scores = slabs[0] if ns == 1 else jnp.concatenate(slabs, axis=0)
    scores = jnp.where((row >= col) & (row // c_sub == col // c_sub), scores, 0.0)

    w = c // 2
    while w >= c_sub:
        q_parts, k_parts = [], []
        for b in range(c // w):
            r0 = b * w
            if b % 2 == 1:
                q_parts.append(q[r0:r0 + w] * jnp.exp(g[r0:r0 + w] - g[r0 - 1:r0, :]))
                k_parts.append(jnp.zeros((w, q.shape[1]), F32))
            else:
                q_parts.append(jnp.zeros((w, q.shape[1]), F32))
                k_parts.append(k[r0:r0 + w] * jnp.exp(g[r0 + w - 1:r0 + w, :] - g[r0:r0 + w]))
        lvl = _dot_nt(jnp.concatenate(q_parts, axis=0).astype(BF16), jnp.concatenate(k_parts, axis=0).astype(BF16))
        scores = scores + jnp.where(((row // w) % 2 == 1) & (col // w == row // w - 1), lvl, 0.0)
        w //= 2
    o = o + _dot(scores.astype(BF16), vb)
    gl = g[c - 1:c, :]
    ke = (k * jnp.exp(gl - g)).astype(BF16)
    st_new = jnp.exp(gl) * st + _dot_tn(vb, ke)
    return o, st_new


def _hgrn2_kernel(aq_ref, af_ref, ai_ref, ag_ref, lb_ref, gn_ref, s0_ref, o_ref, s_ref, st_scr, *, c_sub, ct):
    t = pl.program_id(1)
    group = aq_ref.shape[0]

    @pl.when(t == 0)
    def _():
        for gi in range(group):
            for h in range(H_A):
                st_scr[gi, h] = s0_ref[gi, h].T

    def padded(x):
        if ct < SUBLANES:
            x = jnp.concatenate([x, jnp.zeros((SUBLANES - ct, x.shape[1]), F32)], axis=0)
        return x

    lb = lb_ref[...]
    for gi in range(group):
        aq, zf, vi = padded(aq_ref[gi]), padded(af_ref[gi]), padded(ai_ref[gi])
        c = aq.shape[0]
        real = lax.broadcasted_iota(jnp.int32, (c, F_A), 0) < ct
        sig = _sigmoid(zf)
        q = aq * _sigmoid(aq)
        logf = jnp.where(real, jnp.log(lb + (1.0 - lb) * sig), 0.0)
        k = jnp.where(real, (1.0 - lb) * _sigmoid(-zf), 0.0)
        outs = []
        for h in range(H_A):
            sl = slice(h * DK_A, (h + 1) * DK_A)
            o_h, st_new = _hgrn2_head(q[:, sl], k[:, sl], vi[:, sl], logf[:, sl], st_scr[gi, h], c_sub)
            st_scr[gi, h] = st_new
            outs.append(o_h)
        o = _rms(jnp.concatenate(outs, axis=1)[:ct], gn_ref[...])
        o_ref[gi] = o * _sigmoid(ag_ref[gi])

    @pl.when(t == pl.num_programs(1) - 1)
    def _():
        for gi in range(group):
            for h in range(H_A):
                s_ref[gi, h] = st_scr[gi, h].T


def _hgrn2(x3, n_seq, n_chunks, s0, lb, gnorm, group=1):
    ct = x3.shape[1]
    assert n_seq % group == 0 and (group == 1 or n_chunks == 1)

    def tok_spec(cb):
        return pl.BlockSpec((group, ct, F_A), lambda n, t: (n * n_chunks + t, 0, cb))

    state = pl.BlockSpec((group, H_A, DK_A, DV_A), lambda n, t: (n, 0, 0, 0))
    o, s = pl.pallas_call(
        functools.partial(_hgrn2_kernel, c_sub=SUBLANES, ct=ct),
        grid=(n_seq // group, n_chunks),
        in_specs=[tok_spec(0), tok_spec(1), tok_spec(2), tok_spec(3),
                  pl.BlockSpec((1, F_A), lambda n, t: (0, 0)),
                  pl.BlockSpec((1, W_A), lambda n, t: (0, 0)), state],
        out_specs=[pl.BlockSpec((group, ct, W_A), lambda n, t: (n * n_chunks + t, 0, 0)), state],
        out_shape=[jax.ShapeDtypeStruct((n_seq * n_chunks, ct, W_A), F32),
                   jax.ShapeDtypeStruct((n_seq, H_A, DK_A, DV_A), F32)],
        scratch_shapes=[pltpu.VMEM((group, H_A, DV_A, DK_A), F32)],
        compiler_params=_cparams("parallel", "arbitrary"),
        name="hgrn2",
    )(x3, x3, x3, x3, lb.reshape(1, F_A), gnorm.reshape(1, W_A), s0)
    return o.reshape(n_seq * n_chunks * ct, W_A), s


def _rope_tables(pos):
    half = HD_B // 2
    inv = ROPE_THETA ** (-jnp.arange(half, dtype=F32) / half)
    ang = pos.astype(F32)[:, None] * inv[None, :]
    cos, sin = jnp.cos(ang), jnp.sin(ang)
    return jnp.concatenate([cos, cos], axis=1), jnp.concatenate([-sin, sin], axis=1)


def _rope(x, cos2, sin2, lane):
    swapped = jnp.where((lane & (HD_B - 1)) >= HD_B // 2, pltpu.roll(x, HD_B // 2, 1), pltpu.roll(x, LANES - HD_B // 2, 1))
    return x * cos2 + swapped * sin2


def _dot_nt_f32(a, b):
    a0, a1, a2 = _split3(a)
    b0, b1, b2 = _split3(b)
    return (_dot_nt(a0, b0) + (_dot_nt(a0, b1) + _dot_nt(a1, b0))
            + (_dot_nt(a0, b2) + _dot_nt(a1, b1) + _dot_nt(a2, b0)))


def _top_select(g, index, axis, n_pick):
    sel = jnp.zeros(g.shape, F32)
    big = jnp.int32(1 << 20)
    for _ in range(n_pick):
        m = jnp.max(g, axis=axis, keepdims=True)
        first = jnp.min(jnp.where(g == m, index, big), axis=axis, keepdims=True)
        pick = jnp.where((index == first) & (m > 0.5 * NEG_INF), 1.0, 0.0)
        sel = sel + pick
        g = jnp.where(pick > 0.5, NEG_INF, g)
    return sel


VT_ROWS = HD_B + 16


def _store_heads_t(refs, c, x):
    xt = x.T
    for ref in refs:
        for hh in range(2):
            part = xt[hh * HD_B:(hh + 1) * HD_B]
            extra = ref.shape[1] - HD_B
            if extra:
                r = lax.broadcasted_iota(jnp.int32, (extra, part.shape[1]), 0)
                part = jnp.concatenate([part, jnp.where(r == 0, 1.0, 0.0)], axis=0)
            ref[2 * c + hh] = part.astype(ref.dtype)


def _moba_prep_kernel(q_ref, k_ref, v_ref, cos_ref, sin_ref, kt_ref, vl_ref, qa_ref, ka_ref, vt_ref, mrow_scr):
    t = pl.program_id(1)
    rows = q_ref.shape[0]

    @pl.when(t == 0)
    def _():
        mrow_scr[...] = jnp.zeros_like(mrow_scr)

    lane = lax.broadcasted_iota(jnp.int32, (rows, LANES), 1)
    lane1 = lax.broadcasted_iota(jnp.int32, (1, LANES), 1)
    low = lane < HD_B
    blk = lane & (HD_B - 1)
    nbm = HD_B // 2
    brow = lax.broadcasted_iota(jnp.int32, (nbm, rows), 0)
    cos2, sin2 = cos_ref[...], sin_ref[...]
    scale = HD_B ** -0.5 * LOG2E
    for c in range(W_B // LANES):
        sl = slice(c * LANES, (c + 1) * LANES)
        qr = _rope(q_ref[:, sl], cos2, sin2, lane)
        kr = _rope(k_ref[:, sl], cos2, sin2, lane)
        _store_heads_t([kt_ref], c, kr)
        gate_t = _dot_nt_f32(mrow_scr[c], qr)
        halves = []
        for r0 in (0, HD_B):
            g = jnp.where(brow < t, gate_t[r0:r0 + nbm], NEG_INF)
            keep = (_top_select(g, brow, 0, MOBA_TOPK) > 0.5) | (brow == t)
            halves += [jnp.where(keep, 0.0, NEG_INF), jnp.full((HD_B - nbm, rows), NEG_INF, F32)]
        msel = jnp.concatenate(halves, axis=0).T
        own = jnp.where(blk == t, 1.0, 0.0)
        qs = qr * scale
        qa_ref[2 * c] = jnp.where(low, qs, msel).astype(BF16)
        qa_ref[2 * c + 1] = jnp.where(low, msel, qs).astype(BF16)
        ka_ref[2 * c] = jnp.where(low, kr, own).astype(BF16)
        ka_ref[2 * c + 1] = jnp.where(low, own, kr).astype(BF16)
        _store_heads_t([vl_ref, vt_ref], c, v_ref[:, sl])
        mean = jnp.mean(kr, axis=0, keepdims=True)
        mrow_scr[c, pl.ds(HD_B + t, 1), :] = jnp.where(lane1 < HD_B, mean, 0.0)
        mrow_scr[c, pl.ds(t, 1), :] = jnp.where(lane1 < HD_B, 0.0, mean)


def _pair_specs(batch, heads, seq):
    nt = seq // ATT_TILE
    pair = pl.BlockSpec((None, heads, ATT_TILE, LANES), lambda b, t: (b, 0, t, 0))
    pair_shape = jax.ShapeDtypeStruct((batch, heads, seq, LANES), BF16)
    vt = pl.BlockSpec((None, heads, None, VT_ROWS, ATT_TILE), lambda b, t: (b, 0, t, 0, 0))
    vt_shape = jax.ShapeDtypeStruct((batch, heads, nt, VT_ROWS, ATT_TILE), BF16)
    leaf = pl.BlockSpec((None, heads, LANES // 2, ATT_TILE), lambda b, t: (b, 0, 0, t))
    leaf_shape = jax.ShapeDtypeStruct((batch, heads, LANES // 2, seq), F32)
    return [leaf, leaf, pair, pair, vt], [leaf_shape, leaf_shape, pair_shape, pair_shape, vt_shape]


def _moba_prep(proj, batch, seq, cos2, sin2):
    nb = seq // MOBA_BLOCK
    assert seq % MOBA_BLOCK == 0 and nb <= HD_B // 2 and MOBA_BLOCK == ATT_TILE
    rows = MOBA_BLOCK

    def tok(cb):
        return pl.BlockSpec((rows, W_B), lambda b, t: (b * nb + t, cb))

    specs, shapes = _pair_specs(batch, H_B, seq)
    cb0 = COLS_E_Q // W_B
    return pl.pallas_call(
        _moba_prep_kernel,
        grid=(batch, nb),
        in_specs=[tok(cb0), tok(cb0 + 1), tok(cb0 + 2),
                  pl.BlockSpec((rows, LANES), lambda b, t: (t, 0)),
                  pl.BlockSpec((rows, LANES), lambda b, t: (t, 0))],
        out_specs=specs,
        out_shape=shapes,
        scratch_shapes=[pltpu.VMEM((W_B // LANES, LANES, LANES), F32)],
        compiler_params=_cparams("parallel", "arbitrary"),
        name="moba_prep",
    )(proj, proj, proj, jnp.concatenate([cos2, cos2], axis=1), jnp.concatenate([sin2, sin2], axis=1))


def _flash_kernel(qa_ref, ka_ref, vt_ref, o_ref, s0_scr, s1_scr, acc_scr):
    i = pl.program_id(2)
    tq = qa_ref.shape[1]
    tv = vt_ref.shape[3]
    sub = tq // tv
    qs = [qa_ref[0], qa_ref[1]]

    def produce(j, s_ref):
        for hh in range(2):
            k = ka_ref[hh, pl.ds(pl.multiple_of(j * tq, tq), tq), :]
            s_ref[hh] = _dot_nt(k, qs[hh])

    def fold(x, op):
        return op(x.reshape(tq // SUBLANES, SUBLANES, tq), axis=0)

    def consume(j, s_ref, carry, diag=False):
        new = []
        for hh in range(2):
            m = carry[hh]
            if diag:
                krow = lax.broadcasted_iota(jnp.int32, (tq, tq), 0)
                qcol = lax.broadcasted_iota(jnp.int32, (tq, tq), 1)
                s_ref[hh] = jnp.where(krow <= qcol, s_ref[hh], NEG_INF)
            m_new = jnp.maximum(m, jnp.max(fold(s_ref[hh], jnp.max), axis=0, keepdims=True))
            pb = jnp.exp2(s_ref[hh] - m_new).astype(BF16)
            pv = _dot(vt_ref[hh, j * sub], pb[:tv])
            for u in range(1, sub):
                pv = pv + _dot(vt_ref[hh, j * sub + u], pb[u * tv:(u + 1) * tv])
            acc_scr[hh] = jnp.exp2(m - m_new) * acc_scr[hh] + pv
            new.append(m_new)
        return tuple(new)

    def pair(u, carry):
        j = 2 * u
        produce(j + 1, s1_scr)
        carry = consume(j, s0_scr, carry)
        produce(j + 2, s0_scr)
        return consume(j + 1, s1_scr, carry)

    def odd_tail(carry):
        produce(i, s1_scr)
        carry = consume(i - 1, s0_scr, carry)
        return consume(i, s1_scr, carry, diag=True)

    def even_tail(carry):
        return consume(i, s0_scr, carry, diag=True)

    acc_scr[...] = jnp.zeros_like(acc_scr)
    produce(0, s0_scr)
    init = tuple(jnp.full((1, tq), -jnp.inf, F32) for _ in range(2))
    carry = lax.fori_loop(0, i // 2, pair, init)
    lax.cond(i % 2 == 1, odd_tail, even_tail, carry)
    ot = jnp.concatenate([acc_scr[hh, :HD_B] / acc_scr[hh, HD_B:HD_B + 1] for hh in range(2)], axis=0)
    o_ref[...] = ot.T


def _flash(qa, ka, vt, tq=512):
    batch, heads, seq, _ = qa.shape
    tq = min(tq, seq)
    nq = seq // tq
    nv, hd, tv = vt.shape[2:]
    return pl.pallas_call(
        _flash_kernel,
        grid=(batch, heads // 2, nq),
        in_specs=[pl.BlockSpec((None, 2, tq, LANES), lambda b, c, i: (b, c, i, 0)),
                  pl.BlockSpec((None, 2, seq, LANES), lambda b, c, i: (b, c, 0, 0)),
                  pl.BlockSpec((None, 2, nv, hd, tv), lambda b, c, i: (b, c, 0, 0, 0))],
        out_specs=pl.BlockSpec((tq, LANES), lambda b, c, i: (b * nq + i, c)),
        out_shape=jax.ShapeDtypeStruct((batch * seq, heads * LANES // 2), F32),
        scratch_shapes=[pltpu.VMEM((2, tq, tq), F32), pltpu.VMEM((2, tq, tq), F32), pltpu.VMEM((2, hd, tq), F32)],
        compiler_params=_cparams("parallel", "parallel", "arbitrary"),
        name="flash",
    )(qa, ka, vt)


SAMPLE_GROUP = 2


def _paged_specs(n_pages, tail, group):
    zeros = (0,) * len(tail)
    return [pl.BlockSpec((None,) + tail, functools.partial(lambda n, pt, g, p: (pt[n * group + g, p],) + zeros, g=g, p=p))
            for g in range(group) for p in range(n_pages)]


def _group_size(n_seq):
    return SAMPLE_GROUP if n_seq % SAMPLE_GROUP == 0 else 1


def _pool_pages(pool):
    n_pool, page, h, dd = pool.shape
    return jnp.transpose(pool, (0, 2, 3, 1)).reshape(n_pool, h * dd, page)


def _pad_rows(x, rows):
    return jnp.concatenate([x, jnp.zeros((rows - x.shape[0], x.shape[1]), x.dtype)], axis=0)


def _query_rows(q, lane, rowh):
    t, w = q.shape
    rep = jnp.broadcast_to(q[:, None, :], (t, H_B, w)).reshape(t * H_B, w)
    return jnp.where((lane // HD_B) == rowh, rep, 0.0)


def _head_rows_to_tokens(o, lane, rowh, t):
    o = jnp.where((lane // HD_B) == rowh, o, 0.0)
    return jnp.sum(o.reshape(t, H_B, o.shape[1]), axis=1)


def _sample_softmax_pv(s_new, s_pages, v_new, vp, o_ref, lane, rowh, t):
    m = jnp.max(s_new, axis=1, keepdims=True)
    mm = s_pages[0]
    for s in s_pages[1:]:
        mm = jnp.maximum(mm, s)
    m = jnp.maximum(m, jnp.max(mm, axis=1, keepdims=True))
    p_new = jnp.exp(s_new - m)
    o = _dot(p_new.astype(BF16), v_new.astype(BF16))
    lsum = None
    for p, s in enumerate(s_pages):
        pp = jnp.exp(s - m)
        lsum = pp if lsum is None else lsum + pp
        o = o + _dot_nt(pp.astype(BF16), vp[p][...].astype(BF16))
    l = jnp.sum(p_new, axis=1, keepdims=True) + jnp.sum(lsum, axis=1, keepdims=True)
    o_ref[...] = _head_rows_to_tokens(o / l, lane, rowh, t)


def _moba_sample_kernel(pt_ref, q_ref, k_ref, v_ref, cos_ref, sin_ref, *rest, n_pages):
    del pt_ref
    group = q_ref.shape[0]
    ke_ref, o_ref = rest[2 * group * n_pages:]
    for g in range(group):
        kp = rest[g * n_pages:(g + 1) * n_pages]
        vp = rest[(group + g) * n_pages:(group + g + 1) * n_pages]
        _moba_sample_one(q_ref.at[g], k_ref.at[g], v_ref.at[g], cos_ref, sin_ref, kp, vp, ke_ref.at[g], o_ref.at[g], n_pages)


def _moba_sample_one(q_ref, k_ref, v_ref, cos_ref, sin_ref, kp, vp, ke_ref, o_ref, n_pages):
    t = q_ref.shape[0]
    nc = t * H_B
    nbp = n_pages * PAGE_SIZE // MOBA_BLOCK
    ppb = MOBA_BLOCK // PAGE_SIZE
    lane8 = lax.broadcasted_iota(jnp.int32, (SUBLANES, LANES), 1)
    cos2, sin2 = _pad_rows(cos_ref[...], SUBLANES), _pad_rows(sin_ref[...], SUBLANES)
    q8, k8, v8 = _pad_rows(q_ref[...], SUBLANES), _pad_rows(k_ref[...], SUBLANES), _pad_rows(v_ref[...], SUBLANES)
    qr = jnp.concatenate([_rope(q8[:, c * LANES:(c + 1) * LANES], cos2, sin2, lane8) for c in range(W_B // LANES)], axis=1)
    kr = jnp.concatenate([_rope(k8[:, c * LANES:(c + 1) * LANES], cos2, sin2, lane8) for c in range(W_B // LANES)], axis=1)
    ke_ref[...] = kr[:t]

    lane = lax.broadcasted_iota(jnp.int32, (nc, W_B), 1)
    rowh = lax.broadcasted_iota(jnp.int32, (nc, W_B), 0) % H_B
    qrows = _query_rows(qr[:t], lane, rowh)

    lanem = lax.broadcasted_iota(jnp.int32, (W_B, LANES), 1)
    means = jnp.zeros((W_B, LANES), F32)
    for j in range(nbp):
        blk_sum = kp[ppb * j][...]
        for u in range(1, ppb):
            blk_sum = blk_sum + kp[ppb * j + u][...]
        means = jnp.where(lanem == j, jnp.sum(blk_sum, axis=1, keepdims=True) * (1.0 / MOBA_BLOCK), means)
    q0, q1, q2 = _split3(qrows)
    m0, m1, m2 = _split3(means)
    gate = _dot(q0, m0) + (_dot(q0, m1) + _dot(q1, m0)) + (_dot(q0, m2) + _dot(q1, m1) + _dot(q2, m0))
    lg = lax.broadcasted_iota(jnp.int32, (nc, LANES), 1)
    gm = jnp.where(lg < nbp, gate, NEG_INF)
    rank = jnp.zeros((nc, LANES), F32)
    for sft in range(1, nbp):
        before = pltpu.roll(gm, sft, 1)
        after = pltpu.roll(gm, LANES - sft, 1)
        rank = rank + jnp.where((before >= gm) & (lg >= sft), 1.0, 0.0) + jnp.where(after > gm, 1.0, 0.0)
    sel = jnp.where((rank < MOBA_TOPK - 0.5) & (lg < nbp), 1.0, 0.0)

    qb = (qrows * (HD_B ** -0.5)).astype(BF16)
    s_pages = []
    for p in range(n_pages):
        s = _dot(qb, kp[p][...].astype(BF16))
        j = p // ppb
        s_pages.append(jnp.where(sel[:, j:j + 1] > 0.5, s, NEG_INF))
    s_new = _dot_nt(qb, kr.astype(BF16))
    ktok = lax.broadcasted_iota(jnp.int32, (nc, SUBLANES), 1)
    qtok = lax.broadcasted_iota(jnp.int32, (nc, SUBLANES), 0) // H_B
    s_new = jnp.where((ktok <= qtok) & (ktok < t), s_new, NEG_INF)
    _sample_softmax_pv(s_new, s_pages, v8, vp, o_ref, lane, rowh, t)


def _moba_sample(q, k, v, pool_k, pool_v, page_table, cos2, sin2):
    n_seq, t, w = q.shape
    n_pages = page_table.shape[1]
    assert (n_pages * PAGE_SIZE) % MOBA_BLOCK == 0
    group = _group_size(n_seq)
    row = pl.BlockSpec((group, t, w), lambda n, pt: (n, 0, 0))
    tab = pl.BlockSpec((t, LANES), lambda n, pt: (0, 0))
    page = (w, PAGE_SIZE)
    return pl.pallas_call(
        functools.partial(_moba_sample_kernel, n_pages=n_pages),
        grid_spec=pltpu.PrefetchScalarGridSpec(
            num_scalar_prefetch=1,
            grid=(n_seq // group,),
            in_specs=[row, row, row, tab, tab] + _paged_specs(n_pages, page, group) + _paged_specs(n_pages, page, group),
            out_specs=[row, row],
        ),
        out_shape=[jax.ShapeDtypeStruct((n_seq, t, w), F32)] * 2,
        compiler_params=_cparams("arbitrary"),
        name="moba_sample",
    )(page_table, q, k, v, jnp.concatenate([cos2, cos2], axis=1), jnp.concatenate([sin2, sin2], axis=1),
      *([_pool_pages(pool_k)] * (group * n_pages)), *([_pool_pages(pool_v)] * (group * n_pages)))


def _fox_sample_kernel(pt_ref, q_ref, k_ref, v_ref, df_ref, bf_ref, *rest, n_pages):
    del pt_ref
    group = q_ref.shape[0]
    lf_ref, o_ref = rest[3 * group * n_pages:]
    for g in range(group):
        kp, vp, lp = (rest[(kind * group + g) * n_pages:(kind * group + g + 1) * n_pages] for kind in range(3))
        _fox_sample_one(q_ref.at[g], k_ref.at[g], v_ref.at[g], df_ref.at[g], bf_ref, kp, vp, lp, lf_ref.at[g], o_ref.at[g],
                        n_pages)


def _fox_sample_one(q_ref, k_ref, v_ref, df_ref, bf_ref, kp, vp, lp, lf_ref, o_ref, n_pages):
    t = q_ref.shape[0]
    nc = t * H_D
    k8, v8 = _pad_rows(k_ref[...], SUBLANES), _pad_rows(v_ref[...], SUBLANES)
    lane = lax.broadcasted_iota(jnp.int32, (nc, W_D), 1)
    rowh = lax.broadcasted_iota(jnp.int32, (nc, W_D), 0) % H_D
    qb = (_query_rows(q_ref[...], lane, rowh) * (HD_D ** -0.5)).astype(BF16)

    lf_row = _log_sigmoid(df_ref[...] + bf_ref[...])
    lf_ref[...] = lf_row
    rr = lax.broadcasted_iota(jnp.int32, (nc, nc), 0)
    cc = lax.broadcasted_iota(jnp.int32, (nc, nc), 1)
    same = (rr % H_D) == (cc % H_D)
    cn_col = jnp.sum(jnp.where(same & (cc // H_D <= rr // H_D), lf_row, 0.0), axis=1, keepdims=True)
    ktok = lax.broadcasted_iota(jnp.int32, (nc, SUBLANES), 1)
    qtok = lax.broadcasted_iota(jnp.int32, (nc, SUBLANES), 0) // H_D
    cn_keys = jnp.zeros((nc, SUBLANES), F32)
    for tp in range(t):
        col = jnp.sum(jnp.where(same & (cc // H_D <= tp), lf_row, 0.0), axis=1, keepdims=True)
        cn_keys = jnp.where(ktok == tp, col, cn_keys)

    lfa = jnp.concatenate([lp[p][...] for p in range(n_pages)], axis=0)
    nr = lfa.shape[0]
    lane_r = lax.broadcasted_iota(jnp.int32, (nr, LANES), 1)
    incl = lfa
    sh = 1
    while sh < LANES:
        incl = incl + jnp.where(lane_r < LANES - sh, pltpu.roll(incl, LANES - sh, 1), 0.0)
        sh *= 2
    r2 = lax.broadcasted_iota(jnp.int32, (nr, nr), 0)
    c2 = lax.broadcasted_iota(jnp.int32, (nr, nr), 1)
    later = ((c2 > r2) & ((c2 % H_D) == (r2 % H_D))).astype(BF16)
    suf = (incl - lfa) + _dot3(later, jnp.broadcast_to(incl[:, 0:1], (nr, LANES)))

    s_pages = []
    for p in range(n_pages):
        bias = jnp.concatenate([suf[H_D * p:H_D * (p + 1)]] * t, axis=0)
        s_pages.append(_dot(qb, kp[p][...].astype(BF16)) + bias + cn_col)
    s_new = _dot_nt(qb, k8.astype(BF16)) + (cn_col - cn_keys)
    s_new = jnp.where((ktok <= qtok) & (ktok < t), s_new, NEG_INF)
    _sample_softmax_pv(s_new, s_pages, v8, vp, o_ref, lane, rowh, t)


def _fox_sample(q, k, v, df, bf, pool_k, pool_v, pool_lf, page_table):
    n_seq, t, w = q.shape
    nc = t * H_D
    n_pages = page_table.shape[1]
    bft = jnp.tile(bf.astype(F32), t)
    group = _group_size(n_seq)
    row = pl.BlockSpec((group, t, w), lambda n, pt: (n, 0, 0))
    frow = pl.BlockSpec((group, 1, nc), lambda n, pt: (n, 0, 0))
    page = (w, PAGE_SIZE)
    lf, o = pl.pallas_call(
        functools.partial(_fox_sample_kernel, n_pages=n_pages),
        grid_spec=pltpu.PrefetchScalarGridSpec(
            num_scalar_prefetch=1,
            grid=(n_seq // group,),
            in_specs=[row, row, row, frow, pl.BlockSpec((1, nc), lambda n, pt: (0, 0))]
            + _paged_specs(n_pages, page, group) + _paged_specs(n_pages, page, group)
            + _paged_specs(n_pages, (H_D, PAGE_SIZE), group),
            out_specs=[frow, row],
        ),
        out_shape=[jax.ShapeDtypeStruct((n_seq, 1, nc), F32), jax.ShapeDtypeStruct((n_seq, t, w), F32)],
        compiler_params=_cparams("arbitrary"),
        name="fox_sample",
    )(page_table, q, k, v, df.reshape(n_seq, 1, nc), bft.reshape(1, nc),
      *([_pool_pages(pool_k)] * (group * n_pages)), *([_pool_pages(pool_v)] * (group * n_pages)),
      *([jnp.transpose(pool_lf, (0, 2, 1))] * (group * n_pages)))
    return lf, o


def _shift_rows(x, d, fill):
    return jnp.concatenate([jnp.full((d, x.shape[1]), fill, x.dtype), x[:-d]], axis=0)


def _rglru_gates(xc, wa_ref, ba_ref, wx_ref, bx_ref, sp_ref, first_row_pos0):
    xb = xc.astype(BF16)
    r = _sigmoid(_dot(xb, wa_ref[...]) + ba_ref[...])
    i = _sigmoid(_dot(xb, wx_ref[...]) + bx_ref[...])
    log_a = -RG_C * r * sp_ref[...]
    a = jnp.exp(log_a)
    mult = jnp.sqrt(1.0 - jnp.exp(2.0 * log_a))
    if first_row_pos0 is not None:
        mult = jnp.where(first_row_pos0, 1.0, mult)
    return a, xc * i * mult


def _rglru_kernel(cx_ref, cg_ref, cw_ref, cb_ref, wa_ref, ba_ref, wx_ref, bx_ref, sp_ref,
                  o_ref, buf_ref, h_ref, tail_scr, h_scr):
    t = pl.program_id(1)
    tt = cx_ref.shape[0]

    @pl.when(t == 0)
    def _():
        tail_scr[...] = jnp.zeros_like(tail_scr)
        h_scr[...] = jnp.zeros_like(h_scr)

    cx = cx_ref[...]
    ext = jnp.concatenate([tail_scr[...], cx], axis=0)
    xc = cb_ref[...] + cx * cw_ref[CONV_W - 1:CONV_W, :]
    for d in range(1, CONV_W):
        xc = xc + ext[SUBLANES - d:SUBLANES - d + tt] * cw_ref[CONV_W - 1 - d:CONV_W - d, :]
    row = lax.broadcasted_iota(jnp.int32, (tt, W_C), 0)
    a, b = _rglru_gates(xc, wa_ref, ba_ref, wx_ref, bx_ref, sp_ref, (row == 0) & (t == 0))
    d = 1
    while d < tt:
        b = b + a * _shift_rows(b, d, 0.0)
        a = a * _shift_rows(a, d, 1.0)
        d *= 2
    h = a * h_scr[0:1, :] + b
    o_ref[...] = h * _gelu_tanh(cg_ref[...])
    h_scr[0:1, :] = h[tt - 1:tt, :]
    tail_scr[...] = cx[tt - SUBLANES:tt, :]

    @pl.when(t == pl.num_programs(1) - 1)
    def _():
        buf_ref[...] = cx[tt - (CONV_W - 1):tt, :]
        h_ref[...] = h[tt - 1:tt, :]


def _rglru_prompt(proj, batch, seq, cw, cb, wa, ba, wx, bx, sp, tt=256):
    nt = seq // tt
    vec = pl.BlockSpec((1, W_C), lambda b, t: (0, 0))
    mat = pl.BlockSpec((W_C, W_C), lambda b, t: (0, 0))
    o, buf, h = pl.pallas_call(
        _rglru_kernel,
        grid=(batch, nt),
        in_specs=[pl.BlockSpec((tt, W_C), lambda b, t: (b * nt + t, 0)),
                  pl.BlockSpec((tt, W_C), lambda b, t: (b * nt + t, 1)),
                  pl.BlockSpec((CONV_W, W_C), lambda b, t: (0, 0)), vec, mat, vec, mat, vec, vec],
        out_specs=[pl.BlockSpec((tt, W_C), lambda b, t: (b * nt + t, 0)),
                   pl.BlockSpec((None, CONV_W - 1, W_C), lambda b, t: (b, 0, 0)),
                   pl.BlockSpec((None, 1, W_C), lambda b, t: (b, 0, 0))],
        out_shape=[jax.ShapeDtypeStruct((batch * seq, W_C), F32),
                   jax.ShapeDtypeStruct((batch, CONV_W - 1, W_C), F32),
                   jax.ShapeDtypeStruct((batch, 1, W_C), F32)],
        scratch_shapes=[pltpu.VMEM((SUBLANES, W_C), F32), pltpu.VMEM((SUBLANES, W_C), F32)],
        compiler_params=_cparams("parallel", "arbitrary"),
        name="rglru",
    )(proj, proj, cw, cb, wa, ba, wx, bx, sp)
    return o, buf, h.reshape(batch, W_C)


def _rglru_sample_kernel(cx_ref, cg_ref, buf_ref, h0_ref, cw_ref, cb_ref, wa_ref, ba_ref, wx_ref, bx_ref, sp_ref,
                         o_ref, h_ref, *, pos0_is_zero):
    t, n, _ = cx_ref.shape
    xp = [buf_ref[j] for j in range(CONV_W - 1)] + [cx_ref[j] for j in range(t)]
    xcs = []
    for s in range(t):
        xc = cb_ref[...] + xp[s] * cw_ref[0:1, :]
        for j in range(1, CONV_W):
            xc = xc + xp[s + j] * cw_ref[j:j + 1, :]
        xcs.append(xc)
    xc = jnp.concatenate(xcs, axis=0)
    first = (lax.broadcasted_iota(jnp.int32, xc.shape, 0) < n) if pos0_is_zero else None
    a, b = _rglru_gates(xc, wa_ref, ba_ref, wx_ref, bx_ref, sp_ref, first)
    h = h0_ref[...]
    for s in range(t):
        h = a[s * n:(s + 1) * n] * h + b[s * n:(s + 1) * n]
        o_ref[s] = h * _gelu_tanh(cg_ref[s])
    h_ref[...] = h


def _rglru_sample(cx, cg, buf, h0, pos0_is_zero, cw, cb, wa, ba, wx, bx, sp):
    t, n, _ = cx.shape
    return pl.pallas_call(
        functools.partial(_rglru_sample_kernel, pos0_is_zero=pos0_is_zero),
        out_shape=[jax.ShapeDtypeStruct((t, n, W_C), F32), jax.ShapeDtypeStruct((n, W_C), F32)],
        compiler_params=pltpu.CompilerParams(vmem_limit_bytes=VMEM_LIMIT),
        name="rglru_sample",
    )(cx, cg, buf, h0, cw, cb, wa, ba, wx, bx, sp)


def _odd_weights(cw, cb, wa, ba, wx, bx, lam):
    def bd(w):
        return jax.scipy.linalg.block_diag(*[w[g] for g in range(NB_C)]).astype(BF16)

    def r(v):
        return v.reshape(1, W_C).astype(F32)

    return cw.astype(F32), r(cb), bd(wa), r(ba), bd(wx), r(bx), r(jax.nn.softplus(-lam.astype(F32)))


FOX_ONE_LANE = 3 * H_D


def _fox_selectors():
    selq = np.zeros((H_D // 2, LANES, LANES), np.float32)
    selk = np.zeros((H_D // 2, LANES, LANES), np.float32)
    for cc in range(H_D // 2):
        for h, base in ((2 * cc, HD_D), (2 * cc + 1, 0)):
            for piece in range(3):
                selq[cc, piece * H_D + h, base + piece] = 1.0
                selq[cc, FOX_ONE_LANE, base + 3 + piece] = 1.0
                selk[cc, FOX_ONE_LANE, base + piece] = 1.0
                selk[cc, piece * H_D + h, base + 3 + piece] = -1.0
    return jnp.asarray(selq, BF16), jnp.asarray(selk, BF16)


def _split_features(c, lane):
    hi, mid, lo = _split3(jnp.where(lane < H_D, c, 0.0))
    feat = (hi.astype(F32) + pltpu.roll(mid.astype(F32), H_D, 1) + pltpu.roll(lo.astype(F32), 2 * H_D, 1)
            + jnp.where(lane == FOX_ONE_LANE, 1.0, 0.0))
    return feat.astype(BF16)


def _fox_prep_kernel(q_ref, k_ref, v_ref, df_ref, bf_ref, selq_ref, selk_ref,
                     lf_ref, kt_ref, vl_ref, qa_ref, ka_ref, vt_ref, c_scr):
    t = pl.program_id(1)
    rows = q_ref.shape[0]

    @pl.when(t == 0)
    def _():
        c_scr[...] = jnp.zeros_like(c_scr)

    lane = lax.broadcasted_iota(jnp.int32, (rows, LANES), 1)
    low = lane < HD_D
    logf = jnp.where(lane < H_D, _log_sigmoid(df_ref[...] + bf_ref[...]), 0.0)
    lf_ref[...] = logf.T[:H_D]
    tri =(lax.broadcasted_iota(jnp.int32, (rows, rows), 0) >= lax.broadcasted_iota(jnp.int32, (rows, rows), 1)).astype(BF16)
    c = c_scr[0:1, :] + _dot3(tri, logf)
    c_scr[0:1, :] = c[rows - 1:rows, :]
    feat = _split_features(c * LOG2E, lane)
    scale = HD_D ** -0.5 * LOG2E
    for cc in range(W_D // LANES):
        sl = slice(cc * LANES, (cc + 1) * LANES)
        eq = _dot(feat, selq_ref[cc])
        ek = _dot(feat, selk_ref[cc])
        qs, kc = q_ref[:, sl] * scale, k_ref[:, sl]
        qa_ref[2 * cc] = jnp.where(low, qs, eq).astype(BF16)
        qa_ref[2 * cc + 1] = jnp.where(low, eq, qs).astype(BF16)
        ka_ref[2 * cc] = jnp.where(low, kc, ek).astype(BF16)
        ka_ref[2 * cc + 1] = jnp.where(low, ek, kc).astype(BF16)
        _store_heads_t([kt_ref], cc, kc)
        _store_heads_t([vl_ref, vt_ref], cc, v_ref[:, sl])


def _fox_prep(proj, batch, seq, bf_row, selq, selk):
    rows = ATT_TILE
    nt = seq // rows

    def tok(cb):
        return pl.BlockSpec((rows, W_D), lambda b, t: (b * nt + t, cb))

    specs, shapes = _pair_specs(batch, H_D, seq)
    sel = pl.BlockSpec((H_D // 2, LANES, LANES), lambda b, t: (0, 0, 0))
    cb0 = COLS_O_Q // W_D
    return pl.pallas_call(
        _fox_prep_kernel,
        grid=(batch, nt),
        in_specs=[tok(cb0), tok(cb0 + 1), tok(cb0 + 2),
                  pl.BlockSpec((rows, LANES), lambda b, t: (b * nt + t, COLS_O_F // LANES)),
                  pl.BlockSpec((1, LANES), lambda b, t: (0, 0)), sel, sel],
        out_specs=[pl.BlockSpec((None, H_D, rows), lambda b, t: (b, 0, t))] + specs,
        out_shape=[jax.ShapeDtypeStruct((batch, H_D, seq), F32)] + shapes,
        scratch_shapes=[pltpu.VMEM((SUBLANES, LANES), F32)],
        compiler_params=_cparams("parallel", "arbitrary"),
        name="fox_prep",
    )(proj, proj, proj, proj, bf_row, selq, selk)


def _router_kernel(x_ref, g_ref, r0_ref, r1_ref, r2_ref, o_ref, lg_ref):
    x = x_ref[...]
    y = x * lax.rsqrt(jnp.mean(x * x, axis=-1, keepdims=True) + EPS) * g_ref[...]
    o_ref[...] = y.astype(o_ref.dtype)
    y0, y1, y2 = _split3(y)
    r0, r1, r2 = r0_ref[...], r1_ref[...], r2_ref[...]
    lg_ref[...] = (_dot(y0, r0) + (_dot(y0, r1) + _dot(y1, r0)) + (_dot(y0, r2) + _dot(y1, r1) + _dot(y2, r0)))


def _rmsnorm_router(x, g, router, tm=TM_DENSE):
    m, d = x.shape
    rp = jnp.zeros((d, LANES), F32).at[:, :router.shape[1]].set(router.astype(F32))
    r0 = rp.astype(BF16)
    r1 = (rp - r0.astype(F32)).astype(BF16)
    r2 = (rp - r0.astype(F32) - r1.astype(F32)).astype(BF16)
    rspec = pl.BlockSpec((d, LANES), lambda i: (0, 0))
    return pl.pallas_call(
        _router_kernel,
        grid=(m // tm,),
        in_specs=[pl.BlockSpec((tm, d), lambda i: (i, 0)), pl.BlockSpec((1, d), lambda i: (0, 0)), rspec, rspec, rspec],
        out_specs=[pl.BlockSpec((tm, d), lambda i: (i, 0)), pl.BlockSpec((tm, LANES), lambda i: (i, 0))],
        out_shape=[jax.ShapeDtypeStruct((m, d), F32), jax.ShapeDtypeStruct((m, LANES), F32)],
        compiler_params=_cparams("parallel"),
        name="rmsnorm_router",
    )(x, g.reshape(1, d), r0, r1, r2)


def _combine_norm_kernel(y_ref, ya_ref, yb_ref, gt_ref, g_ref, op_ref, os_ref, *, n_prompt):
    i = pl.program_id(0)
    gt = gt_ref[...]
    out = _rms(y_ref[...] + (gt[:, 0:1] * ya_ref[...] + gt[:, 1:2] * yb_ref[...]), g_ref[...])

    @pl.when(i < n_prompt)
    def _():
        op_ref[...] = out

    @pl.when(i >= n_prompt)
    def _():
        os_ref[...] = out


def _combine_norm(y, ya, yb, gates, g, mp, tm=TM_DENSE):
    m, d = y.shape
    assert mp % tm == 0 and (m - mp) % tm == 0
    n_p = mp // tm
    row = pl.BlockSpec((tm, d), lambda i: (i, 0))
    return pl.pallas_call(
        functools.partial(_combine_norm_kernel, n_prompt=n_p),
        grid=(m // tm,),
        in_specs=[row, row, row, pl.BlockSpec((tm, TOP_K), lambda i: (i, 0)), pl.BlockSpec((1, d), lambda i: (0, 0))],
        out_specs=[pl.BlockSpec((tm, d), lambda i: (jnp.minimum(i, n_p - 1), 0)),
                   pl.BlockSpec((tm, d), lambda i: (jnp.maximum(i - n_p, 0), 0))],
        out_shape=[jax.ShapeDtypeStruct((mp, d), F32), jax.ShapeDtypeStruct((m - mp, d), F32)],
        compiler_params=_cparams("arbitrary"),
        name="combine_norm",
    )(y, ya, yb, gates, g.reshape(1, d))


def _scatter_rows_kernel(dest_ref, x_ref, init_ref, o_ref, sem):
    del init_ref
    tm = x_ref.shape[0]

    def row_copy(r, d):
        return pltpu.make_async_copy(x_ref.at[pl.ds(r, 1)], o_ref.at[pl.ds(d, 1)], sem)

    def body(r, carry):
        for k in range(TOP_K):
            row_copy(r, dest_ref[TOP_K * r + k]).start()
        return carry

    lax.fori_loop(0, tm, body, 0, unroll=8)
    for _ in range(TOP_K):
        pltpu.make_async_copy(x_ref, o_ref.at[pl.ds(0, tm)], sem).wait()


def _scatter_rows(x, dest, n_rows, tm=TM_DENSE):
    m, d = x.shape
    return pl.pallas_call(
        _scatter_rows_kernel,
        grid=(m // tm,),
        in_specs=[pl.BlockSpec((TOP_K * tm,), lambda i: (i,), memory_space=pltpu.SMEM),
                  pl.BlockSpec((tm, d), lambda i: (i, 0)),
                  pl.BlockSpec(memory_space=pl.ANY)],
        out_specs=pl.BlockSpec(memory_space=pl.ANY),
        out_shape=jax.ShapeDtypeStruct((n_rows, d), x.dtype),
        scratch_shapes=[pltpu.SemaphoreType.DMA(())],
        input_output_aliases={2: 0},
        compiler_params=_cparams("arbitrary"),
        name="scatter_rows",
    )(dest, x, jnp.zeros((n_rows, d), x.dtype))


def _moe_dispatch(logits, tm):
    m, e = logits.shape
    idx = lax.broadcasted_iota(jnp.int32, (m, e), 1)
    m1 = jnp.max(logits, axis=1, keepdims=True)
    i1 = jnp.min(jnp.where(logits == m1, idx, e), axis=1, keepdims=True)
    rest = jnp.where(idx == i1, -jnp.inf, logits)
    m2 = jnp.max(rest, axis=1, keepdims=True)
    i2 = jnp.min(jnp.where(rest == m2, idx, e), axis=1, keepdims=True)
    ex = jnp.exp(m2 - m1)
    gates = jnp.concatenate([1.0 / (1.0 + ex), ex / (1.0 + ex)], axis=1)
    flat_e = jnp.concatenate([i1, i2], axis=1).reshape(m * TOP_K)
    onehot = (flat_e[:, None] == jnp.arange(e, dtype=jnp.int32)[None, :]).astype(jnp.int32)
    csum = jnp.cumsum(onehot, axis=0)
    counts = csum[-1]
    padded = -(-counts // tm) * tm
    ends = jnp.cumsum(padded)
    dest = jnp.sum(onehot * ((ends - padded)[None, :] + csum - 1), axis=1)
    n_tiles = (m * TOP_K) // tm + e
    tile_start = jnp.arange(n_tiles, dtype=jnp.int32) * tm
    tile_expert = jnp.minimum(jnp.sum((ends[None, :] <= tile_start[:, None]).astype(jnp.int32), axis=1), e - 1)
    return gates, dest.astype(jnp.int32), tile_expert, (ends[-1:] // tm).astype(jnp.int32)


def kernel(x_prompt, x_sample, cache_k_e, cache_v_e, state_s_e, state_conv_o, state_h_o, cache_k_o, cache_v_o,
           cache_logf_o, page_table, w_in_e, lb_logits, gnorm_a, w_out_e, ffn_w1, ffn_w3, ffn_w2, w_in_o, conv_w,
           conv_b, rg_wa, rg_ba, rg_wx, rg_bx, rg_lambda, fox_bf, w_out_o, moe_router, moe_w1, moe_w3, moe_w2,
           norm_mix, norm_ffn, norm_final):
    batch, seq, d = x_prompt.shape
    n_seq, ts, _ = x_sample.shape
    mp, ms = batch * seq, n_seq * ts
    m = mp + ms
    n_pages = page_table.shape[1]
    past = n_pages * PAGE_SIZE
    chunk_a = CHUNK_A
    assert d == D_MODEL and ts <= SUBLANES and seq % chunk_a == 0 and m % chunk_a == 0

    x = (x_prompt.reshape(mp, d), x_sample.reshape(ms, d))
    pos_p = jnp.arange(seq, dtype=jnp.int32)
    pos_s = past + jnp.arange(ts, dtype=jnp.int32)

    def seq_rows(a):
        return a.reshape(n_seq, ts, a.shape[1])

    proj = _matmul([x], w_in_e[0].astype(BF16), norm_g=norm_mix[0])
    proj_s = proj[mp:]
    lb = jnp.cumsum(jax.nn.softmax(lb_logits.astype(F32), axis=0), axis=0)[0]
    oa_p, s_p = _hgrn2(proj.reshape(m // chunk_a, chunk_a, proj.shape[1]), batch, seq // chunk_a,
                       jnp.zeros((batch, H_A, DK_A, DV_A), F32), lb, gnorm_a[0])
    oa_s, s_s = _hgrn2(proj_s.reshape(n_seq, ts, proj.shape[1]), n_seq, 1, state_s_e[0], lb, gnorm_a[0],
                       group=8 if n_seq % 8 == 0 else 1)
    cos_p, sin_p = _rope_tables(pos_p)
    cos_s, sin_s = _rope_tables(pos_s)
    ke_p, ve_p, qa, ka, vt = _moba_prep(proj, batch, seq, cos_p, sin_p)
    ob_p = _flash(qa, ka, vt)
    ve_s = proj_s[:, COLS_E_Q + 2 * W_B:COLS_E_Q + 3 * W_B]
    ke_s, ob_s = _moba_sample(seq_rows(proj_s[:, COLS_E_Q:COLS_E_Q + W_B]),
                              seq_rows(proj_s[:, COLS_E_Q + W_B:COLS_E_Q + 2 * W_B]), seq_rows(ve_s),
                              cache_k_e[0], cache_v_e[0], page_table, cos_s, sin_s)
    y = _matmul([(oa_p, oa_s), (ob_p, ob_s.reshape(ms, W_B))], w_out_e[0].astype(BF16), res=x)
    y = _ffn(y, ffn_w1.astype(BF16), ffn_w3.astype(BF16), ffn_w2.astype(BF16), residual=True, norm_g=norm_ffn[0],
             tm=TM_EXPERT if m % TM_EXPERT == 0 else TM_DENSE, tf=ffn_w1.shape[2] // 2)

    cols_o = w_in_o.shape[2]
    cols_pad = -(-cols_o // LANES) * LANES
    w_in_o_p = jnp.zeros((d, cols_pad), BF16).at[:, :cols_o].set(w_in_o[0].astype(BF16))
    proj_o = _matmul([y], w_in_o_p, norm_g=norm_mix[1])
    proj_os = proj_o[mp:]
    ow = _odd_weights(conv_w[0], conv_b[0], rg_wa[0], rg_ba[0], rg_wx[0], rg_bx[0], rg_lambda[0])
    oc_p, buf_p, h_p = _rglru_prompt(proj_o, batch, seq, *ow)
    ps = proj_os.reshape(n_seq, ts, cols_pad)
    cx_s = ps[..., :W_C]
    oc_s, h_s = _rglru_sample(cx_s.swapaxes(0, 1), ps[..., W_C:2 * W_C].swapaxes(0, 1), state_conv_o[0].swapaxes(0, 1),
                              state_h_o[0], past == 0, *ow)
    oc_s = oc_s.swapaxes(0, 1).reshape(ms, W_C)
    buf_s = jnp.concatenate([state_conv_o[0].astype(F32), cx_s], axis=1)[:, ts:]
    bf_row = jnp.zeros((1, LANES), F32).at[0, :H_D].set(fox_bf[0].astype(F32))
    selq, selk = _fox_selectors()
    lf_p, ko_p, vo_p, qa, ka, vt = _fox_prep(proj_o, batch, seq, bf_row, selq, selk)
    od_p = _flash(qa, ka, vt)
    ko_s = proj_os[:, COLS_O_Q + W_D:COLS_O_Q + 2 * W_D]
    vo_s = proj_os[:, COLS_O_Q + 2 * W_D:COLS_O_Q + 3 * W_D]
    lf_s, od_s = _fox_sample(seq_rows(proj_os[:, COLS_O_Q:COLS_O_Q + W_D]), seq_rows(ko_s), seq_rows(vo_s),
                             proj_os[:, COLS_O_F:COLS_O_F + H_D].reshape(n_seq, ts * H_D), fox_bf[0],
                             cache_k_o[0], cache_v_o[0], cache_logf_o[0], page_table)
    y = _matmul([(oc_p, oc_s), (od_p, od_s.reshape(ms, W_D))], w_out_o[0].astype(BF16), res=y)
    tm_e = TM_EXPERT if (m * TOP_K) % TM_EXPERT == 0 else TM_DENSE
    hn, logits = _rmsnorm_router(y, norm_ffn[1], moe_router[0])
    gates, dest, tile_expert, n_used = _moe_dispatch(logits[:, :N_EXPERTS], tm_e)
    pair_pos = dest.reshape(m, TOP_K)
    y_e = _ffn(_scatter_rows(hn, dest, tile_expert.shape[0] * tm_e), moe_w1[0], moe_w3[0], moe_w2[0],
               tile_expert=tile_expert, n_used=n_used, tm=tm_e, tf=512)
    out_p, out_s = _combine_norm(y, jnp.take(y_e, pair_pos[:, 0], axis=0, mode="clip"),
                                 jnp.take(y_e, pair_pos[:, 1], axis=0, mode="clip"), gates, norm_final, mp)

    def heads(a, h):
        return a.reshape(1, n_seq, ts, h, a.shape[-1] // h)

    def heads_t(a):
        return jnp.transpose(a, (0, 3, 1, 2))[None]

    return (out_p.reshape(batch, seq, d), out_s.reshape(n_seq, ts, d),
            s_p[None], s_s[None],
            heads_t(ke_p), heads_t(ve_p), heads(ke_s, H_B), heads(ve_s, H_B),
            buf_p[None], buf_s[None], h_p[None], h_s[None],
            heads_t(ko_p), heads_t(vo_p), jnp.transpose(lf_p, (0, 2, 1))[None],
            heads(ko_s, H_D), heads(vo_s, H_D), lf_s.reshape(1, n_seq, ts, H_D))
```

```python
import functools
import math

import numpy as np
import jax
import jax.numpy as jnp
from jax import lax
from jax.experimental import pallas as pl
from jax.experimental.pallas import tpu as pltpu

F32 = jnp.float32
BF16 = jnp.bfloat16

D_MODEL = 1024
PAGE_SIZE = 128
H_A, DK_A, DV_A = 4, 128, 128
F_A, W_A = H_A * DK_A, H_A * DV_A
H_B, HD_B = 8, 64
W_B = H_B * HD_B
MOBA_BLOCK, MOBA_TOPK = 256, 3
W_C, NB_C, CONV_W, RG_C = 512, 8, 4, 8.0
BW_C = W_C // NB_C
H_D, HD_D = 8, 64
W_D = H_D * HD_D
N_EXPERTS, TOP_K = 8, 2
ROPE_THETA = 10000.0
EPS = 1e-6
NEG_INF = -1e30
LOG2E = math.log2(math.e)

COLS_E_Q = 2 * F_A + 2 * W_A
COLS_O_Q = 2 * W_C
COLS_O_F = 2 * W_C + 3 * W_D

LANES = 128
SUBLANES = 8
VMEM_LIMIT = 56 * 1024 * 1024
ATT_TILE = 256
TM_DENSE = 512
TM_EXPERT = 768
CHUNK_A = 128


def _cparams(*sem):
    return pltpu.CompilerParams(dimension_semantics=sem, vmem_limit_bytes=VMEM_LIMIT)


def _split3(x):
    hi = x.astype(BF16)
    r1 = x - hi.astype(F32)
    mid = r1.astype(BF16)
    lo = (r1 - mid.astype(F32)).astype(BF16)
    return hi, mid, lo


def _dot(a, b):
    return jnp.dot(a, b, preferred_element_type=F32)


def _dot_nt(a, b):
    return lax.dot_general(a, b, (((1,), (1,)), ((), ())), preferred_element_type=F32)


def _dot_tn(a, b):
    return lax.dot_general(a, b, (((0,), (0,)), ((), ())), preferred_element_type=F32)


def _dot3(a, b):
    b0, b1, b2 = _split3(b)
    return _dot(a, b0) + _dot(a, b1) + _dot(a, b2)


def _sigmoid(x):
    return 1.0 / (1.0 + jnp.exp(-x))


def _log_sigmoid(x):
    return jnp.minimum(x, 0.0) - jnp.log(1.0 + jnp.exp(-jnp.abs(x)))


def _gelu_tanh(x):
    return 0.5 * x * (1.0 + jnp.tanh(math.sqrt(2.0 / math.pi) * (x + 0.044715 * (x * x * x))))


def _rms(x, g):
    return x * lax.rsqrt(jnp.mean(x * x, axis=-1, keepdims=True) + EPS) * g


def _matmul_kernel(*refs, parts, n_prompt, has_norm, has_res):
    i = pl.program_id(0)
    refs = list(refs)
    g_ref = refs.pop(0) if has_norm else None
    xs = []
    for split in parts:
        if split:
            p_ref, s_ref = refs.pop(0), refs.pop(0)
            xs.append(jnp.where(i < n_prompt, p_ref[...], s_ref[...]))
        else:
            xs.append(refs.pop(0)[...])
    w_ref = refs.pop(0)
    res = None
    if has_res == "split":
        p_ref, s_ref = refs.pop(0), refs.pop(0)
        res = jnp.where(i < n_prompt, p_ref[...], s_ref[...])
    elif has_res:
        res = refs.pop(0)[...]
    o_ref = refs.pop(0)
    acc = None
    k0 = 0
    for x in xs:
        if has_norm:
            x = _rms(x, g_ref[...])
        kk = x.shape[1]
        part = _dot(x.astype(BF16), w_ref[k0:k0 + kk, :])
        acc = part if acc is None else acc + part
        k0 += kk
    if res is not None:
        acc = acc + res
    o_ref[...] = acc


def _matmul(xs, w, res=None, norm_g=None, tm=TM_DENSE):
    kt, n = w.shape
    parts = tuple(isinstance(x, tuple) for x in xs)
    m = sum(a.shape[0] for a in xs[0]) if parts[0] else xs[0].shape[0]
    in_specs, args = [], []
    n_prompt = [0]

    def add(x):
        if isinstance(x, tuple):
            xp, xsm = x
            assert xp.shape[0] % tm == 0 and xsm.shape[0] % tm == 0 and xp.shape[0] + xsm.shape[0] == m
            n_p = n_prompt[0] = xp.shape[0] // tm
            in_specs.append(pl.BlockSpec((tm, xp.shape[1]), lambda i: (jnp.minimum(i, n_p - 1), 0)))
            in_specs.append(pl.BlockSpec((tm, xp.shape[1]), lambda i: (jnp.maximum(i - n_p, 0), 0)))
            args.extend([xp, xsm])
        else:
            in_specs.append(pl.BlockSpec((tm, x.shape[1]), lambda i: (i, 0)))
            args.append(x)

    if norm_g is not None:
        assert len(xs) == 1
        in_specs.append(pl.BlockSpec((1, kt), lambda i: (0, 0)))
        args.append(norm_g.reshape(1, kt).astype(F32))
    for x in xs:
        add(x)
    in_specs.append(pl.BlockSpec((kt, n), lambda i: (0, 0)))
    args.append(w)
    if res is not None:
        add(res)
    has_res = "split" if isinstance(res, tuple) else res is not None
    return pl.pallas_call(
        functools.partial(_matmul_kernel, parts=parts, n_prompt=n_prompt[0], has_norm=norm_g is not None,
                          has_res=has_res),
        grid=(m // tm,),
        in_specs=in_specs,
        out_specs=pl.BlockSpec((tm, n), lambda i: (i, 0)),
        out_shape=jax.ShapeDtypeStruct((m, n), F32),
        compiler_params=_cparams("parallel"),
        name="matmul",
    )(*args)


def _ffn_kernel(te_ref, nu_ref, x_ref, w1_ref, w3_ref, w2_ref, *rest, has_norm, has_res):
    del te_ref
    rest = list(rest)
    g_ref = rest.pop(0) if has_norm else None
    res_ref = x_ref if has_res else None
    o_ref, acc_ref, xb_ref = rest
    i, f = pl.program_id(0), pl.program_id(1)

    @pl.when(i < nu_ref[0])
    def _():
        @pl.when(f == 0)
        def _():
            x = x_ref[...]
            if has_norm:
                x = _rms(x, g_ref[...])
            xb_ref[...] = x.astype(BF16)
            acc_ref[...] = jnp.zeros_like(acc_ref)

        x = xb_ref[...]
        a = _dot(x, w1_ref[...].astype(BF16))
        b = _dot(x, w3_ref[...].astype(BF16))
        g = (a * _sigmoid(a) * b).astype(BF16)
        acc_ref[...] += _dot(g, w2_ref[...].astype(BF16))

        @pl.when(f == pl.num_programs(1) - 1)
        def _():
            out = acc_ref[...]
            if has_res:
                out = out + res_ref[...]
            o_ref[...] = out

    @pl.when((i >= nu_ref[0]) & (f == pl.num_programs(1) - 1))
    def _():
        o_ref[...] = jnp.zeros_like(o_ref)


def _ffn(x, w1, w3, w2, residual=False, norm_g=None, tile_expert=None, n_used=None, tm=TM_DENSE, tf=256):
    m, d = x.shape
    f = w1.shape[-1]
    nf = f // tf
    if tile_expert is None:
        tile_expert = jnp.zeros((m // tm,), jnp.int32)
    if n_used is None:
        n_used = jnp.full((1,), m // tm, jnp.int32)

    def fblk(i, j, nu):
        return jnp.where(i < nu[0], j, nf - 1)

    row = pl.BlockSpec((tm, d), lambda i, j, te, nu: (i, 0))
    in_specs = [
        row,
        pl.BlockSpec((None, d, tf), lambda i, j, te, nu: (te[i], 0, fblk(i, j, nu))),
        pl.BlockSpec((None, d, tf), lambda i, j, te, nu: (te[i], 0, fblk(i, j, nu))),
        pl.BlockSpec((None, tf, d), lambda i, j, te, nu: (te[i], fblk(i, j, nu), 0)),
    ]
    args = [x, w1, w3, w2]
    if norm_g is not None:
        in_specs.append(pl.BlockSpec((1, d), lambda i, j, te, nu: (0, 0)))
        args.append(norm_g.reshape(1, d).astype(F32))
    return pl.pallas_call(
        functools.partial(_ffn_kernel, has_norm=norm_g is not None, has_res=residual),
        grid_spec=pltpu.PrefetchScalarGridSpec(
            num_scalar_prefetch=2,
            grid=(m // tm, nf),
            in_specs=in_specs,
            out_specs=row,
            scratch_shapes=[pltpu.VMEM((tm, d), F32), pltpu.VMEM((tm, d), BF16)],
        ),
        out_shape=jax.ShapeDtypeStruct((m, d), F32),
        compiler_params=_cparams("arbitrary", "arbitrary"),
        name="ffn",
    )(tile_expert, n_used, *args)


def _hgrn2_head(q, k, v, logf, st, c_sub):
    c = q.shape[0]
    ns = c // c_sub
    row = lax.broadcasted_iota(jnp.int32, (c, c), 0)
    col = lax.broadcasted_iota(jnp.int32, (c, c), 1)
    tri = (row >= col).astype(BF16)
    g = _dot3(tri, logf)
    vb = v.astype(BF16)
    o = _dot_nt((q * jnp.exp(g)).astype(BF16), st.astype(BF16))

    lane_c = lax.broadcasted_iota(jnp.int32, (c_sub, c), 1)
    slabs = []
    for i in range(ns):
        r0 = i * c_sub
        qi, ki, gi = q[r0:r0 + c_sub], k[r0:r0 + c_sub], g[r0:r0 + c_sub]
        slab = jnp.zeros((c_sub, c), F32)
        for s in range(c_sub):
            w = jnp.exp(jnp.minimum(gi - gi[s:s + 1, :], 0.0))
            colv = jnp.sum(qi * w * ki[s:s + 1, :], axis=-1, keepdims=True)
            slab = jnp.where(lane_c == r0 + s, colv, slab)
        slabs.append(slab)
    scores = slabs[0] if ns == 1 else jnp.concatenate(slabs, axis=0)
    scores = jnp.where((row >= col) & (row // c_sub == col // c_sub), scores, 0.0)

    w = c // 2
    while w >= c_sub:
        q_parts, k_parts = [], []
        for b in range(c // w):
            r0 = b * w
            if b % 2 == 1:
                q_parts.append(q[r0:r0 + w] * jnp.exp(g[r0:r0 + w] - g[r0 - 1:r0, :]))
                k_parts.append(jnp.zeros((w, q.shape[1]), F32))
            else:
                q_parts.append(jnp.zeros((w, q.shape[1]), F32))
                k_parts.append(k[r0:r0 + w] * jnp.exp(g[r0 + w - 1:r0 + w, :] - g[r0:r0 + w]))
        lvl = _dot_nt(jnp.concatenate(q_parts, axis=0).astype(BF16), jnp.concatenate(k_parts, axis=0).astype(BF16))
        scores = scores + jnp.where(((row // w) % 2 == 1) & (col // w == row // w - 1), lvl, 0.0)
        w //= 2
    o = o + _dot(scores.astype(BF16), vb)
    gl = g[c - 1:c, :]
    ke = (k * jnp.exp(gl - g)).astype(BF16)
    st_new = jnp.exp(gl) * st + _dot_tn(vb, ke)
    return o, st_new


def _hgrn2_kernel(aq_ref, af_ref, ai_ref, ag_ref, lb_ref, gn_ref, s0_ref, o_ref, s_ref, st_scr, *, c_sub, ct):
    t = pl.program_id(1)
    group = aq_ref.shape[0]

    @pl.when(t == 0)
    def _():
        for gi in range(group):
            for h in range(H_A):
                st_scr[gi, h] = s0_ref[gi, h].T

    def padded(x):
        if ct < SUBLANES:
            x = jnp.concatenate([x, jnp.zeros((SUBLANES - ct, x.shape[1]), F32)], axis=0)
        return x

    lb = lb_ref[...]
    for gi in range(group):
        aq, zf, vi = padded(aq_ref[gi]), padded(af_ref[gi]), padded(ai_ref[gi])
        c = aq.shape[0]
        real = lax.broadcasted_iota(jnp.int32, (c, F_A), 0) < ct
        sig = _sigmoid(zf)
        q = aq * _sigmoid(aq)
        logf = jnp.where(real, jnp.log(lb + (1.0 - lb) * sig), 0.0)
        k = jnp.where(real, (1.0 - lb) * _sigmoid(-zf), 0.0)
        outs = []
        for h in range(H_A):
            sl = slice(h * DK_A, (h + 1) * DK_A)
            o_h, st_new = _hgrn2_head(q[:, sl], k[:, sl], vi[:, sl], logf[:, sl], st_scr[gi, h], c_sub)
            st_scr[gi, h] = st_new
            outs.append(o_h)
        o = _rms(jnp.concatenate(outs, axis=1)[:ct], gn_ref[...])
        o_ref[gi] = o * _sigmoid(ag_ref[gi])

    @pl.when(t == pl.num_programs(1) - 1)
    def _():
        for gi in range(group):
            for h in range(H_A):
                s_ref[gi, h] = st_scr[gi, h].T


def _hgrn2(x3, n_seq, n_chunks, s0, lb, gnorm, group=1):
    ct = x3.shape[1]
    assert n_seq % group == 0 and (group == 1 or n_chunks == 1)

    def tok_spec(cb):
        return pl.BlockSpec((group, ct, F_A), lambda n, t: (n * n_chunks + t, 0, cb))

    state = pl.BlockSpec((group, H_A, DK_A, DV_A), lambda n, t: (n, 0, 0, 0))
    o, s = pl.pallas_call(
        functools.partial(_hgrn2_kernel, c_sub=SUBLANES, ct=ct),
        grid=(n_seq // group, n_chunks),
        in_specs=[tok_spec(0), tok_spec(1), tok_spec(2), tok_spec(3),
                  pl.BlockSpec((1, F_A), lambda n, t: (0, 0)),
                  pl.BlockSpec((1, W_A), lambda n, t: (0, 0)), state],
        out_specs=[pl.BlockSpec((group, ct, W_A), lambda n, t: (n * n_chunks + t, 0, 0)), state],
        out_shape=[jax.ShapeDtypeStruct((n_seq * n_chunks, ct, W_A), F32),
                   jax.ShapeDtypeStruct((n_seq, H_A, DK_A, DV_A), F32)],
        scratch_shapes=[pltpu.VMEM((group, H_A, DV_A, DK_A), F32)],
        compiler_params=_cparams("parallel", "arbitrary"),
        name="hgrn2",
    )(x3, x3, x3, x3, lb.reshape(1, F_A), gnorm.reshape(1, W_A), s0)
    return o.reshape(n_seq * n_chunks * ct, W_A), s


def _rope_tables(pos):
    half = HD_B // 2
    inv = ROPE_THETA ** (-jnp.arange(half, dtype=F32) / half)
    ang = pos.astype(F32)[:, None] * inv[None, :]
    cos, sin = jnp.cos(ang), jnp.sin(ang)
    return jnp.concatenate([cos, cos], axis=1), jnp.concatenate([-sin, sin], axis=1)


def _rope(x, cos2, sin2, lane):
    swapped = jnp.where((lane & (HD_B - 1)) >= HD_B // 2, pltpu.roll(x, HD_B // 2, 1), pltpu.roll(x, LANES - HD_B // 2, 1))
    return x * cos2 + swapped * sin2


def _dot_nt_f32(a, b):
    a0, a1, a2 = _split3(a)
    b0, b1, b2 = _split3(b)
    return (_dot_nt(a0, b0) + (_dot_nt(a0, b1) + _dot_nt(a1, b0))
            + (_dot_nt(a0, b2) + _dot_nt(a1, b1) + _dot_nt(a2, b0)))


def _top_select(g, index, axis, n_pick):
    sel = jnp.zeros(g.shape, F32)
    big = jnp.int32(1 << 20)
    for _ in range(n_pick):
        m = jnp.max(g, axis=axis, keepdims=True)
        first = jnp.min(jnp.where(g == m, index, big), axis=axis, keepdims=True)
        pick = jnp.where((index == first) & (m > 0.5 * NEG_INF), 1.0, 0.0)
        sel = sel + pick
        g = jnp.where(pick > 0.5, NEG_INF, g)
    return sel


VT_ROWS = HD_B + 16


def _store_heads_t(refs, c, x):
    xt = x.T
    for ref in refs:
        for hh in range(2):
            part = xt[hh * HD_B:(hh + 1) * HD_B]
            extra = ref.shape[1] - HD_B
            if extra:
                r = lax.broadcasted_iota(jnp.int32, (extra, part.shape[1]), 0)
                part = jnp.concatenate([part, jnp.where(r == 0, 1.0, 0.0)], axis=0)
            ref[2 * c + hh] = part.astype(ref.dtype)


def _moba_prep_kernel(q_ref, k_ref, v_ref, cos_ref, sin_ref, kt_ref, vl_ref, qa_ref, ka_ref, vt_ref, mrow_scr):
    t = pl.program_id(1)
    rows = q_ref.shape[0]

    @pl.when(t == 0)
    def _():
        mrow_scr[...] = jnp.zeros_like(mrow_scr)

    lane = lax.broadcasted_iota(jnp.int32, (rows, LANES), 1)
    lane1 = lax.broadcasted_iota(jnp.int32, (1, LANES), 1)
    low = lane < HD_B
    blk = lane & (HD_B - 1)
    nbm = HD_B // 2
    brow = lax.broadcasted_iota(jnp.int32, (nbm, rows), 0)
    cos2, sin2 = cos_ref[...], sin_ref[...]
    scale = HD_B ** -0.5 * LOG2E
    for c in range(W_B // LANES):
        sl = slice(c * LANES, (c + 1) * LANES)
        qr = _rope(q_ref[:, sl], cos2, sin2, lane)
        kr = _rope(k_ref[:, sl], cos2, sin2, lane)
        _store_heads_t([kt_ref], c, kr)
        gate_t = _dot_nt_f32(mrow_scr[c], qr)
        halves = []
        for r0 in (0, HD_B):
            g = jnp.where(brow < t, gate_t[r0:r0 + nbm], NEG_INF)
            keep = (_top_select(g, brow, 0, MOBA_TOPK) > 0.5) | (brow == t)
            halves += [jnp.where(keep, 0.0, NEG_INF), jnp.full((HD_B - nbm, rows), NEG_INF, F32)]
        msel = jnp.concatenate(halves, axis=0).T
        own = jnp.where(blk == t, 1.0, 0.0)
        qs = qr * scale
        qa_ref[2 * c] = jnp.where(low, qs, msel).astype(BF16)
        qa_ref[2 * c + 1] = jnp.where(low, msel, qs).astype(BF16)
        ka_ref[2 * c] = jnp.where(low, kr, own).astype(BF16)
        ka_ref[2 * c + 1] = jnp.where(low, own, kr).astype(BF16)
        _store_heads_t([vl_ref, vt_ref], c, v_ref[:, sl])
        mean = jnp.mean(kr, axis=0, keepdims=True)
        mrow_scr[c, pl.ds(HD_B + t, 1), :] = jnp.where(lane1 < HD_B, mean, 0.0)
        mrow_scr[c, pl.ds(t, 1), :] = jnp.where(lane1 < HD_B, 0.0, mean)


def _pair_specs(batch, heads, seq):
    nt = seq // ATT_TILE
    pair = pl.BlockSpec((None, heads, ATT_TILE, LANES), lambda b, t: (b, 0, t, 0))
    pair_shape = jax.ShapeDtypeStruct((batch, heads, seq, LANES), BF16)
    vt = pl.BlockSpec((None, heads, None, VT_ROWS, ATT_TILE), lambda b, t: (b, 0, t, 0, 0))
    vt_shape = jax.ShapeDtypeStruct((batch, heads, nt, VT_ROWS, ATT_TILE), BF16)
    leaf = pl.BlockSpec((None, heads, LANES // 2, ATT_TILE), lambda b, t: (b, 0, 0, t))
    leaf_shape = jax.ShapeDtypeStruct((batch, heads, LANES // 2, seq), F32)
    return [leaf, leaf, pair, pair, vt], [leaf_shape, leaf_shape, pair_shape, pair_shape, vt_shape]


def _moba_prep(proj, batch, seq, cos2, sin2):
    nb = seq // MOBA_BLOCK
    assert seq % MOBA_BLOCK == 0 and nb <= HD_B // 2 and MOBA_BLOCK == ATT_TILE
    rows = MOBA_BLOCK

    def tok(cb):
        return pl.BlockSpec((rows, W_B), lambda b, t: (b * nb + t, cb))

    specs, shapes = _pair_specs(batch, H_B, seq)
    cb0 = COLS_E_Q // W_B
    return pl.pallas_call(
        _moba_prep_kernel,
        grid=(batch, nb),
        in_specs=[tok(cb0), tok(cb0 + 1), tok(cb0 + 2),
                  pl.BlockSpec((rows, LANES), lambda b, t: (t, 0)),
                  pl.BlockSpec((rows, LANES), lambda b, t: (t, 0))],
        out_specs=specs,
        out_shape=shapes,
        scratch_shapes=[pltpu.VMEM((W_B // LANES, LANES, LANES), F32)],
        compiler_params=_cparams("parallel", "arbitrary"),
        name="moba_prep",
    )(proj, proj, proj, jnp.concatenate([cos2, cos2], axis=1), jnp.concatenate([sin2, sin2], axis=1))


def _flash_kernel(qa_ref, ka_ref, vt_ref, o_ref, s0_scr, s1_scr, acc_scr):
    i = pl.program_id(2)
    tq = qa_ref.shape[1]
    tv = vt_ref.shape[3]
    sub = tq // tv
    qs = [qa_ref[0], qa_ref[1]]

    def produce(j, s_ref):
        for hh in range(2):
            k = ka_ref[hh, pl.ds(pl.multiple_of(j * tq, tq), tq), :]
            s_ref[hh] = _dot_nt(k, qs[hh])

    def fold(x, op):
        return op(x.reshape(tq // SUBLANES, SUBLANES, tq), axis=0)

    def consume(j, s_ref, carry, diag=False):
        new = []
        for hh in range(2):
            m = carry[hh]
            if diag:
                krow = lax.broadcasted_iota(jnp.int32, (tq, tq), 0)
                qcol = lax.broadcasted_iota(jnp.int32, (tq, tq), 1)
                s_ref[hh] = jnp.where(krow <= qcol, s_ref[hh], NEG_INF)
            m_new = jnp.maximum(m, jnp.max(fold(s_ref[hh], jnp.max), axis=0, keepdims=True))
            pb = jnp.exp2(s_ref[hh] - m_new).astype(BF16)
            pv = _dot(vt_ref[hh, j * sub], pb[:tv])
            for u in range(1, sub):
                pv = pv + _dot(vt_ref[hh, j * sub + u], pb[u * tv:(u + 1) * tv])
            acc_scr[hh] = jnp.exp2(m - m_new) * acc_scr[hh] + pv
            new.append(m_new)
        return tuple(new)

    def pair(u, carry):
        j = 2 * u
        produce(j + 1, s1_scr)
        carry = consume(j, s0_scr, carry)
        produce(j + 2, s0_scr)
        return consume(j + 1, s1_scr, carry)

    def odd_tail(carry):
        produce(i, s1_scr)
        carry = consume(i - 1, s0_scr, carry)
        return consume(i, s1_scr, carry, diag=True)

    def even_tail(carry):
        return consume(i, s0_scr, carry, diag=True)

    acc_scr[...] = jnp.zeros_like(acc_scr)
    produce(0, s0_scr)
    init = tuple(jnp.full((1, tq), -jnp.inf, F32) for _ in range(2))
    carry = lax.fori_loop(0, i // 2, pair, init)
    lax.cond(i % 2 == 1, odd_tail, even_tail, carry)
    ot = jnp.concatenate([acc_scr[hh, :HD_B] / acc_scr[hh, HD_B:HD_B + 1] for hh in range(2)], axis=0)
    o_ref[...] = ot.T


def _flash(qa, ka, vt, tq=512):
    batch, heads, seq, _ = qa.shape
    tq = min(tq, seq)
    nq = seq // tq
    nv, hd, tv = vt.shape[2:]
    return pl.pallas_call(
        _flash_kernel,
        grid=(batch, heads // 2, nq),
        in_specs=[pl.BlockSpec((None, 2, tq, LANES), lambda b, c, i: (b, c, i, 0)),
                  pl.BlockSpec((None, 2, seq, LANES), lambda b, c, i: (b, c, 0, 0)),
                  pl.BlockSpec((None, 2, nv, hd, tv), lambda b, c, i: (b, c, 0, 0, 0))],
        out_specs=pl.BlockSpec((tq, LANES), lambda b, c, i: (b * nq + i, c)),
        out_shape=jax.ShapeDtypeStruct((batch * seq, heads * LANES // 2), F32),
        scratch_shapes=[pltpu.VMEM((2, tq, tq), F32), pltpu.VMEM((2, tq, tq), F32), pltpu.VMEM((2, hd, tq), F32)],
        compiler_params=_cparams("parallel", "parallel", "arbitrary"),
        name="flash",
    )(qa, ka, vt)


SAMPLE_GROUP = 2


def _paged_specs(n_pages, tail, group):
    zeros = (0,) * len(tail)
    return [pl.BlockSpec((None,) + tail, functools.partial(lambda n, pt, g, p: (pt[n * group + g, p],) + zeros, g=g, p=p))
            for g in range(group) for p in range(n_pages)]


def _group_size(n_seq):
    return SAMPLE_GROUP if n_seq % SAMPLE_GROUP == 0 else 1


def _pool_pages(pool):
    n_pool, page, h, dd = pool.shape
    return jnp.transpose(pool, (0, 2, 3, 1)).reshape(n_pool, h * dd, page)


def _pad_rows(x, rows):
    return jnp.concatenate([x, jnp.zeros((rows - x.shape[0], x.shape[1]), x.dtype)], axis=0)


def _query_rows(q, lane, rowh):
    t, w = q.shape
    rep = jnp.broadcast_to(q[:, None, :], (t, H_B, w)).reshape(t * H_B, w)
    return jnp.where((lane // HD_B) == rowh, rep, 0.0)


def _head_rows_to_tokens(o, lane, rowh, t):
    o = jnp.where((lane // HD_B) == rowh, o, 0.0)
    return jnp.sum(o.reshape(t, H_B, o.shape[1]), axis=1)


def _sample_softmax_pv(s_new, s_pages, v_new, vp, o_ref, lane, rowh, t):
    m = jnp.max(s_new, axis=1, keepdims=True)
    mm = s_pages[0]
    for s in s_pages[1:]:
        mm = jnp.maximum(mm, s)
    m = jnp.maximum(m, jnp.max(mm, axis=1, keepdims=True))
    p_new = jnp.exp(s_new - m)
    o = _dot(p_new.astype(BF16), v_new.astype(BF16))
    lsum = None
    for p, s in enumerate(s_pages):
        pp = jnp.exp(s - m)
        lsum = pp if lsum is None else lsum + pp
        o = o + _dot_nt(pp.astype(BF16), vp[p][...].astype(BF16))
    l = jnp.sum(p_new, axis=1, keepdims=True) + jnp.sum(lsum, axis=1, keepdims=True)
    o_ref[...] = _head_rows_to_tokens(o / l, lane, rowh, t)


def _moba_sample_kernel(pt_ref, q_ref, k_ref, v_ref, cos_ref, sin_ref, *rest, n_pages):
    del pt_ref
    group = q_ref.shape[0]
    ke_ref, o_ref = rest[2 * group * n_pages:]
    for g in range(group):
        kp = rest[g * n_pages:(g + 1) * n_pages]
        vp = rest[(group + g) * n_pages:(group + g + 1) * n_pages]
        _moba_sample_one(q_ref.at[g], k_ref.at[g], v_ref.at[g], cos_ref, sin_ref, kp, vp, ke_ref.at[g], o_ref.at[g], n_pages)


def _moba_sample_one(q_ref, k_ref, v_ref, cos_ref, sin_ref, kp, vp, ke_ref, o_ref, n_pages):
    t = q_ref.shape[0]
    nc = t * H_B
    nbp = n_pages * PAGE_SIZE // MOBA_BLOCK
    ppb = MOBA_BLOCK // PAGE_SIZE
    lane8 = lax.broadcasted_iota(jnp.int32, (SUBLANES, LANES), 1)
    cos2, sin2 = _pad_rows(cos_ref[...], SUBLANES), _pad_rows(sin_ref[...], SUBLANES)
    q8, k8, v8 = _pad_rows(q_ref[...], SUBLANES), _pad_rows(k_ref[...], SUBLANES), _pad_rows(v_ref[...], SUBLANES)
    qr = jnp.concatenate([_rope(q8[:, c * LANES:(c + 1) * LANES], cos2, sin2, lane8) for c in range(W_B // LANES)], axis=1)
    kr = jnp.concatenate([_rope(k8[:, c * LANES:(c + 1) * LANES], cos2, sin2, lane8) for c in range(W_B // LANES)], axis=1)
    ke_ref[...] = kr[:t]

    lane = lax.broadcasted_iota(jnp.int32, (nc, W_B), 1)
    rowh = lax.broadcasted_iota(jnp.int32, (nc, W_B), 0) % H_B
    qrows = _query_rows(qr[:t], lane, rowh)

    lanem = lax.broadcasted_iota(jnp.int32, (W_B, LANES), 1)
    means = jnp.zeros((W_B, LANES), F32)
    for j in range(nbp):
        blk_sum = kp[ppb * j][...]
        for u in range(1, ppb):
            blk_sum = blk_sum + kp[ppb * j + u][...]
        means = jnp.where(lanem == j, jnp.sum(blk_sum, axis=1, keepdims=True) * (1.0 / MOBA_BLOCK), means)
    q0, q1, q2 = _split3(qrows)
    m0, m1, m2 = _split3(means)
    gate = _dot(q0, m0) + (_dot(q0, m1) + _dot(q1, m0)) + (_dot(q0, m2) + _dot(q1, m1) + _dot(q2, m0))
    lg = lax.broadcasted_iota(jnp.int32, (nc, LANES), 1)
    gm = jnp.where(lg < nbp, gate, NEG_INF)
    rank = jnp.zeros((nc, LANES), F32)
    for sft in range(1, nbp):
        before = pltpu.roll(gm, sft, 1)
        after = pltpu.roll(gm, LANES - sft, 1)
        rank = rank + jnp.where((before >= gm) & (lg >= sft), 1.0, 0.0) + jnp.where(after > gm, 1.0, 0.0)
    sel = jnp.where((rank < MOBA_TOPK - 0.5) & (lg < nbp), 1.0, 0.0)

    qb = (qrows * (HD_B ** -0.5)).astype(BF16)
    s_pages = []
    for p in range(n_pages):
        s = _dot(qb, kp[p][...].astype(BF16))
        j = p // ppb
        s_pages.append(jnp.where(sel[:, j:j + 1] > 0.5, s, NEG_INF))
    s_new = _dot_nt(qb, kr.astype(BF16))
    ktok = lax.broadcasted_iota(jnp.int32, (nc, SUBLANES), 1)
    qtok = lax.broadcasted_iota(jnp.int32, (nc, SUBLANES), 0) // H_B
    s_new = jnp.where((ktok <= qtok) & (ktok < t), s_new, NEG_INF)
    _sample_softmax_pv(s_new, s_pages, v8, vp, o_ref, lane, rowh, t)


def _moba_sample(q, k, v, pool_k, pool_v, page_table, cos2, sin2):
    n_seq, t, w = q.shape
    n_pages = page_table.shape[1]
    assert (n_pages * PAGE_SIZE) % MOBA_BLOCK == 0
    group = _group_size(n_seq)
    row = pl.BlockSpec((group, t, w), lambda n, pt: (n, 0, 0))
    tab = pl.BlockSpec((t, LANES), lambda n, pt: (0, 0))
    page = (w, PAGE_SIZE)
    return pl.pallas_call(
        functools.partial(_moba_sample_kernel, n_pages=n_pages),
        grid_spec=pltpu.PrefetchScalarGridSpec(
            num_scalar_prefetch=1,
            grid=(n_seq // group,),
            in_specs=[row, row, row, tab, tab] + _paged_specs(n_pages, page, group) + _paged_specs(n_pages, page, group),
            out_specs=[row, row],
        ),
        out_shape=[jax.ShapeDtypeStruct((n_seq, t, w), F32)] * 2,
        compiler_params=_cparams("arbitrary"),
        name="moba_sample",
    )(page_table, q, k, v, jnp.concatenate([cos2, cos2], axis=1), jnp.concatenate([sin2, sin2], axis=1),
      *([_pool_pages(pool_k)] * (group * n_pages)), *([_pool_pages(pool_v)] * (group * n_pages)))


def _fox_sample_kernel(pt_ref, q_ref, k_ref, v_ref, df_ref, bf_ref, *rest, n_pages):
    del pt_ref
    group = q_ref.shape[0]
    lf_ref, o_ref = rest[3 * group * n_pages:]
    for g in range(group):
        kp, vp, lp = (rest[(kind * group + g) * n_pages:(kind * group + g + 1) * n_pages] for kind in range(3))
        _fox_sample_one(q_ref.at[g], k_ref.at[g], v_ref.at[g], df_ref.at[g], bf_ref, kp, vp, lp, lf_ref.at[g], o_ref.at[g],
                        n_pages)


def _fox_sample_one(q_ref, k_ref, v_ref, df_ref, bf_ref, kp, vp, lp, lf_ref, o_ref, n_pages):
    t = q_ref.shape[0]
    nc = t * H_D
    k8, v8 = _pad_rows(k_ref[...], SUBLANES), _pad_rows(v_ref[...], SUBLANES)
    lane = lax.broadcasted_iota(jnp.int32, (nc, W_D), 1)
    rowh = lax.broadcasted_iota(jnp.int32, (nc, W_D), 0) % H_D
    qb = (_query_rows(q_ref[...], lane, rowh) * (HD_D ** -0.5)).astype(BF16)

    lf_row = _log_sigmoid(df_ref[...] + bf_ref[...])
    lf_ref[...] = lf_row
    rr = lax.broadcasted_iota(jnp.int32, (nc, nc), 0)
    cc = lax.broadcasted_iota(jnp.int32, (nc, nc), 1)
    same = (rr % H_D) == (cc % H_D)
    cn_col = jnp.sum(jnp.where(same & (cc // H_D <= rr // H_D), lf_row, 0.0), axis=1, keepdims=True)
    ktok = lax.broadcasted_iota(jnp.int32, (nc, SUBLANES), 1)
    qtok = lax.broadcasted_iota(jnp.int32, (nc, SUBLANES), 0) // H_D
    cn_keys = jnp.zeros((nc, SUBLANES), F32)
    for tp in range(t):
        col = jnp.sum(jnp.where(same & (cc // H_D <= tp), lf_row, 0.0), axis=1, keepdims=True)
        cn_keys = jnp.where(ktok == tp, col, cn_keys)

    lfa = jnp.concatenate([lp[p][...] for p in range(n_pages)], axis=0)
    nr = lfa.shape[0]
    lane_r = lax.broadcasted_iota(jnp.int32, (nr, LANES), 1)
    incl = lfa
    sh = 1
    while sh < LANES:
        incl = incl + jnp.where(lane_r < LANES - sh, pltpu.roll(incl, LANES - sh, 1), 0.0)
        sh *= 2
    r2 = lax.broadcasted_iota(jnp.int32, (nr, nr), 0)
    c2 = lax.broadcasted_iota(jnp.int32, (nr, nr), 1)
    later = ((c2 > r2) & ((c2 % H_D) == (r2 % H_D))).astype(BF16)
    suf = (incl - lfa) + _dot3(later, jnp.broadcast_to(incl[:, 0:1], (nr, LANES)))

    s_pages = []
    for p in range(n_pages):
        bias = jnp.concatenate([suf[H_D * p:H_D * (p + 1)]] * t, axis=0)
        s_pages.append(_dot(qb, kp[p][...].astype(BF16)) + bias + cn_col)
    s_new = _dot_nt(qb, k8.astype(BF16)) + (cn_col - cn_keys)
    s_new = jnp.where((ktok <= qtok) & (ktok < t), s_new, NEG_INF)
    _sample_softmax_pv(s_new, s_pages, v8, vp, o_ref, lane, rowh, t)


def _fox_sample(q, k, v, df, bf, pool_k, pool_v, pool_lf, page_table):
    n_seq, t, w = q.shape
    nc = t * H_D
    n_pages = page_table.shape[1]
    bft = jnp.tile(bf.astype(F32), t)
    group = _group_size(n_seq)
    row = pl.BlockSpec((group, t, w), lambda n, pt: (n, 0, 0))
    frow = pl.BlockSpec((group, 1, nc), lambda n, pt: (n, 0, 0))
    page = (w, PAGE_SIZE)
    lf, o = pl.pallas_call(
        functools.partial(_fox_sample_kernel, n_pages=n_pages),
        grid_spec=pltpu.PrefetchScalarGridSpec(
            num_scalar_prefetch=1,
            grid=(n_seq // group,),
            in_specs=[row, row, row, frow, pl.BlockSpec((1, nc), lambda n, pt: (0, 0))]
            + _paged_specs(n_pages, page, group) + _paged_specs(n_pages, page, group)
            + _paged_specs(n_pages, (H_D, PAGE_SIZE), group),
            out_specs=[frow, row],
        ),
        out_shape=[jax.ShapeDtypeStruct((n_seq, 1, nc), F32), jax.ShapeDtypeStruct((n_seq, t, w), F32)],
        compiler_params=_cparams("arbitrary"),
        name="fox_sample",
    )(page_table, q, k, v, df.reshape(n_seq, 1, nc), bft.reshape(1, nc),
      *([_pool_pages(pool_k)] * (group * n_pages)), *([_pool_pages(pool_v)] * (group * n_pages)),
      *([jnp.transpose(pool_lf, (0, 2, 1))] * (group * n_pages)))
    return lf, o


def _shift_rows(x, d, fill):
    return jnp.concatenate([jnp.full((d, x.shape[1]), fill, x.dtype), x[:-d]], axis=0)


def _rglru_gates(xc, wa_ref, ba_ref, wx_ref, bx_ref, sp_ref, first_row_pos0):
    xb = xc.astype(BF16)
    r = _sigmoid(_dot(xb, wa_ref[...]) + ba_ref[...])
    i = _sigmoid(_dot(xb, wx_ref[...]) + bx_ref[...])
    log_a = -RG_C * r * sp_ref[...]
    a = jnp.exp(log_a)
    mult = jnp.sqrt(1.0 - jnp.exp(2.0 * log_a))
    if first_row_pos0 is not None:
        mult = jnp.where(first_row_pos0, 1.0, mult)
    return a, xc * i * mult


def _rglru_kernel(cx_ref, cg_ref, cw_ref, cb_ref, wa_ref, ba_ref, wx_ref, bx_ref, sp_ref,
                  o_ref, buf_ref, h_ref, tail_scr, h_scr):
    t = pl.program_id(1)
    tt = cx_ref.shape[0]

    @pl.when(t == 0)
    def _():
        tail_scr[...] = jnp.zeros_like(tail_scr)
        h_scr[...] = jnp.zeros_like(h_scr)

    cx = cx_ref[...]
    ext = jnp.concatenate([tail_scr[...], cx], axis=0)
    xc = cb_ref[...] + cx * cw_ref[CONV_W - 1:CONV_W, :]
    for d in range(1, CONV_W):
        xc = xc + ext[SUBLANES - d:SUBLANES - d + tt] * cw_ref[CONV_W - 1 - d:CONV_W - d, :]
    row = lax.broadcasted_iota(jnp.int32, (tt, W_C), 0)
    a, b = _rglru_gates(xc, wa_ref, ba_ref, wx_ref, bx_ref, sp_ref, (row == 0) & (t == 0))
    d = 1
    while d < tt:
        b = b + a * _shift_rows(b, d, 0.0)
        a = a * _shift_rows(a, d, 1.0)
        d *= 2
    h = a * h_scr[0:1, :] + b
    o_ref[...] = h * _gelu_tanh(cg_ref[...])
    h_scr[0:1, :] = h[tt - 1:tt, :]
    tail_scr[...] = cx[tt - SUBLANES:tt, :]

    @pl.when(t == pl.num_programs(1) - 1)
    def _():
        buf_ref[...] = cx[tt - (CONV_W - 1):tt, :]
        h_ref[...] = h[tt - 1:tt, :]


def _rglru_prompt(proj, batch, seq, cw, cb, wa, ba, wx, bx, sp, tt=256):
    nt = seq // tt
    vec = pl.BlockSpec((1, W_C), lambda b, t: (0, 0))
    mat = pl.BlockSpec((W_C, W_C), lambda b, t: (0, 0))
    o, buf, h = pl.pallas_call(
        _rglru_kernel,
        grid=(batch, nt),
        in_specs=[pl.BlockSpec((tt, W_C), lambda b, t: (b * nt + t, 0)),
                  pl.BlockSpec((tt, W_C), lambda b, t: (b * nt + t, 1)),
                  pl.BlockSpec((CONV_W, W_C), lambda b, t: (0, 0)), vec, mat, vec, mat, vec, vec],
        out_specs=[pl.BlockSpec((tt, W_C), lambda b, t: (b * nt + t, 0)),
                   pl.BlockSpec((None, CONV_W - 1, W_C), lambda b, t: (b, 0, 0)),
                   pl.BlockSpec((None, 1, W_C), lambda b, t: (b, 0, 0))],
        out_shape=[jax.ShapeDtypeStruct((batch * seq, W_C), F32),
                   jax.ShapeDtypeStruct((batch, CONV_W - 1, W_C), F32),
                   jax.ShapeDtypeStruct((batch, 1, W_C), F32)],
        scratch_shapes=[pltpu.VMEM((SUBLANES, W_C), F32), pltpu.VMEM((SUBLANES, W_C), F32)],
        compiler_params=_cparams("parallel", "arbitrary"),
        name="rglru",
    )(proj, proj, cw, cb, wa, ba, wx, bx, sp)
    return o, buf, h.reshape(batch, W_C)


def _rglru_sample_kernel(cx_ref, cg_ref, buf_ref, h0_ref, cw_ref, cb_ref, wa_ref, ba_ref, wx_ref, bx_ref, sp_ref,
                         o_ref, h_ref, *, pos0_is_zero):
    t, n, _ = cx_ref.shape
    xp = [buf_ref[j] for j in range(CONV_W - 1)] + [cx_ref[j] for j in range(t)]
    xcs = []
    for s in range(t):
        xc = cb_ref[...] + xp[s] * cw_ref[0:1, :]
        for j in range(1, CONV_W):
            xc = xc + xp[s + j] * cw_ref[j:j + 1, :]
        xcs.append(xc)
    xc = jnp.concatenate(xcs, axis=0)
    first = (lax.broadcasted_iota(jnp.int32, xc.shape, 0) < n) if pos0_is_zero else None
    a, b = _rglru_gates(xc, wa_ref, ba_ref, wx_ref, bx_ref, sp_ref, first)
    h = h0_ref[...]
    for s in range(t):
        h = a[s * n:(s + 1) * n] * h + b[s * n:(s + 1) * n]
        o_ref[s] = h * _gelu_tanh(cg_ref[s])
    h_ref[...] = h


def _rglru_sample(cx, cg, buf, h0, pos0_is_zero, cw, cb, wa, ba, wx, bx, sp):
    t, n, _ = cx.shape
    return pl.pallas_call(
        functools.partial(_rglru_sample_kernel, pos0_is_zero=pos0_is_zero),
        out_shape=[jax.ShapeDtypeStruct((t, n, W_C), F32), jax.ShapeDtypeStruct((n, W_C), F32)],
        compiler_params=pltpu.CompilerParams(vmem_limit_bytes=VMEM_LIMIT),
        name="rglru_sample",
    )(cx, cg, buf, h0, cw, cb, wa, ba, wx, bx, sp)


def _odd_weights(cw, cb, wa, ba, wx, bx, lam):
    def bd(w):
        return jax.scipy.linalg.block_diag(*[w[g] for g in range(NB_C)]).astype(BF16)

    def r(v):
        return v.reshape(1, W_C).astype(F32)

    return cw.astype(F32), r(cb), bd(wa), r(ba), bd(wx), r(bx), r(jax.nn.softplus(-lam.astype(F32)))


FOX_ONE_LANE = 3 * H_D


def _fox_selectors():
    selq = np.zeros((H_D // 2, LANES, LANES), np.float32)
    selk = np.zeros((H_D // 2, LANES, LANES), np.float32)
    for cc in range(H_D // 2):
        for h, base in ((2 * cc, HD_D), (2 * cc + 1, 0)):
            for piece in range(3):
                selq[cc, piece * H_D + h, base + piece] = 1.0
                selq[cc, FOX_ONE_LANE, base + 3 + piece] = 1.0
                selk[cc, FOX_ONE_LANE, base + piece] = 1.0
                selk[cc, piece * H_D + h, base + 3 + piece] = -1.0
    return jnp.asarray(selq, BF16), jnp.asarray(selk, BF16)


def _split_features(c, lane):
    hi, mid, lo = _split3(jnp.where(lane < H_D, c, 0.0))
    feat = (hi.astype(F32) + pltpu.roll(mid.astype(F32), H_D, 1) + pltpu.roll(lo.astype(F32), 2 * H_D, 1)
            + jnp.where(lane == FOX_ONE_LANE, 1.0, 0.0))
    return feat.astype(BF16)


def _fox_prep_kernel(q_ref, k_ref, v_ref, df_ref, bf_ref, selq_ref, selk_ref,
                     lf_ref, kt_ref, vl_ref, qa_ref, ka_ref, vt_ref, c_scr):
    t = pl.program_id(1)
    rows = q_ref.shape[0]

    @pl.when(t == 0)
    def _():
        c_scr[...] = jnp.zeros_like(c_scr)

    lane = lax.broadcasted_iota(jnp.int32, (rows, LANES), 1)
    low = lane < HD_D
    logf = jnp.where(lane < H_D, _log_sigmoid(df_ref[...] + bf_ref[...]), 0.0)
    lf_ref[...] = logf.T[:H_D]
    tri =(lax.broadcasted_iota(jnp.int32, (rows, rows), 0) >= lax.broadcasted_iota(jnp.int32, (rows, rows), 1)).astype(BF16)
    c = c_scr[0:1, :] + _dot3(tri, logf)
    c_scr[0:1, :] = c[rows - 1:rows, :]
    feat = _split_features(c * LOG2E, lane)
    scale = HD_D ** -0.5 * LOG2E
    for cc in range(W_D // LANES):
        sl = slice(cc * LANES, (cc + 1) * LANES)
        eq = _dot(feat, selq_ref[cc])
        ek = _dot(feat, selk_ref[cc])
        qs, kc = q_ref[:, sl] * scale, k_ref[:, sl]
        qa_ref[2 * cc] = jnp.where(low, qs, eq).astype(BF16)
        qa_ref[2 * cc + 1] = jnp.where(low, eq, qs).astype(BF16)
        ka_ref[2 * cc] = jnp.where(low, kc, ek).astype(BF16)
        ka_ref[2 * cc + 1] = jnp.where(low, ek, kc).astype(BF16)
        _store_heads_t([kt_ref], cc, kc)
        _store_heads_t([vl_ref, vt_ref], cc, v_ref[:, sl])


def _fox_prep(proj, batch, seq, bf_row, selq, selk):
    rows = ATT_TILE
    nt = seq // rows

    def tok(cb):
        return pl.BlockSpec((rows, W_D), lambda b, t: (b * nt + t, cb))

    specs, shapes = _pair_specs(batch, H_D, seq)
    sel = pl.BlockSpec((H_D // 2, LANES, LANES), lambda b, t: (0, 0, 0))
    cb0 = COLS_O_Q // W_D
    return pl.pallas_call(
        _fox_prep_kernel,
        grid=(batch, nt),
        in_specs=[tok(cb0), tok(cb0 + 1), tok(cb0 + 2),
                  pl.BlockSpec((rows, LANES), lambda b, t: (b * nt + t, COLS_O_F // LANES)),
                  pl.BlockSpec((1, LANES), lambda b, t: (0, 0)), sel, sel],
        out_specs=[pl.BlockSpec((None, H_D, rows), lambda b, t: (b, 0, t))] + specs,
        out_shape=[jax.ShapeDtypeStruct((batch, H_D, seq), F32)] + shapes,
        scratch_shapes=[pltpu.VMEM((SUBLANES, LANES), F32)],
        compiler_params=_cparams("parallel", "arbitrary"),
        name="fox_prep",
    )(proj, proj, proj, proj, bf_row, selq, selk)


def _router_kernel(x_ref, g_ref, r0_ref, r1_ref, r2_ref, o_ref, lg_ref):
    x = x_ref[...]
    y = x * lax.rsqrt(jnp.mean(x * x, axis=-1, keepdims=True) + EPS) * g_ref[...]
    o_ref[...] = y.astype(o_ref.dtype)
    y0, y1, y2 = _split3(y)
    r0, r1, r2 = r0_ref[...], r1_ref[...], r2_ref[...]
    lg_ref[...] = (_dot(y0, r0) + (_dot(y0, r1) + _dot(y1, r0)) + (_dot(y0, r2) + _dot(y1, r1) + _dot(y2, r0)))


def _rmsnorm_router(x, g, router, tm=TM_DENSE):
    m, d = x.shape
    rp = jnp.zeros((d, LANES), F32).at[:, :router.shape[1]].set(router.astype(F32))
    r0 = rp.astype(BF16)
    r1 = (rp - r0.astype(F32)).astype(BF16)
    r2 = (rp - r0.astype(F32) - r1.astype(F32)).astype(BF16)
    rspec = pl.BlockSpec((d, LANES), lambda i: (0, 0))
    return pl.pallas_call(
        _router_kernel,
        grid=(m // tm,),
        in_specs=[pl.BlockSpec((tm, d), lambda i: (i, 0)), pl.BlockSpec((1, d), lambda i: (0, 0)), rspec, rspec, rspec],
        out_specs=[pl.BlockSpec((tm, d), lambda i: (i, 0)), pl.BlockSpec((tm, LANES), lambda i: (i, 0))],
        out_shape=[jax.ShapeDtypeStruct((m, d), F32), jax.ShapeDtypeStruct((m, LANES), F32)],
        compiler_params=_cparams("parallel"),
        name="rmsnorm_router",
    )(x, g.reshape(1, d), r0, r1, r2)


def _combine_norm_kernel(y_ref, ya_ref, yb_ref, gt_ref, g_ref, op_ref, os_ref, *, n_prompt):
    i = pl.program_id(0)
    gt = gt_ref[...]
    out = _rms(y_ref[...] + (gt[:, 0:1] * ya_ref[...] + gt[:, 1:2] * yb_ref[...]), g_ref[...])

    @pl.when(i < n_prompt)
    def _():
        op_ref[...] = out

    @pl.when(i >= n_prompt)
    def _():
        os_ref[...] = out


def _combine_norm(y, ya, yb, gates, g, mp, tm=TM_DENSE):
    m, d = y.shape
    assert mp % tm == 0 and (m - mp) % tm == 0
    n_p = mp // tm
    row = pl.BlockSpec((tm, d), lambda i: (i, 0))
    return pl.pallas_call(
        functools.partial(_combine_norm_kernel, n_prompt=n_p),
        grid=(m // tm,),
        in_specs=[row, row, row, pl.BlockSpec((tm, TOP_K), lambda i: (i, 0)), pl.BlockSpec((1, d), lambda i: (0, 0))],
        out_specs=[pl.BlockSpec((tm, d), lambda i: (jnp.minimum(i, n_p - 1), 0)),
                   pl.BlockSpec((tm, d), lambda i: (jnp.maximum(i - n_p, 0), 0))],
        out_shape=[jax.ShapeDtypeStruct((mp, d), F32), jax.ShapeDtypeStruct((m - mp, d), F32)],
        compiler_params=_cparams("arbitrary"),
        name="combine_norm",
    )(y, ya, yb, gates, g.reshape(1, d))


def _scatter_rows_kernel(dest_ref, x_ref, init_ref, o_ref, sem):
    del init_ref
    tm = x_ref.shape[0]

    def row_copy(r, d):
        return pltpu.make_async_copy(x_ref.at[pl.ds(r, 1)], o_ref.at[pl.ds(d, 1)], sem)

    def body(r, carry):
        for k in range(TOP_K):
            row_copy(r, dest_ref[TOP_K * r + k]).start(priority=k % 2)
        return carry

    lax.fori_loop(0, tm, body, 0, unroll=8)
    for _ in range(TOP_K):
        pltpu.make_async_copy(x_ref, o_ref.at[pl.ds(0, tm)], sem).wait()


def _scatter_rows(x, dest, n_rows, tm=TM_DENSE):
    m, d = x.shape
    return pl.pallas_call(
        _scatter_rows_kernel,
        grid=(m // tm,),
        in_specs=[pl.BlockSpec((TOP_K * tm,), lambda i: (i,), memory_space=pltpu.SMEM),
                  pl.BlockSpec((tm, d), lambda i: (i, 0)),
                  pl.BlockSpec(memory_space=pl.ANY)],
        out_specs=pl.BlockSpec(memory_space=pl.ANY),
        out_shape=jax.ShapeDtypeStruct((n_rows, d), x.dtype),
        scratch_shapes=[pltpu.SemaphoreType.DMA(())],
        input_output_aliases={2: 0},
        compiler_params=_cparams("arbitrary"),
        name="scatter_rows",
    )(dest, x, jnp.zeros((n_rows, d), x.dtype))


def _moe_dispatch(logits, tm):
    m, e = logits.shape
    idx = lax.broadcasted_iota(jnp.int32, (m, e), 1)
    m1 = jnp.max(logits, axis=1, keepdims=True)
    i1 = jnp.min(jnp.where(logits == m1, idx, e), axis=1, keepdims=True)
    rest = jnp.where(idx == i1, -jnp.inf, logits)
    m2 = jnp.max(rest, axis=1, keepdims=True)
    i2 = jnp.min(jnp.where(rest == m2, idx, e), axis=1, keepdims=True)
    ex = jnp.exp(m2 - m1)
    gates = jnp.concatenate([1.0 / (1.0 + ex), ex / (1.0 + ex)], axis=1)
    flat_e = jnp.concatenate([i1, i2], axis=1).reshape(m * TOP_K)
    onehot = (flat_e[:, None] == jnp.arange(e, dtype=jnp.int32)[None, :]).astype(jnp.int32)
    csum = jnp.cumsum(onehot, axis=0)
    counts = csum[-1]
    padded = -(-counts // tm) * tm
    ends = jnp.cumsum(padded)
    dest = jnp.sum(onehot * ((ends - padded)[None, :] + csum - 1), axis=1)
    n_tiles = (m * TOP_K) // tm + e
    tile_start = jnp.arange(n_tiles, dtype=jnp.int32) * tm
    tile_expert = jnp.minimum(jnp.sum((ends[None, :] <= tile_start[:, None]).astype(jnp.int32), axis=1), e - 1)
    return gates, dest.astype(jnp.int32), tile_expert, (ends[-1:] // tm).astype(jnp.int32)


def kernel(x_prompt, x_sample, cache_k_e, cache_v_e, state_s_e, state_conv_o, state_h_o, cache_k_o, cache_v_o,
           cache_logf_o, page_table, w_in_e, lb_logits, gnorm_a, w_out_e, ffn_w1, ffn_w3, ffn_w2, w_in_o, conv_w,
           conv_b, rg_wa, rg_ba, rg_wx, rg_bx, rg_lambda, fox_bf, w_out_o, moe_router, moe_w1, moe_w3, moe_w2,
           norm_mix, norm_ffn, norm_final):
    batch, seq, d = x_prompt.shape
    n_seq, ts, _ = x_sample.shape
    mp, ms = batch * seq, n_seq * ts
    m = mp + ms
    n_pages = page_table.shape[1]
    past = n_pages * PAGE_SIZE
    chunk_a = CHUNK_A
    assert d == D_MODEL and ts <= SUBLANES and seq % chunk_a == 0 and m % chunk_a == 0

    x = (x_prompt.reshape(mp, d), x_sample.reshape(ms, d))
    pos_p = jnp.arange(seq, dtype=jnp.int32)
    pos_s = past + jnp.arange(ts, dtype=jnp.int32)

    def seq_rows(a):
        return a.reshape(n_seq, ts, a.shape[1])

    proj = _matmul([x], w_in_e[0].astype(BF16), norm_g=norm_mix[0])
    proj_s = proj[mp:]
    lb = jnp.cumsum(jax.nn.softmax(lb_logits.astype(F32), axis=0), axis=0)[0]
    oa_p, s_p = _hgrn2(proj.reshape(m // chunk_a, chunk_a, proj.shape[1]), batch, seq // chunk_a,
                       jnp.zeros((batch, H_A, DK_A, DV_A), F32), lb, gnorm_a[0])
    oa_s, s_s = _hgrn2(proj_s.reshape(n_seq, ts, proj.shape[1]), n_seq, 1, state_s_e[0], lb, gnorm_a[0],
                       group=8 if n_seq % 8 == 0 else 1)
    cos_p, sin_p = _rope_tables(pos_p)
    cos_s, sin_s = _rope_tables(pos_s)
    ke_p, ve_p, qa, ka, vt = _moba_prep(proj, batch, seq, cos_p, sin_p)
    ob_p = _flash(qa, ka, vt)
    ve_s = proj_s[:, COLS_E_Q + 2 * W_B:COLS_E_Q + 3 * W_B]
    ke_s, ob_s = _moba_sample(seq_rows(proj_s[:, COLS_E_Q:COLS_E_Q + W_B]),
                              seq_rows(proj_s[:, COLS_E_Q + W_B:COLS_E_Q + 2 * W_B]), seq_rows(ve_s),
                              cache_k_e[0], cache_v_e[0], page_table, cos_s, sin_s)
    y = _matmul([(oa_p, oa_s), (ob_p, ob_s.reshape(ms, W_B))], w_out_e[0].astype(BF16), res=x)
    y = _ffn(y, ffn_w1.astype(BF16), ffn_w3.astype(BF16), ffn_w2.astype(BF16), residual=True, norm_g=norm_ffn[0],
             tm=TM_EXPERT if m % TM_EXPERT == 0 else TM_DENSE, tf=ffn_w1.shape[2] // 2)

    cols_o = w_in_o.shape[2]
    cols_pad = -(-cols_o // LANES) * LANES
    w_in_o_p = jnp.zeros((d, cols_pad), BF16).at[:, :cols_o].set(w_in_o[0].astype(BF16))
    proj_o = _matmul([y], w_in_o_p, norm_g=norm_mix[1])
    proj_os = proj_o[mp:]
    ow = _odd_weights(conv_w[0], conv_b[0], rg_wa[0], rg_ba[0], rg_wx[0], rg_bx[0], rg_lambda[0])
    oc_p, buf_p, h_p = _rglru_prompt(proj_o, batch, seq, *ow)
    ps = proj_os.reshape(n_seq, ts, cols_pad)
    cx_s = ps[..., :W_C]
    oc_s, h_s = _rglru_sample(cx_s.swapaxes(0, 1), ps[..., W_C:2 * W_C].swapaxes(0, 1), state_conv_o[0].swapaxes(0, 1),
                              state_h_o[0], past == 0, *ow)
    oc_s = oc_s.swapaxes(0, 1).reshape(ms, W_C)
    buf_s = jnp.concatenate([state_conv_o[0].astype(F32), cx_s], axis=1)[:, ts:]
    bf_row = jnp.zeros((1, LANES), F32).at[0, :H_D].set(fox_bf[0].astype(F32))
    selq, selk = _fox_selectors()
    lf_p, ko_p, vo_p, qa, ka, vt = _fox_prep(proj_o, batch, seq, bf_row, selq, selk)
    od_p = _flash(qa, ka, vt)
    ko_s = proj_os[:, COLS_O_Q + W_D:COLS_O_Q + 2 * W_D]
    vo_s = proj_os[:, COLS_O_Q + 2 * W_D:COLS_O_Q + 3 * W_D]
    lf_s, od_s = _fox_sample(seq_rows(proj_os[:, COLS_O_Q:COLS_O_Q + W_D]), seq_rows(ko_s), seq_rows(vo_s),
                             proj_os[:, COLS_O_F:COLS_O_F + H_D].reshape(n_seq, ts * H_D), fox_bf[0],
                             cache_k_o[0], cache_v_o[0], cache_logf_o[0], page_table)
    y = _matmul([(oc_p, oc_s), (od_p, od_s.reshape(ms, W_D))], w_out_o[0].astype(BF16), res=y)
    tm_e = TM_EXPERT if (m * TOP_K) % TM_EXPERT == 0 else TM_DENSE
    hn, logits = _rmsnorm_router(y, norm_ffn[1], moe_router[0])
    gates, dest, tile_expert, n_used = _moe_dispatch(logits[:, :N_EXPERTS], tm_e)
    pair_pos = dest.reshape(m, TOP_K)
    y_e = _ffn(_scatter_rows(hn, dest, tile_expert.shape[0] * tm_e), moe_w1[0], moe_w3[0], moe_w2[0],
               tile_expert=tile_expert, n_used=n_used, tm=tm_e, tf=512)
    out_p, out_s = _combine_norm(y, jnp.take(y_e, pair_pos[:, 0], axis=0, mode="clip"),
                                 jnp.take(y_e, pair_pos[:, 1], axis=0, mode="clip"), gates, norm_final, mp)

    def heads(a, h):
        return a.reshape(1, n_seq, ts, h, a.shape[-1] // h)

    def heads_t(a):
        return jnp.transpose(a, (0, 3, 1, 2))[None]

    return (out_p.reshape(batch, seq, d), out_s.reshape(n_seq, ts, d),
            s_p[None], s_s[None],
            heads_t(ke_p), heads_t(ve_p), heads(ke_s, H_B), heads(ve_s, H_B),
            buf_p[None], buf_s[None], h_p[None], h_s[None],
            heads_t(ko_p), heads_t(vo_p), jnp.transpose(lf_p, (0, 2, 1))[None],
            heads(ko_s, H_D), heads(vo_s, H_D), lf_s.reshape(1, n_seq, ts, H_D))
```
